```python
import jax, jax.numpy as jnp
from jax import lax
import numpy as np

D_MODEL = 4096
BATCH = 4
SEQ = 4096
DEPTH = 4

MEM_LEN = 256
GRID_W = 64
N_BRANCH = 4
BRANCH_W = D_MODEL // N_BRANCH

RWKV_HEAD = 64
RWKV_HEADS = BRANCH_W // RWKV_HEAD
LORA_W = 64
LORA_A = 64
SHIFT_TAPS = 3
LNX_EPS = 64e-5

CHUNK = 128
SG_GROUPS = 8
SG_CH = BRANCH_W // SG_GROUPS

NA_HEAD = 64
NA_HEADS = BRANCH_W // NA_HEAD
MAX_WIN_ROWS = 8
WIN_COLS = 16

MEM_HEADS = 4
MEM_HEAD = BRANCH_W // MEM_HEADS

A_SHIFT_W = 3 * BRANCH_W + 2 * LORA_W + 2 * LORA_A
A_W = A_SHIFT_W + BRANCH_W
B_W = 3 * BRANCH_W
C_W = 4 * BRANCH_W
M_W = 2 * BRANCH_W
GATE_W = N_BRANCH * D_MODEL
IN_W = A_W + B_W + C_W + M_W + GATE_W

NEG_INF = -1e30

kernel_name = 'hybrid_bidir_rwkv7_gmlp_natten_memxattn'


def _rms_norm(x, g, eps=1e-6):
    xf = x.astype(jnp.float32)
    y = xf * lax.rsqrt(jnp.mean(xf * xf, axis=-1, keepdims=True) + eps)
    return y.astype(x.dtype) * g


def _layer_norm(x, g, b, eps=1e-5):
    xf = x.astype(jnp.float32)
    mu = jnp.mean(xf, axis=-1, keepdims=True)
    var = jnp.mean(jnp.square(xf - mu), axis=-1, keepdims=True)
    return ((xf - mu) * lax.rsqrt(var + eps)).astype(x.dtype) * g + b


def _centred_dwconv(x, w):
    K = w.shape[0]
    pad = K // 2
    T = x.shape[1]
    xp = jnp.pad(x, ((0, 0), (pad, pad), (0, 0)))
    out = xp[:, 0:T] * w[0]
    for j in range(1, K):
        out = out + xp[:, j:j + T] * w[j]
    return out


def _wkv7_bidirectional(r, decay, k_t, v, kk, a):
    f32 = jnp.float32
    r, decay, k_t, v, kk, a = (t.astype(f32) for t in (r, decay, k_t, v, kk, a))
    B, T, H, N = r.shape
    z = -kk
    b = kk[:, :, None] * a

    def dirs(x_f, x_b):
        return jnp.moveaxis(jnp.stack([x_f, jnp.flip(x_b, axis=1)], axis=0), 2, 0)

    seq = (dirs(r, r), dirs(decay[:, :, 0], decay[:, :, 1]), dirs(k_t[:, :, 0], k_t[:, :, 1]),
           dirs(v, v), dirs(z, z), dirs(b[:, :, 0], b[:, :, 1]))

    def step(S, inp):
        r_t, w_t, k_tt, v_t, z_t, b_t = inp
        Sz = jnp.einsum('dbhij,dbhj->dbhi', S, z_t)
        S = S * w_t[..., None, :] + Sz[..., :, None] * b_t[..., None, :] + v_t[..., :, None] * k_tt[..., None, :]
        return S, jnp.einsum('dbhij,dbhj->dbhi', S, r_t)

    S0 = jnp.zeros((2, B, H, N, N), f32)
    _, ys = lax.scan(step, S0, seq)
    y = ys[:, 0] + jnp.flip(ys[:, 1], axis=0)
    return jnp.moveaxis(y, 0, 1)


def _rwkv7_branch(xn, w_a, conv, w_up, w0, a_up, a0, k_k, k_a, r_k, lnx_w, lnx_b):
    B, T, _ = xn.shape
    BW = BRANCH_W
    h = xn @ w_a
    hs = _centred_dwconv(h[..., :A_SHIFT_W], conv)
    g = h[..., A_SHIFT_W:]
    r, k, v, wd, ad = jnp.split(hs, [BW, 2 * BW, 3 * BW, 3 * BW + 2 * LORA_W], axis=-1)
    wd = wd.reshape(B, T, 2, LORA_W)
    ad = ad.reshape(B, T, 2, LORA_A)
    w_raw = (w0 + jnp.einsum('btzr,zrc->btzc', jnp.tanh(wd), w_up)).astype(jnp.float32)
    decay = jnp.exp(-jnp.exp(-jax.nn.softplus(-w_raw) - 0.5))
    a = jax.nn.sigmoid(a0 + jnp.einsum('btzr,zrc->btzc', ad, a_up))
    heads = lambda t: t.reshape(*t.shape[:-1], RWKV_HEADS, RWKV_HEAD)
    kk = heads(k * k_k).astype(jnp.float32)
    kk = kk / jnp.maximum(jnp.sqrt(jnp.sum(kk * kk, axis=-1, keepdims=True)), 1e-12)
    k_t = k[:, :, None, :] * (1.0 + (a - 1.0) * k_a)
    wkv = _wkv7_bidirectional(heads(r), heads(decay), heads(k_t), heads(v), kk, heads(a))
    mu = jnp.mean(wkv, axis=-1, keepdims=True)
    var = jnp.mean(jnp.square(wkv - mu), axis=-1, keepdims=True)
    gn = ((wkv - mu) * lax.rsqrt(var + LNX_EPS)).reshape(B, T, BW).astype(xn.dtype) * lnx_w + lnx_b
    bonus = jnp.einsum('bthn,btzhn,hn->bth', heads(r), heads(k_t), r_k)[..., None] * heads(v)
    y = gn + bonus.reshape(B, T, BW)
    return y * jax.nn.silu(g)


def _spatial_gating_branch(xn, w_b, ln_g, ln_b, w_s, b_s):
    B, T, _ = xn.shape
    u, v, g = jnp.split(xn @ w_b, 3, axis=-1)
    u = jax.nn.gelu(u)
    v = _layer_norm(jax.nn.gelu(v), ln_g, ln_b)
    vc = v.reshape(B, T // CHUNK, CHUNK, SG_GROUPS, SG_CH)
    sv = jnp.einsum('gpq,bnqgc->bnpgc', w_s, vc) + b_s.T[:, :, None]
    return u * sv.reshape(B, T, BRANCH_W) * jax.nn.silu(g)


def _natten_col_tables():
    n_cb = GRID_W // WIN_COLS
    span = 2 * WIN_COLS
    cs = np.clip(WIN_COLS * np.arange(n_cb) - WIN_COLS // 2, 0, GRID_W - span)
    col_idx = cs[:, None] + np.arange(span)[None, :]
    qcol = (WIN_COLS * np.arange(n_cb))[:, None] + np.arange(WIN_COLS)[None, :]
    sj = np.clip(qcol - WIN_COLS // 2, 0, GRID_W - WIN_COLS)
    kc = col_idx[:, None, :]
    valid = (kc >= sj[..., None]) & (kc < sj[..., None] + WIN_COLS)
    dc = np.clip(kc - qcol[..., None], -(WIN_COLS - 1), WIN_COLS - 1) + WIN_COLS - 1
    return col_idx, valid, dc


def _neighbourhood_attention(q, k, v, rpb):
    B, T, H, d = q.shape
    rows = T // GRID_W
    kh = min(MAX_WIN_ROWS, rows)
    col_idx, valid, dc = _natten_col_tables()
    n_cb, span = col_idx.shape
    to_grid = lambda t: t.reshape(B, rows, GRID_W, H, d).transpose(0, 3, 1, 2, 4)
    qg, kg, vg = to_grid(q), to_grid(k), to_grid(v)
    scale = d ** -0.5

    def row_fn(i):
        si = jnp.clip(i - kh // 2, 0, rows - kh)
        k_rows = lax.dynamic_slice_in_dim(kg, si, kh, axis=2)
        v_rows = lax.dynamic_slice_in_dim(vg, si, kh, axis=2)
        k_blk = k_rows[:, :, :, col_idx]
        v_blk = v_rows[:, :, :, col_idx]
        q_i = lax.dynamic_index_in_dim(qg, i, axis=2, keepdims=False).reshape(B, H, n_cb, WIN_COLS, d)
        s = jnp.einsum('bhcpd,bhrcmd->bhcprm', q_i, k_blk).astype(jnp.float32) * scale
        rel_r = si + jnp.arange(kh) - i + (MAX_WIN_ROWS - 1)
        bias = rpb[:, rel_r][:, :, dc].transpose(0, 2, 3, 1, 4)
        s = jnp.where(valid[:, :, None, :], s + bias.astype(jnp.float32), NEG_INF)
        p = jax.nn.softmax(s.reshape(B, H, n_cb, WIN_COLS, kh * span), axis=-1)
        p = p.reshape(s.shape).astype(v.dtype)
        o = jnp.einsum('bhcprm,bhrcmd->bhcpd', p, v_blk)
        return o.reshape(B, H, GRID_W, d)

    out = lax.map(row_fn, jnp.arange(rows))
    return out.transpose(1, 0, 3, 2, 4).reshape(B, T, H * d)


def _neighbourhood_branch(xn, w_c, q_norm, k_norm, rpb):
    B, T, _ = xn.shape
    q, k, v, g = jnp.split(xn @ w_c, 4, axis=-1)
    heads = lambda t: t.reshape(B, T, NA_HEADS, NA_HEAD)
    q = _rms_norm(heads(q), q_norm)
    k = _rms_norm(heads(k), k_norm)
    o = _neighbourhood_attention(q, k, heads(v), rpb)
    return o * jax.nn.silu(g)


def _memory_branch(xn, mem, w_m, m_norm_g, w_kv, q_norm, k_norm):
    B, T, _ = xn.shape
    M = mem.shape[1]
    q, g = jnp.split(xn @ w_m, 2, axis=-1)
    q = _rms_norm(q.reshape(B, T, MEM_HEADS, MEM_HEAD), q_norm)
    k, v = jnp.split(_rms_norm(mem, m_norm_g) @ w_kv, 2, axis=-1)
    k = _rms_norm(k.reshape(B, M, MEM_HEADS, MEM_HEAD), k_norm)
    v = v.reshape(B, M, MEM_HEADS, MEM_HEAD)
    s = jnp.einsum('bthd,bmhd->bhtm', q, k).astype(jnp.float32) * (MEM_HEAD ** -0.5)
    p = jax.nn.softmax(s, axis=-1).astype(v.dtype)
    o = jnp.einsum('bhtm,bmhd->bthd', p, v).reshape(B, T, BRANCH_W)
    return o * jax.nn.silu(g)


def setup_inputs(seed: int = 0) -> dict:
    key = jax.random.key(seed)
    ks = jax.random.split(key, 27)
    f32 = jnp.float32
    L, BW = DEPTH, BRANCH_W
    nrm = lambda k, shape, s: jax.random.normal(k, shape, f32) * s
    return {
        'x': nrm(ks[0], (BATCH, SEQ, D_MODEL), 1.0),
        'mem': nrm(ks[1], (BATCH, MEM_LEN, D_MODEL), 1.0),
        'norm_g': 1.0 + nrm(ks[2], (L, D_MODEL), 0.02),
        'w_in': nrm(ks[3], (L, D_MODEL, IN_W), D_MODEL ** -0.5),
        'a_conv': jnp.array([0.25, 0.5, 0.25], f32)[None, :, None] + nrm(ks[4], (L, SHIFT_TAPS, A_SHIFT_W), 0.05),
        'a_w_up': nrm(ks[5], (L, 2, LORA_W, BW), 0.1),
        'a_w0': jax.random.uniform(ks[6], (L, 2, BW), f32, minval=-6.0, maxval=0.0),
        'a_a_up': nrm(ks[7], (L, 2, LORA_A, BW), LORA_A ** -0.5),
        'a_a0': nrm(ks[8], (L, 2, BW), 0.1),
        'a_k_k': 0.85 + nrm(ks[9], (L, BW), 0.05),
        'a_k_a': 1.0 + nrm(ks[10], (L, BW), 0.05),
        'a_r_k': nrm(ks[11], (L, RWKV_HEADS, RWKV_HEAD), 0.1),
        'a_lnx_w': 1.0 + nrm(ks[12], (L, BW), 0.02),
        'a_lnx_b': nrm(ks[13], (L, BW), 0.02),
        'b_ln_g': 1.0 + nrm(ks[14], (L, BW), 0.02),
        'b_ln_b': nrm(ks[15], (L, BW), 0.02),
        'b_w_s': nrm(ks[16], (L, SG_GROUPS, CHUNK, CHUNK), CHUNK ** -0.5),
        'b_b_s': 1.0 + nrm(ks[17], (L, SG_GROUPS, CHUNK), 0.1),
        'c_q_norm': 1.0 + nrm(ks[18], (L, NA_HEAD), 0.02),
        'c_k_norm': 1.0 + nrm(ks[19], (L, NA_HEAD), 0.02),
        'c_rpb': nrm(ks[20], (L, NA_HEADS, 2 * MAX_WIN_ROWS - 1, 2 * WIN_COLS - 1), 0.1),
        'm_norm_g': 1.0 + nrm(ks[21], (L, D_MODEL), 0.02),
        'm_w_kv': nrm(ks[22], (L, D_MODEL, 2 * BW), D_MODEL ** -0.5),
        'm_q_norm': 1.0 + nrm(ks[23], (L, MEM_HEAD), 0.02),
        'm_k_norm': 1.0 + nrm(ks[24], (L, MEM_HEAD), 0.02),
        'w_branch': nrm(ks[25], (L, N_BRANCH, BW, D_MODEL), BW ** -0.5),
        'w_out': nrm(ks[26], (L, D_MODEL, D_MODEL), (2 * D_MODEL) ** -0.5),
    }


def reference(x, mem, norm_g, w_in, a_conv, a_w_up, a_w0, a_a_up, a_a0, a_k_k, a_k_a, a_r_k,
              a_lnx_w, a_lnx_b, b_ln_g, b_ln_b, b_w_s, b_b_s, c_q_norm, c_k_norm, c_rpb,
              m_norm_g, m_w_kv, m_q_norm, m_k_norm, w_branch, w_out):
    o1 = A_W
    o2 = o1 + B_W
    o3 = o2 + C_W
    o4 = o3 + M_W
    for l in range(DEPTH):
        xn = _rms_norm(x, norm_g[l])
        w = w_in[l]
        ys = (
            _rwkv7_branch(xn, w[:, :o1], a_conv[l], a_w_up[l], a_w0[l], a_a_up[l], a_a0[l],
                          a_k_k[l], a_k_a[l], a_r_k[l], a_lnx_w[l], a_lnx_b[l]),
            _spatial_gating_branch(xn, w[:, o1:o2], b_ln_g[l], b_ln_b[l], b_w_s[l], b_b_s[l]),
            _neighbourhood_branch(xn, w[:, o2:o3], c_q_norm[l], c_k_norm[l], c_rpb[l]),
            _memory_branch(xn, mem, w[:, o3:o4], m_norm_g[l], m_w_kv[l], m_q_norm[l], m_k_norm[l]),
        )
        merged = jnp.zeros_like(x)
        for n in range(N_BRANCH):
            gate = jax.nn.sigmoid(xn @ w[:, o4 + n * D_MODEL:o4 + (n + 1) * D_MODEL])
            merged = merged + gate * (ys[n] @ w_branch[l, n])
        x = x + merged @ w_out[l]
    return x
```

```python
import functools

import numpy as np
import jax
import jax.numpy as jnp
from jax import lax
from jax.experimental import pallas as pl
from jax.experimental.pallas import tpu as pltpu

F32 = jnp.float32
BF16 = jnp.bfloat16
HIGHEST = lax.Precision.HIGHEST

VMEM_LIMIT_BYTES = 56 * 1024 * 1024
LANES = 128
BF16_SUBLANES = 16

BRANCH_W = 1024
RWKV_HEAD = 64
LORA = 64
A_SHIFT_W = 3 * BRANCH_W + 4 * LORA
LNX_EPS = 64e-5
SG_CHUNK = 128
SG_GROUPS = 8
NA_HEAD = 64
NA_ROWS = 8
NA_COLS = 16
GRID_W = 64
MEM_HEADS = 4
MEM_HEAD = 256
WKV_CHUNK = 64
NEG_INF = -1e30


def _cparams(*sem):
    return pltpu.CompilerParams(dimension_semantics=sem, vmem_limit_bytes=VMEM_LIMIT_BYTES)


def _sigmoid(x):
    return 1.0 / (1.0 + jnp.exp(-x))


def _silu(x):
    return x * _sigmoid(x)


def _softplus(x):
    return jnp.maximum(x, 0.0) + jnp.log(1.0 + jnp.exp(-jnp.abs(x)))


def _gelu_tanh(x):
    c = np.float32(np.sqrt(2.0 / np.pi))
    return 0.5 * x * (1.0 + jnp.tanh(c * (x + 0.044715 * (x * x * x))))


def _dot(a, b, precision=None):
    return jnp.dot(a, b, preferred_element_type=F32, precision=precision)


def _dot_nt(a, b, precision=None):
    return lax.dot_general(a, b, (((1,), (1,)), ((), ())), preferred_element_type=F32,
                           precision=precision)


def _dot_tn(a, b, precision=None):
    return lax.dot_general(a, b, (((0,), (0,)), ((), ())), preferred_element_type=F32,
                           precision=precision)


def _head_block_ones(head):
    shift = int(np.log2(head))
    r = lax.broadcasted_iota(jnp.int32, (LANES, LANES), 0) >> shift
    c = lax.broadcasted_iota(jnp.int32, (LANES, LANES), 1) >> shift
    return (r == c).astype(F32)


def _head_sum(x, ones_bd):
    parts = [_dot(x[:, j:j + LANES], ones_bd, HIGHEST) for j in range(0, x.shape[1], LANES)]
    return parts[0] if len(parts) == 1 else jnp.concatenate(parts, axis=1)


def _rmsnorm_kernel(x_ref, g_ref, o_ref, *, eps):
    x = x_ref[...].astype(F32)
    ms = jnp.mean(x * x, axis=-1, keepdims=True)
    o_ref[...] = (x * lax.rsqrt(ms + eps) * g_ref[...]).astype(o_ref.dtype)


def _rmsnorm(x2d, g, out_dtype, tm=256, eps=1e-6):
    m, d = x2d.shape
    tm = min(tm, m)
    return pl.pallas_call(
        functools.partial(_rmsnorm_kernel, eps=eps),
        grid=(m // tm,),
        in_specs=[pl.BlockSpec((tm, d), lambda i: (i, 0)), pl.BlockSpec((1, d), lambda i: (0, 0))],
        out_specs=pl.BlockSpec((tm, d), lambda i: (i, 0)),
        out_shape=jax.ShapeDtypeStruct((m, d), out_dtype),
        compiler_params=_cparams("parallel"),
        name="rmsnorm",
    )(x2d, g.reshape(1, d))


def _mm_kernel(a_ref, b_ref, o_ref, *, act):
    acc = _dot(a_ref[...], b_ref[...])
    if act == "sigmoid":
        acc = _sigmoid(acc)
    o_ref[...] = acc.astype(o_ref.dtype)


def _matmul(a, b, out_dtype, tm, tn, act=None, name="matmul"):
    m, k = a.shape
    n = b.shape[1]
    tm, tn = min(tm, m), min(tn, n)
    return pl.pallas_call(
        functools.partial(_mm_kernel, act=act),
        grid=(m // tm, n // tn),
        in_specs=[pl.BlockSpec((tm, k), lambda i, j: (i, 0)), pl.BlockSpec((k, tn), lambda i, j: (0, j))],
        out_specs=pl.BlockSpec((tm, tn), lambda i, j: (i, j)),
        out_shape=jax.ShapeDtypeStruct((m, n), out_dtype),
        compiler_params=_cparams("parallel", "parallel"),
        name=name,
    )(a, b)


def _mm_residual_kernel(a_ref, b_ref, x_ref, o_ref):
    o_ref[...] = x_ref[...] + _dot(a_ref[...], b_ref[...])


def _matmul_residual(a, b, x, tm, tn):
    m, k = a.shape
    n = b.shape[1]
    tm, tn = min(tm, m), min(tn, n)
    return pl.pallas_call(
        _mm_residual_kernel,
        grid=(m // tm, n // tn),
        in_specs=[pl.BlockSpec((tm, k), lambda i, j: (i, 0)), pl.BlockSpec((k, tn), lambda i, j: (0, j)),
                  pl.BlockSpec((tm, tn), lambda i, j: (i, j))],
        out_specs=pl.BlockSpec((tm, tn), lambda i, j: (i, j)),
        out_shape=jax.ShapeDtypeStruct((m, n), x.dtype),
        compiler_params=_cparams("parallel", "parallel"),
        name="out_proj_residual",
    )(a, b, x)


def _merge_kernel(y0, y1, y2, y3, wb_ref, g0, g1, g2, g3, o_ref):
    acc = None
    for n, (y, g) in enumerate(((y0, g0), (y1, g1), (y2, g2), (y3, g3))):
        term = g[...].astype(F32) * _dot(y[...], wb_ref[n])
        acc = term if acc is None else acc + term
    o_ref[...] = acc.astype(o_ref.dtype)


def _merge(ys, wb, gates, tm, tn):
    m, bw = ys[0].shape
    d = wb.shape[2]
    tm, tn = min(tm, m), min(tn, d)
    nj = d // tn
    y_spec = pl.BlockSpec((tm, bw), lambda i, j: (i, 0))
    g_specs = [pl.BlockSpec((tm, tn), functools.partial(lambda i, j, n: (i, n * nj + j), n=n))
               for n in range(4)]
    return pl.pallas_call(
        _merge_kernel,
        grid=(m // tm, nj),
        in_specs=[y_spec] * 4 + [pl.BlockSpec((4, bw, tn), lambda i, j: (0, 0, j))] + g_specs,
        out_specs=pl.BlockSpec((tm, tn), lambda i, j: (i, j)),
        out_shape=jax.ShapeDtypeStruct((m, d), BF16),
        compiler_params=_cparams("parallel", "parallel"),
        name="gated_merge",
    )(*ys, wb, gates, gates, gates, gates)


def _rwkv_prep_kernel(h_ref, hp_ref, hn_ref, conv_ref, wup_ref, w0_ref, aup_ref, a0_ref, kk_ref, ka_ref,
                      rk_ref, r_o, v_o, kk_o, lw0_o, lw1_o, kt0_o, kt1_o, b0_o, b1_o, bonus_o):
    i = pl.program_id(1)
    nt = pl.num_programs(1)
    h = h_ref[0].astype(F32)
    tt = h.shape[0]
    prev_row = hp_ref[0, BF16_SUBLANES - 1:BF16_SUBLANES, :].astype(F32) * (i > 0).astype(F32)
    next_row = hn_ref[0, 0:1, :].astype(F32) * (i < nt - 1).astype(F32)
    rows = lax.broadcasted_iota(jnp.int32, (tt, 1), 0)
    h_dn = jnp.where(rows == 0, prev_row, pltpu.roll(h, 1, 0))
    h_up = jnp.where(rows == tt - 1, next_row, pltpu.roll(h, tt - 1, 0))
    hs = h_dn * conv_ref[0:1, :] + h * conv_ref[1:2, :] + h_up * conv_ref[2:3, :]

    bw = BRANCH_W
    r = hs[:, 0:bw]
    k = hs[:, bw:2 * bw]
    v = hs[:, 2 * bw:3 * bw]
    wd = jnp.tanh(hs[:, 3 * bw:3 * bw + 2 * LORA])
    ad = hs[:, 3 * bw + 2 * LORA:3 * bw + 4 * LORA]

    ones_bd = _head_block_ones(RWKV_HEAD)
    kkr = k * kk_ref[...]
    nrm = jnp.sqrt(_head_sum(kkr * kkr, ones_bd))
    kk = kkr / jnp.maximum(nrm, 1e-12)

    kts = []
    for z, (lw_o, kt_o, b_o) in enumerate(((lw0_o, kt0_o, b0_o), (lw1_o, kt1_o, b1_o))):
        w_raw = w0_ref[z:z + 1, :] + _dot(wd, wup_ref[z], HIGHEST)
        lw_o[0] = -jnp.exp(-_softplus(-w_raw) - 0.5)
        a = _sigmoid(a0_ref[z:z + 1, :] + _dot(ad, aup_ref[z], HIGHEST))
        kt = k * (1.0 + (a - 1.0) * ka_ref[...])
        kt_o[0] = kt
        b_o[0] = kk * a
        kts.append(kt)
    r_o[0] = r
    v_o[0] = v
    kk_o[0] = kk
    bonus_o[0] = _head_sum(r * (kts[0] + kts[1]) * rk_ref[...], ones_bd) * v


def _rwkv_prep(h_shift, conv, wup_pad, w0, aup_pad, a0, k_k, k_a, r_k, tt=256):
    bsz, t, w = h_shift.shape
    tt = min(tt, t)
    nt = t // tt
    hb = tt // BF16_SUBLANES
    n_halo = t // BF16_SUBLANES
    row = lambda a: a.reshape(1, -1)
    vec_spec = pl.BlockSpec((1, BRANCH_W), lambda b, i: (0, 0))
    out_spec = pl.BlockSpec((1, tt, BRANCH_W), lambda b, i: (b, i, 0))
    out_sds = jax.ShapeDtypeStruct((bsz, t, BRANCH_W), F32)
    return pl.pallas_call(
        _rwkv_prep_kernel,
        grid=(bsz, nt),
        in_specs=[
            pl.BlockSpec((1, tt, w), lambda b, i: (b, i, 0)),
            pl.BlockSpec((1, BF16_SUBLANES, w), lambda b, i: (b, jnp.maximum(i * hb - 1, 0), 0)),
            pl.BlockSpec((1, BF16_SUBLANES, w), lambda b, i: (b, jnp.minimum((i + 1) * hb, n_halo - 1), 0)),
            pl.BlockSpec((3, w), lambda b, i: (0, 0)),
            pl.BlockSpec((2, 2 * LORA, BRANCH_W), lambda b, i: (0, 0, 0)),
            pl.BlockSpec((2, BRANCH_W), lambda b, i: (0, 0)),
            pl.BlockSpec((2, 2 * LORA, BRANCH_W), lambda b, i: (0, 0, 0)),
            pl.BlockSpec((2, BRANCH_W), lambda b, i: (0, 0)),
            vec_spec, vec_spec, vec_spec,
        ],
        out_specs=[out_spec] * 10,
        out_shape=[out_sds] * 10,
        compiler_params=_cparams("parallel", "parallel"),
        name="rwkv_prep",
    )(h_shift, h_shift, h_shift, conv, wup_pad, w0, aup_pad, a0, row(k_k), row(k_a), row(r_k))


def _wkv_masks():
    c = WKV_CHUNK
    row = lax.broadcasted_iota(jnp.int32, (2 * c, 2 * c), 0)
    col = lax.broadcasted_iota(jnp.int32, (2 * c, 2 * c), 1)
    same = (row >> 6) == (col >> 6)
    rt, ct = row & (c - 1), col & (c - 1)
    f = lambda m: jnp.where(same & m, 1.0, 0.0).astype(F32)
    return {
        "bd": f(rt == rt),
        "eye": f(rt == ct),
        "strict": (f(ct < rt), f(ct > rt)),
        "incl": (f(ct <= rt), f(ct >= rt)),
    }


def _wkv_chunk(lw, r, kt, v, kk, b, s_bd, cum, masks, direction):
    c = WKV_CHUNK
    lane = lax.broadcasted_iota(jnp.int32, (1, LANES), 1)
    m0 = (lane < RWKV_HEAD).astype(F32)
    m1 = 1.0 - m0
    pair = lambda x: jnp.concatenate([x * m0, x * m1], axis=0)
    dup = lambda x: jnp.concatenate([x, x], axis=0)

    cl = _dot(cum, lw, HIGHEST)
    tot = cl[c - 1:c, :] if direction == 0 else cl[0:1, :]
    e_neg = jnp.exp(-cl)
    e_end = jnp.exp(tot - cl)
    zt = -kk * jnp.exp(cl - lw)
    rt = r * jnp.exp(cl)
    bm, km = pair(b * e_neg), pair(kt * e_neg)
    bh, kh = pair(b * e_end), pair(kt * e_end)
    vp = pair(v)
    zr = jnp.concatenate([dup(zt), dup(rt)], axis=0)

    scores = _dot_nt(zr, jnp.concatenate([bm, km], axis=0))
    strict, incl = masks["strict"][direction], masks["incl"][direction]
    l_zb = scores[0:2 * c, 0:2 * c] * strict
    a_zk = scores[0:2 * c, 2 * c:4 * c] * strict
    a_rb = scores[2 * c:4 * c, 0:2 * c] * incl
    a_rk = scores[2 * c:4 * c, 2 * c:4 * c] * incl

    inv = masks["eye"] + l_zb
    p = l_zb
    for _ in range(5):
        p = _dot(p, p)
        inv = inv + _dot(inv, p)

    zr_s = _dot_nt(zr, s_bd)
    rhs = zr_s[0:2 * c] * masks["bd"] + _dot(a_zk, vp)
    u = _dot(inv, rhs)
    uv = jnp.concatenate([u, vp], axis=0)
    yp = zr_s[2 * c:4 * c] * masks["bd"] + _dot(jnp.concatenate([a_rb, a_rk], axis=1), uv)
    y = yp[0:c] + yp[c:2 * c]
    s_new = s_bd * jnp.exp(tot) + _dot_tn(uv, jnp.concatenate([bh, kh], axis=0))
    return y, s_new


def _wkv_kernel(r_f, v_f, kk_f, lw_f, kt_f, b_f, r_b, v_b, kk_b, lw_b, kt_b, b_b, yf_o, yb_o, sf_ref, sb_ref):
    @pl.when(pl.program_id(2) == 0)
    def _():
        sf_ref[...] = jnp.zeros_like(sf_ref)
        sb_ref[...] = jnp.zeros_like(sb_ref)

    c = WKV_CHUNK
    n_chunks = r_f.shape[1] // c
    masks = _wkv_masks()
    row = lax.broadcasted_iota(jnp.int32, (c, c), 0)
    col = lax.broadcasted_iota(jnp.int32, (c, c), 1)
    cum_f = (col <= row).astype(F32)
    cum_b = (col >= row).astype(F32)

    def body(ci, carry):
        of = pl.multiple_of(ci * c, c)
        ob = pl.multiple_of((n_chunks - 1 - ci) * c, c)
        ld = lambda ref, o: ref[0, pl.ds(o, c), :]
        y, s = _wkv_chunk(ld(lw_f, of), ld(r_f, of), ld(kt_f, of), ld(v_f, of), ld(kk_f, of), ld(b_f, of),
                          sf_ref[...], cum_f, masks, 0)
        yf_o[0, pl.ds(of, c), :] = y
        sf_ref[...] = s
        y, s = _wkv_chunk(ld(lw_b, ob), ld(r_b, ob), ld(kt_b, ob), ld(v_b, ob), ld(kk_b, ob), ld(b_b, ob),
                          sb_ref[...], cum_b, masks, 1)
        yb_o[0, pl.ds(ob, c), :] = y
        sb_ref[...] = s
        return carry

    lax.fori_loop(0, n_chunks, body, 0)


def _wkv(r, v, kk, lw0, lw1, kt0, kt1, b0, b1, tb=512):
    bsz, t, w = r.shape
    tb = min(tb, t)
    nb = t // tb
    fwd = pl.BlockSpec((1, tb, LANES), lambda b, h, g: (b, g, h))
    bwd = pl.BlockSpec((1, tb, LANES), lambda b, h, g: (b, nb - 1 - g, h))
    sds = jax.ShapeDtypeStruct((bsz, t, w), F32)
    return pl.pallas_call(
        _wkv_kernel,
        grid=(bsz, w // LANES, nb),
        in_specs=[fwd] * 6 + [bwd] * 6,
        out_specs=[fwd, bwd],
        out_shape=[sds, sds],
        scratch_shapes=[pltpu.VMEM((LANES, LANES), F32), pltpu.VMEM((LANES, LANES), F32)],
        compiler_params=_cparams("parallel", "parallel", "arbitrary"),
        name="wkv7_chunked",
    )(r, v, kk, lw0, kt0, b0, r, v, kk, lw1, kt1, b1)


def _rwkv_post_kernel(yf_ref, yb_ref, bonus_ref, g_ref, lw_ref, lb_ref, o_ref):
    ones_bd = _head_block_ones(RWKV_HEAD)
    wkv = yf_ref[0] + yb_ref[0]
    inv_n = 1.0 / RWKV_HEAD
    mu = _head_sum(wkv, ones_bd) * inv_n
    d = wkv - mu
    var = _head_sum(d * d, ones_bd) * inv_n
    gn = d * lax.rsqrt(var + LNX_EPS) * lw_ref[...] + lb_ref[...]
    o_ref[0] = ((gn + bonus_ref[0]) * _silu(g_ref[0].astype(F32))).astype(o_ref.dtype)


def _rwkv_post(yf, yb, bonus, g, lnx_w, lnx_b, tt=512):
    bsz, t, w = yf.shape
    tt = min(tt, t)
    spec = pl.BlockSpec((1, tt, w), lambda b, i: (b, i, 0))
    vec = pl.BlockSpec((1, w), lambda b, i: (0, 0))
    return pl.pallas_call(
        _rwkv_post_kernel,
        grid=(bsz, t // tt),
        in_specs=[spec, spec, spec, spec, vec, vec],
        out_specs=spec,
        out_shape=jax.ShapeDtypeStruct((bsz, t, w), BF16),
        compiler_params=_cparams("parallel", "parallel"),
        name="rwkv_post",
    )(yf, yb, bonus, g, lnx_w.reshape(1, w), lnx_b.reshape(1, w))


def _sgu_kernel(h_ref, lg_ref, lb_ref, ws_ref, bs_ref, o_ref):
    bw = BRANCH_W
    tt = h_ref.shape[1]
    u = _gelu_tanh(h_ref[0, :, 0:bw].astype(F32))
    vv = _gelu_tanh(h_ref[0, :, bw:2 * bw].astype(F32))
    g = h_ref[0, :, 2 * bw:3 * bw].astype(F32)
    mu = jnp.mean(vv, axis=-1, keepdims=True)
    d = vv - mu
    var = jnp.mean(d * d, axis=-1, keepdims=True)
    vn = (d * lax.rsqrt(var + 1e-5) * lg_ref[...] + lb_ref[...]).astype(BF16)
    gate = u * _silu(g)
    for ck in range(tt // SG_CHUNK):
        rs = slice(ck * SG_CHUNK, (ck + 1) * SG_CHUNK)
        for grp in range(SG_GROUPS):
            cs = slice(grp * LANES, (grp + 1) * LANES)
            sv = _dot(ws_ref[grp], vn[rs, cs]) + bs_ref[:, cs]
            o_ref[0, rs, cs] = (gate[rs, cs] * sv).astype(o_ref.dtype)


def _sgu(h_b, ln_g, ln_b, w_s, bs_cols, tt=256):
    bsz, t, w3 = h_b.shape
    bw = BRANCH_W
    tt = min(tt, t)
    vec = pl.BlockSpec((1, bw), lambda b, i: (0, 0))
    return pl.pallas_call(
        _sgu_kernel,
        grid=(bsz, t // tt),
        in_specs=[pl.BlockSpec((1, tt, w3), lambda b, i: (b, i, 0)), vec, vec,
                  pl.BlockSpec((SG_GROUPS, SG_CHUNK, SG_CHUNK), lambda b, i: (0, 0, 0)),
                  pl.BlockSpec((SG_CHUNK, bw), lambda b, i: (0, 0))],
        out_specs=pl.BlockSpec((1, tt, bw), lambda b, i: (b, i, 0)),
        out_shape=jax.ShapeDtypeStruct((bsz, t, bw), BF16),
        compiler_params=_cparams("parallel", "parallel"),
        name="spatial_gating",
    )(h_b, ln_g.reshape(1, bw), ln_b.reshape(1, bw), w_s, bs_cols)


def _natten_bias_table(rpb):
    p = np.arange(GRID_W)[:, None]
    m = np.arange(GRID_W)[None, :]
    sj = np.clip(p - NA_COLS // 2, 0, GRID_W - NA_COLS)
    valid = (m >= sj) & (m < sj + NA_COLS)
    dc = np.clip(m - p, -(NA_COLS - 1), NA_COLS - 1) + NA_COLS - 1
    rel = np.arange(NA_ROWS)[:, None] + np.arange(NA_ROWS)[None, :]
    tab = rpb[:, rel][:, :, :, dc]
    tab = jnp.where(valid[None, None, None], tab, NEG_INF)
    tab = tab.transpose(0, 1, 3, 2, 4)
    return tab.reshape(rpb.shape[0], NA_ROWS, GRID_W, NA_ROWS * GRID_W).astype(F32)


def _natten_kernel(q_ref, k_ref, v_ref, g_ref, qn_ref, kn_ref, bias_ref, o_ref, qs_ref, ks_ref):
    t = q_ref.shape[1]
    n_rows = t // GRID_W
    win = NA_ROWS * GRID_W
    ones_bd = _head_block_ones(NA_HEAD)
    lane = lax.broadcasted_iota(jnp.int32, (1, LANES), 1)
    m0 = lane < NA_HEAD

    def norm(x_ref, gain_ref, scale):
        x = x_ref[0].astype(F32)
        ms = _head_sum(x * x, ones_bd) * (1.0 / NA_HEAD)
        return x * lax.rsqrt(ms + 1e-6) * (gain_ref[...] * scale)

    qs_ref[...] = norm(q_ref, qn_ref, NA_HEAD ** -0.5).astype(BF16)
    ks_ref[...] = norm(k_ref, kn_ref, 1.0).astype(BF16)

    def body(i, carry):
        si = jnp.clip(i - NA_ROWS // 2, 0, n_rows - NA_ROWS)
        start = si - i + (NA_ROWS - 1)
        qo = pl.multiple_of(i * GRID_W, GRID_W)
        ko = pl.multiple_of(si * GRID_W, GRID_W)
        q = qs_ref[pl.ds(qo, GRID_W), :]
        kw = ks_ref[pl.ds(ko, win), :]
        vw = v_ref[0, pl.ds(ko, win), :]
        outs = []
        for h in range(2):
            qh = jnp.where(m0 if h == 0 else jnp.logical_not(m0), q, jnp.zeros_like(q))
            s = _dot_nt(qh, kw) + bias_ref[h, start]
            s = s - jnp.max(s, axis=-1, keepdims=True)
            e = jnp.exp(s)
            p = e / jnp.sum(e, axis=-1, keepdims=True)
            outs.append(_dot(p.astype(BF16), vw))
        o = jnp.where(m0, outs[0], outs[1])
        g = g_ref[0, pl.ds(qo, GRID_W), :].astype(F32)
        o_ref[0, pl.ds(qo, GRID_W), :] = (o * _silu(g)).astype(o_ref.dtype)
        return carry

    lax.fori_loop(0, n_rows, body, 0)


def _natten(h_c, q_norm, k_norm, bias_tab):
    bsz, t, w4 = h_c.shape
    bw = BRANCH_W
    nlb = bw // LANES
    sec = lambda s: pl.BlockSpec((1, t, LANES), functools.partial(lambda b, hp, s: (b, 0, s * nlb + hp), s=s))
    two = lambda a: jnp.concatenate([a, a]).reshape(1, LANES)
    return pl.pallas_call(
        _natten_kernel,
        grid=(bsz, nlb),
        in_specs=[sec(0), sec(1), sec(2), sec(3),
                  pl.BlockSpec((1, LANES), lambda b, hp: (0, 0)), pl.BlockSpec((1, LANES), lambda b, hp: (0, 0)),
                  pl.BlockSpec((2, NA_ROWS, GRID_W, NA_ROWS * GRID_W), lambda b, hp: (hp, 0, 0, 0))],
        out_specs=pl.BlockSpec((1, t, LANES), lambda b, hp: (b, 0, hp)),
        out_shape=jax.ShapeDtypeStruct((bsz, t, bw), BF16),
        scratch_shapes=[pltpu.VMEM((t, LANES), BF16), pltpu.VMEM((t, LANES), BF16)],
        compiler_params=_cparams("parallel", "parallel"),
        name="neighbourhood_attention",
    )(h_c, h_c, h_c, h_c, two(q_norm), two(k_norm), bias_tab)


def _memattn_kernel(q_ref, g_ref, k_ref, v_ref, qn_ref, kn_ref, o_ref):
    def norm(x, gain, scale):
        ms = jnp.mean(x * x, axis=-1, keepdims=True)
        return x * lax.rsqrt(ms + 1e-6) * (gain * scale)

    q = norm(q_ref[0].astype(F32), qn_ref[...], MEM_HEAD ** -0.5).astype(BF16)
    k = norm(k_ref[0].astype(F32), kn_ref[...], 1.0).astype(BF16)
    s = _dot_nt(q, k)
    s = s - jnp.max(s, axis=-1, keepdims=True)
    e = jnp.exp(s)
    p = e / jnp.sum(e, axis=-1, keepdims=True)
    o = _dot(p.astype(BF16), v_ref[0])
    o_ref[0] = (o * _silu(g_ref[0].astype(F32))).astype(o_ref.dtype)


def _memattn(h_m, kv, q_norm, k_norm, tt=512):
    bsz, t, _ = h_m.shape
    mlen = kv.shape[1]
    tt = min(tt, t)
    hd = MEM_HEAD
    vec = pl.BlockSpec((1, hd), lambda b, i, h: (0, 0))
    return pl.pallas_call(
        _memattn_kernel,
        grid=(bsz, t // tt, MEM_HEADS),
        in_specs=[pl.BlockSpec((1, tt, hd), lambda b, i, h: (b, i, h)),
                  pl.BlockSpec((1, tt, hd), lambda b, i, h: (b, i, MEM_HEADS + h)),
                  pl.BlockSpec((1, mlen, hd), lambda b, i, h: (b, 0, h)),
                  pl.BlockSpec((1, mlen, hd), lambda b, i, h: (b, 0, MEM_HEADS + h)),
                  vec, vec],
        out_specs=pl.BlockSpec((1, tt, hd), lambda b, i, h: (b, i, h)),
        out_shape=jax.ShapeDtypeStruct((bsz, t, BRANCH_W), BF16),
        compiler_params=_cparams("parallel", "parallel", "parallel"),
        name="memory_attention",
    )(h_m, h_m, kv, kv, q_norm.reshape(1, hd), k_norm.reshape(1, hd))


def _pad_lora(up):
    z = jnp.zeros_like(up[0])
    return jnp.stack([jnp.concatenate([up[0], z], axis=0), jnp.concatenate([z, up[1]], axis=0)])


def _layer(x2d, mem2d, bsz, p):
    m, d = x2d.shape
    t = m // bsz
    bw = BRANCH_W
    a_w = A_SHIFT_W + bw
    o1, o2, o3, o4 = a_w, a_w + 3 * bw, a_w + 7 * bw, a_w + 9 * bw
    w_in = p["w_in"]
    wcast = lambda lo, hi: w_in[:, lo:hi].astype(BF16)

    xn = _rmsnorm(x2d, p["norm_g"], BF16)
    h_as = _matmul(xn, wcast(0, A_SHIFT_W), BF16, 1024, 256, name="proj_a_shift")
    h_ag = _matmul(xn, wcast(A_SHIFT_W, o1), BF16, 1024, 1024, name="proj_a_gate")
    h_b = _matmul(xn, wcast(o1, o2), BF16, 1024, 1024, name="proj_b")
    h_c = _matmul(xn, wcast(o2, o3), BF16, 1024, 1024, name="proj_c")
    h_m = _matmul(xn, wcast(o3, o4), BF16, 1024, 1024, name="proj_m")
    gates = _matmul(xn, wcast(o4, o4 + 4 * d), BF16, 1024, 1024, act="sigmoid", name="proj_gates")

    prep = _rwkv_prep(h_as.reshape(bsz, t, A_SHIFT_W), p["a_conv"], _pad_lora(p["a_w_up"]), p["a_w0"],
                      _pad_lora(p["a_a_up"]), p["a_a0"], p["a_k_k"], p["a_k_a"], p["a_r_k"].reshape(-1))
    r, v, kk, lw0, lw1, kt0, kt1, b0, b1, bonus = prep
    yf, yb = _wkv(r, v, kk, lw0, lw1, kt0, kt1, b0, b1)
    y_a = _rwkv_post(yf, yb, bonus, h_ag.reshape(bsz, t, bw), p["a_lnx_w"], p["a_lnx_b"])

    bs_cols = jnp.repeat(p["b_b_s"].T, SG_CHUNK, axis=1)
    y_b = _sgu(h_b.reshape(bsz, t, 3 * bw), p["b_ln_g"], p["b_ln_b"], p["b_w_s"].astype(BF16), bs_cols)

    y_c = _natten(h_c.reshape(bsz, t, 4 * bw), p["c_q_norm"], p["c_k_norm"], _natten_bias_table(p["c_rpb"]))

    mem_n = _rmsnorm(mem2d, p["m_norm_g"], BF16)
    kv = _matmul(mem_n, p["m_w_kv"].astype(BF16), BF16, 1024, 1024, name="proj_mem_kv")
    y_d = _memattn(h_m.reshape(bsz, t, 2 * bw), kv.reshape(bsz, -1, 2 * bw), p["m_q_norm"], p["m_k_norm"])

    ys = [y.reshape(m, bw) for y in (y_a, y_b, y_c, y_d)]
    merged = _merge(ys, p["w_branch"].astype(BF16), gates, 512, 1024)
    return _matmul_residual(merged, p["w_out"].astype(BF16), x2d, 512, 1024)


def kernel(x, mem, norm_g, w_in, a_conv, a_w_up, a_w0, a_a_up, a_a0, a_k_k, a_k_a, a_r_k, a_lnx_w, a_lnx_b,
           b_ln_g, b_ln_b, b_w_s, b_b_s, c_q_norm, c_k_norm, c_rpb, m_norm_g, m_w_kv, m_q_norm, m_k_norm,
           w_branch, w_out):
    params = dict(norm_g=norm_g, w_in=w_in, a_conv=a_conv, a_w_up=a_w_up, a_w0=a_w0, a_a_up=a_a_up, a_a0=a_a0,
                  a_k_k=a_k_k, a_k_a=a_k_a, a_r_k=a_r_k, a_lnx_w=a_lnx_w, a_lnx_b=a_lnx_b, b_ln_g=b_ln_g,
                  b_ln_b=b_ln_b, b_w_s=b_w_s, b_b_s=b_b_s, c_q_norm=c_q_norm, c_k_norm=c_k_norm, c_rpb=c_rpb,
                  m_norm_g=m_norm_g, m_w_kv=m_w_kv, m_q_norm=m_q_norm, m_k_norm=m_k_norm, w_branch=w_branch,
                  w_out=w_out)
    bsz, t, d = x.shape
    x2d = x.reshape(bsz * t, d)
    mem2d = mem.reshape(-1, d)
    for l in range(norm_g.shape[0]):
        x2d = _layer(x2d, mem2d, bsz, {k: v[l] for k, v in params.items()})
    return x2d.reshape(bsz, t, d)
```

```python
import functools

import numpy as np
import jax
import jax.numpy as jnp
from jax import lax
from jax.experimental import pallas as pl
from jax.experimental.pallas import tpu as pltpu

F32 = jnp.float32
BF16 = jnp.bfloat16
HIGHEST = lax.Precision.HIGHEST

VMEM_LIMIT_BYTES = 56 * 1024 * 1024
LANES = 128
BF16_SUBLANES = 16

BRANCH_W = 1024
RWKV_HEAD = 64
LORA = 64
A_SHIFT_W = 3 * BRANCH_W + 4 * LORA
LNX_EPS = 64e-5
SG_CHUNK = 128
SG_GROUPS = 8
NA_HEAD = 64
NA_ROWS = 8
NA_COLS = 16
GRID_W = 64
MEM_HEADS = 4
MEM_HEAD = 256
WKV_CHUNK = 64
NEG_INF = -1e30


def _cparams(*sem):
    return pltpu.CompilerParams(dimension_semantics=sem, vmem_limit_bytes=VMEM_LIMIT_BYTES)


def _sigmoid(x):
    return 1.0 / (1.0 + jnp.exp(-x))


def _silu(x):
    return x * _sigmoid(x)


def _softplus(x):
    return jnp.maximum(x, 0.0) + jnp.log(1.0 + jnp.exp(-jnp.abs(x)))


def _gelu_tanh(x):
    c = np.float32(np.sqrt(2.0 / np.pi))
    return 0.5 * x * (1.0 + jnp.tanh(c * (x + 0.044715 * (x * x * x))))


def _dot(a, b, precision=None):
    return jnp.dot(a, b, preferred_element_type=F32, precision=precision)


def _dot_nt(a, b, precision=None):
    return lax.dot_general(a, b, (((1,), (1,)), ((), ())), preferred_element_type=F32,
                           precision=precision)


def _dot_tn(a, b, precision=None):
    return lax.dot_general(a, b, (((0,), (0,)), ((), ())), preferred_element_type=F32,
                           precision=precision)


def _split_bf16(x, terms):
    parts = []
    for _ in range(terms - 1):
        hi = x.astype(BF16)
        parts.append(hi)
        x = x - hi.astype(F32)
    parts.append(x.astype(BF16))
    return parts


def _dot_exact_rhs(x, w_bf16, terms):
    acc = None
    for piece in _split_bf16(x, terms):
        d = _dot(piece, w_bf16)
        acc = d if acc is None else acc + d
    return acc


def _head_block_ones(head):
    shift = int(np.log2(head))
    r = lax.broadcasted_iota(jnp.int32, (LANES, LANES), 0) >> shift
    c = lax.broadcasted_iota(jnp.int32, (LANES, LANES), 1) >> shift
    return jnp.where(r == c, 1.0, 0.0).astype(BF16)


def _head_sum(x, ones_bd, terms=3):
    parts = [_dot_exact_rhs(x[:, j:j + LANES], ones_bd, terms) for j in range(0, x.shape[1], LANES)]
    return parts[0] if len(parts) == 1 else jnp.concatenate(parts, axis=1)


def _rmsnorm_kernel(x_ref, g_ref, o_ref, *, eps):
    x = x_ref[...].astype(F32)
    ms = jnp.mean(x * x, axis=-1, keepdims=True)
    o_ref[...] = (x * lax.rsqrt(ms + eps) * g_ref[...]).astype(o_ref.dtype)


def _rmsnorm(x2d, g, out_dtype, tm=256, eps=1e-6):
    m, d = x2d.shape
    tm = min(tm, m)
    return pl.pallas_call(
        functools.partial(_rmsnorm_kernel, eps=eps),
        grid=(m // tm,),
        in_specs=[pl.BlockSpec((tm, d), lambda i: (i, 0)), pl.BlockSpec((1, d), lambda i: (0, 0))],
        out_specs=pl.BlockSpec((tm, d), lambda i: (i, 0)),
        out_shape=jax.ShapeDtypeStruct((m, d), out_dtype),
        compiler_params=_cparams("parallel"),
        name="rmsnorm",
    )(x2d, g.reshape(1, d))


def _mm_kernel(a_ref, b_ref, o_ref, *, act):
    acc = _dot(a_ref[...], b_ref[0])
    if act == "sigmoid":
        acc = _sigmoid(acc)
    o_ref[...] = acc.astype(o_ref.dtype)


def _matmul(a, w, layer, col0, n, out_dtype, tm, tn, act=None, name="matmul"):
    m, k = a.shape
    tm, tn = min(tm, m), min(tn, n)
    return pl.pallas_call(
        functools.partial(_mm_kernel, act=act),
        grid=(m // tm, n // tn),
        in_specs=[pl.BlockSpec((tm, k), lambda i, j: (i, 0)),
                  pl.BlockSpec((pl.Element(1), pl.Element(k), pl.Element(tn)),
                               lambda i, j: (layer, 0, pl.multiple_of(col0 + j * tn, LANES)))],
        out_specs=pl.BlockSpec((tm, tn), lambda i, j: (i, j)),
        out_shape=jax.ShapeDtypeStruct((m, n), out_dtype),
        compiler_params=_cparams("parallel", "parallel"),
        name=name,
    )(a, w)


def _mm_residual_kernel(a_ref, b_ref, x_ref, o_ref):
    o_ref[...] = x_ref[...] + _dot(a_ref[...], b_ref[...])


def _matmul_residual(a, w, layer, x, tm, tn):
    m, k = a.shape
    n = w.shape[2]
    tm, tn = min(tm, m), min(tn, n)
    return pl.pallas_call(
        _mm_residual_kernel,
        grid=(m // tm, n // tn),
        in_specs=[pl.BlockSpec((tm, k), lambda i, j: (i, 0)),
                  pl.BlockSpec((None, k, tn), lambda i, j: (layer, 0, j)),
                  pl.BlockSpec((tm, tn), lambda i, j: (i, j))],
        out_specs=pl.BlockSpec((tm, tn), lambda i, j: (i, j)),
        out_shape=jax.ShapeDtypeStruct((m, n), x.dtype),
        compiler_params=_cparams("parallel", "parallel"),
        name="out_proj_residual",
    )(a, w, x)


def _merge_kernel(y0, y1, y2, y3, wb_ref, g0, g1, g2, g3, o_ref):
    acc = None
    for n, (y, g) in enumerate(((y0, g0), (y1, g1), (y2, g2), (y3, g3))):
        term = g[...].astype(F32) * _dot(y[...], wb_ref[n])
        acc = term if acc is None else acc + term
    o_ref[...] = acc.astype(o_ref.dtype)


def _merge(ys, wb, layer, gates, tm, tn):
    m, bw = ys[0].shape
    d = wb.shape[3]
    tm, tn = min(tm, m), min(tn, d)
    nj = d // tn
    y_spec = pl.BlockSpec((tm, bw), lambda i, j: (i, 0))
    g_specs = [pl.BlockSpec((tm, tn), functools.partial(lambda i, j, n: (i, n * nj + j), n=n))
               for n in range(4)]
    return pl.pallas_call(
        _merge_kernel,
        grid=(m // tm, nj),
        in_specs=[y_spec] * 4 + [pl.BlockSpec((None, 4, bw, tn), lambda i, j: (layer, 0, 0, j))] + g_specs,
        out_specs=pl.BlockSpec((tm, tn), lambda i, j: (i, j)),
        out_shape=jax.ShapeDtypeStruct((m, d), BF16),
        compiler_params=_cparams("parallel", "parallel"),
        name="gated_merge",
    )(*ys, wb, gates, gates, gates, gates)


def _rwkv_prep_kernel(h_ref, hp_ref, hn_ref, conv_ref, wup_ref, w0_ref, aup_ref, a0_ref, kk_ref, ka_ref,
                      rk_ref, r_o, v_o, kk_o, lw0_o, lw1_o, kt0_o, kt1_o, b0_o, b1_o, bonus_o):
    i = pl.program_id(1)
    nt = pl.num_programs(1)
    h = h_ref[0].astype(F32)
    tt = h.shape[0]
    prev_row = hp_ref[0, BF16_SUBLANES - 1:BF16_SUBLANES, :].astype(F32) * (i > 0).astype(F32)
    next_row = hn_ref[0, 0:1, :].astype(F32) * (i < nt - 1).astype(F32)
    rows = lax.broadcasted_iota(jnp.int32, (tt, 1), 0)
    h_dn = jnp.where(rows == 0, prev_row, pltpu.roll(h, 1, 0))
    h_up = jnp.where(rows == tt - 1, next_row, pltpu.roll(h, tt - 1, 0))
    hs = h_dn * conv_ref[0:1, :] + h * conv_ref[1:2, :] + h_up * conv_ref[2:3, :]

    bw = BRANCH_W
    r = hs[:, 0:bw]
    k = hs[:, bw:2 * bw]
    v = hs[:, 2 * bw:3 * bw]
    wd = jnp.tanh(hs[:, 3 * bw:3 * bw + 2 * LORA])
    ad = hs[:, 3 * bw + 2 * LORA:3 * bw + 4 * LORA]

    ones_bd = _head_block_ones(RWKV_HEAD)
    kkr = k * kk_ref[...]
    nrm = jnp.sqrt(_head_sum(kkr * kkr, ones_bd))
    kk = kkr / jnp.maximum(nrm, 1e-12)

    kts = []
    for z, (lw_o, kt_o, b_o) in enumerate(((lw0_o, kt0_o, b0_o), (lw1_o, kt1_o, b1_o))):
        w_raw = w0_ref[z:z + 1, :] + _dot(wd, wup_ref[z], HIGHEST)
        lw_o[0] = -jnp.exp(-_softplus(-w_raw) - 0.5)
        a = _sigmoid(a0_ref[z:z + 1, :] + _dot(ad, aup_ref[z], HIGHEST))
        kt = k * (1.0 + (a - 1.0) * ka_ref[...])
        kt_o[0] = kt
        b_o[0] = kk * a
        kts.append(kt)
    r_o[0] = r
    v_o[0] = v
    kk_o[0] = kk
    bonus_o[0] = _head_sum(r * (kts[0] + kts[1]) * rk_ref[...], ones_bd) * v


def _rwkv_prep(h_shift, conv, wup_pad, w0, aup_pad, a0, k_k, k_a, r_k, tt=256):
    bsz, t, w = h_shift.shape
    tt = min(tt, t)
    nt = t // tt
    hb = tt // BF16_SUBLANES
    n_halo = t // BF16_SUBLANES
    row = lambda a: a.reshape(1, -1)
    vec_spec = pl.BlockSpec((1, BRANCH_W), lambda b, i: (0, 0))
    out_spec = pl.BlockSpec((1, tt, BRANCH_W), lambda b, i: (b, i, 0))
    out_sds = jax.ShapeDtypeStruct((bsz, t, BRANCH_W), F32)
    return pl.pallas_call(
        _rwkv_prep_kernel,
        grid=(bsz, nt),
        in_specs=[
            pl.BlockSpec((1, tt, w), lambda b, i: (b, i, 0)),
            pl.BlockSpec((1, BF16_SUBLANES, w), lambda b, i: (b, jnp.maximum(i * hb - 1, 0), 0)),
            pl.BlockSpec((1, BF16_SUBLANES, w), lambda b, i: (b, jnp.minimum((i + 1) * hb, n_halo - 1), 0)),
            pl.BlockSpec((3, w), lambda b, i: (0, 0)),
            pl.BlockSpec((2, 2 * LORA, BRANCH_W), lambda b, i: (0, 0, 0)),
            pl.BlockSpec((2, BRANCH_W), lambda b, i: (0, 0)),
            pl.BlockSpec((2, 2 * LORA, BRANCH_W), lambda b, i: (0, 0, 0)),
            pl.BlockSpec((2, BRANCH_W), lambda b, i: (0, 0)),
            vec_spec, vec_spec, vec_spec,
        ],
        out_specs=[out_spec] * 10,
        out_shape=[out_sds] * 10,
        compiler_params=_cparams("parallel", "parallel"),
        name="rwkv_prep",
    )(h_shift, h_shift, h_shift, conv, wup_pad, w0, aup_pad, a0, row(k_k), row(k_a), row(r_k))


def _wkv_masks():
    c = WKV_CHUNK
    row = lax.broadcasted_iota(jnp.int32, (2 * c, 2 * c), 0)
    col = lax.broadcasted_iota(jnp.int32, (2 * c, 2 * c), 1)
    same = (row >> 6) == (col >> 6)
    rt, ct = row & (c - 1), col & (c - 1)
    f = lambda m: jnp.where(same & m, 1.0, 0.0).astype(F32)
    lane = lax.broadcasted_iota(jnp.int32, (1, LANES), 1)
    crow = lax.broadcasted_iota(jnp.int32, (c, c), 0)
    ccol = lax.broadcasted_iota(jnp.int32, (c, c), 1)
    return {
        "eye": f(rt == ct),
        "strict": (f(ct < rt), f(ct > rt)),
        "incl": (f(ct <= rt), f(ct >= rt)),
        "cum": (jnp.where(ccol <= crow, 1.0, 0.0).astype(BF16), jnp.where(ccol >= crow, 1.0, 0.0).astype(BF16)),
        "head0": jnp.where(lane < RWKV_HEAD, 1.0, 0.0).astype(F32),
        "head1": jnp.where(lane < RWKV_HEAD, 0.0, 1.0).astype(F32),
    }


def _wkv_local(problems, masks):
    c = WKV_CHUNK
    bf = lambda x: x.astype(BF16)
    m0, m1 = masks["head0"], masks["head1"]
    pair = lambda x: jnp.concatenate([x * m0, x * m1], axis=0)
    each = lambda fn, *lists: [fn(*xs) for xs in zip(*lists)]
    dirs = [p[6] for p in problems]

    cl = [_cumsum_dot(masks["cum"][p[6]], p[0]) for p in problems]
    tot = [x[c - 1:c, :] if d == 0 else x[0:1, :] for x, d in zip(cl, dirs)]
    zp = [pair(-p[4] * jnp.exp(x - p[0])) for p, x in zip(problems, cl)]
    rp = [pair(p[1] * jnp.exp(x)) for p, x in zip(problems, cl)]
    vpb = [bf(pair(p[3])) for p in problems]
    e_neg = [jnp.exp(-x) for x in cl]
    bk_start = [bf(jnp.concatenate([pair(p[5] * e), pair(p[2] * e)], axis=0)) for p, e in zip(problems, e_neg)]
    e_end = [jnp.exp(t - x) for t, x in zip(tot, cl)]
    bk_end = [bf(jnp.concatenate([pair(p[5] * e), pair(p[2] * e)], axis=0)) for p, e in zip(problems, e_end)]

    scores = each(lambda z, r, bk: _dot_nt(bf(jnp.concatenate([z, r], axis=0)), bk), zp, rp, bk_start)
    strict = [masks["strict"][d] for d in dirs]
    incl = [masks["incl"][d] for d in dirs]
    l_zb = each(lambda s, m: s[0:2 * c, 0:2 * c] * m, scores, strict)
    a_zk = each(lambda s, m: bf(s[0:2 * c, 2 * c:4 * c] * m), scores, strict)
    a_r = each(lambda s, m: bf(jnp.concatenate([s[2 * c:4 * c, 0:2 * c] * m, s[2 * c:4 * c, 2 * c:4 * c] * m],
                                               axis=1)), scores, incl)
    azk_v = each(_dot, a_zk, vpb)

    inv = [masks["eye"] + l for l in l_zb]
    pb = [bf(l) for l in l_zb]
    for _ in range(5):
        pb = [bf(_dot(x, x)) for x in pb]
        inv = each(lambda t, x: t + _dot(bf(t), x), inv, pb)

    zu = each(lambda t, z, u: _dot(bf(t), bf(jnp.concatenate([z, u], axis=1))), inv, zp, azk_v)
    stack = each(lambda x, v: jnp.concatenate([bf(x), jnp.concatenate([jnp.zeros_like(v), v], axis=1)], axis=0),
                 zu, vpb)
    ry = each(_dot, a_r, stack)
    gh = each(_dot_tn, stack, bk_end)
    out = []
    for r, y, g, t in zip(rp, ry, gh, tot):
        rb = r + y[:, 0:2 * c]
        out.append((rb[0:c] + rb[c:2 * c], y[0:c, 2 * c:4 * c] + y[c:2 * c, 2 * c:4 * c],
                    g[0:2 * c], g[2 * c:4 * c], jnp.exp(t)))
    return out


def _cumsum_dot(cum_bf16, lw):
    acc = None
    for piece in _split_bf16(lw, 3):
        d = _dot(cum_bf16, piece)
        acc = d if acc is None else acc + d
    return acc


WKV_GROUP = 4


def _wkv_kernel(r_f, v_f, kk_f, lw_f, kt_f, b_f, r_b, v_b, kk_b, lw_b, kt_b, b_b, yf_o, yb_o,
                s_ref, rb_ref, g_ref, h_ref, wc_ref):
    @pl.when(pl.program_id(2) == 0)
    def _():
        s_ref[...] = jnp.zeros_like(s_ref)

    c = WKV_CHUNK
    n_chunks = r_f.shape[1] // c
    masks = _wkv_masks()
    ins = ((lw_f, r_f, kt_f, v_f, kk_f, b_f), (lw_b, r_b, kt_b, v_b, kk_b, b_b))
    outs = (yf_o, yb_o)

    def local_body(gi, carry):
        where, problems = [], []
        for j in range(WKV_GROUP):
            ci = gi * WKV_GROUP + j
            off = pl.multiple_of(ci * c, c)
            for d in range(2):
                where.append((d, ci, off))
                problems.append([ref[0, pl.ds(off, c), :] for ref in ins[d]] + [d])
        for (d, ci, off), (rb, yloc, g, h, wc) in zip(where, _wkv_local(problems, masks)):
            rb_ref[d, pl.ds(off, c), :] = rb.astype(BF16)
            outs[d][0, pl.ds(off, c), :] = yloc
            g_ref[d, ci] = g.astype(BF16)
            h_ref[d, ci] = h
            wc_ref[d, ci] = jnp.broadcast_to(wc, (8, LANES))
        return carry

    lax.fori_loop(0, n_chunks // WKV_GROUP, local_body, 0)

    states = [s_ref[0], s_ref[1]]
    for step in range(n_chunks):
        for d in range(2):
            ci = step if d == 0 else n_chunks - 1 - step
            rows = slice(ci * c, (ci + 1) * c)
            sb = states[d].astype(BF16)
            outs[d][0, rows, :] = outs[d][0, rows, :] + _dot_nt(rb_ref[d, rows, :], sb)
            states[d] = states[d] * wc_ref[d, ci, 0:1, :] + _dot(sb, g_ref[d, ci]) + h_ref[d, ci]
    s_ref[0] = states[0]
    s_ref[1] = states[1]


def _wkv(r, v, kk, lw0, lw1, kt0, kt1, b0, b1, tb=512):
    bsz, t, w = r.shape
    tb = min(tb, t)
    nb = t // tb
    nc = tb // WKV_CHUNK
    fwd = pl.BlockSpec((1, tb, LANES), lambda b, h, g: (b, g, h))
    bwd = pl.BlockSpec((1, tb, LANES), lambda b, h, g: (b, nb - 1 - g, h))
    sds = jax.ShapeDtypeStruct((bsz, t, w), F32)
    return pl.pallas_call(
        _wkv_kernel,
        grid=(bsz, w // LANES, nb),
        in_specs=[fwd] * 6 + [bwd] * 6,
        out_specs=[fwd, bwd],
        out_shape=[sds, sds],
        scratch_shapes=[pltpu.VMEM((2, LANES, LANES), F32), pltpu.VMEM((2, tb, LANES), BF16),
                        pltpu.VMEM((2, nc, LANES, LANES), BF16), pltpu.VMEM((2, nc, LANES, LANES), F32),
                        pltpu.VMEM((2, nc, 8, LANES), F32)],
        compiler_params=_cparams("parallel", "parallel", "arbitrary"),
        name="wkv7_chunked",
    )(r, v, kk, lw0, kt0, b0, r, v, kk, lw1, kt1, b1)


def _rwkv_post_kernel(yf_ref, yb_ref, bonus_ref, g_ref, lw_ref, lb_ref, o_ref):
    ones_bd = _head_block_ones(RWKV_HEAD)
    wkv = yf_ref[0] + yb_ref[0]
    inv_n = 1.0 / RWKV_HEAD
    mu = _head_sum(wkv, ones_bd) * inv_n
    d = wkv - mu
    var = _head_sum(d * d, ones_bd) * inv_n
    gn = d * lax.rsqrt(var + LNX_EPS) * lw_ref[...] + lb_ref[...]
    o_ref[0] = ((gn + bonus_ref[0]) * _silu(g_ref[0].astype(F32))).astype(o_ref.dtype)


def _rwkv_post(yf, yb, bonus, g, lnx_w, lnx_b, tt=512):
    bsz, t, w = yf.shape
    tt = min(tt, t)
    spec = pl.BlockSpec((1, tt, w), lambda b, i: (b, i, 0))
    vec = pl.BlockSpec((1, w), lambda b, i: (0, 0))
    return pl.pallas_call(
        _rwkv_post_kernel,
        grid=(bsz, t // tt),
        in_specs=[spec, spec, spec, spec, vec, vec],
        out_specs=spec,
        out_shape=jax.ShapeDtypeStruct((bsz, t, w), BF16),
        compiler_params=_cparams("parallel", "parallel"),
        name="rwkv_post",
    )(yf, yb, bonus, g, lnx_w.reshape(1, w), lnx_b.reshape(1, w))


def _sgu_kernel(h_ref, lg_ref, lb_ref, ws_ref, bs_ref, o_ref):
    bw = BRANCH_W
    tt = h_ref.shape[1]
    u = _gelu_tanh(h_ref[0, :, 0:bw].astype(F32))
    vv = _gelu_tanh(h_ref[0, :, bw:2 * bw].astype(F32))
    g = h_ref[0, :, 2 * bw:3 * bw].astype(F32)
    mu = jnp.mean(vv, axis=-1, keepdims=True)
    d = vv - mu
    var = jnp.mean(d * d, axis=-1, keepdims=True)
    vn = (d * lax.rsqrt(var + 1e-5) * lg_ref[...] + lb_ref[...]).astype(BF16)
    gate = u * _silu(g)
    for ck in range(tt // SG_CHUNK):
        rs = slice(ck * SG_CHUNK, (ck + 1) * SG_CHUNK)
        for grp in range(SG_GROUPS):
            cs = slice(grp * LANES, (grp + 1) * LANES)
            sv = _dot(ws_ref[grp], vn[rs, cs]) + bs_ref[:, cs]
            o_ref[0, rs, cs] = (gate[rs, cs] * sv).astype(o_ref.dtype)


def _sgu(h_b, ln_g, ln_b, w_s, bs_cols, tt=256):
    bsz, t, w3 = h_b.shape
    bw = BRANCH_W
    tt = min(tt, t)
    vec = pl.BlockSpec((1, bw), lambda b, i: (0, 0))
    return pl.pallas_call(
        _sgu_kernel,
        grid=(bsz, t // tt),
        in_specs=[pl.BlockSpec((1, tt, w3), lambda b, i: (b, i, 0)), vec, vec,
                  pl.BlockSpec((SG_GROUPS, SG_CHUNK, SG_CHUNK), lambda b, i: (0, 0, 0)),
                  pl.BlockSpec((SG_CHUNK, bw), lambda b, i: (0, 0))],
        out_specs=pl.BlockSpec((1, tt, bw), lambda b, i: (b, i, 0)),
        out_shape=jax.ShapeDtypeStruct((bsz, t, bw), BF16),
        compiler_params=_cparams("parallel", "parallel"),
        name="spatial_gating",
    )(h_b, ln_g.reshape(1, bw), ln_b.reshape(1, bw), w_s, bs_cols)


def _natten_bias_table(rpb):
    p = np.arange(GRID_W)[:, None]
    m = np.arange(GRID_W)[None, :]
    sj = np.clip(p - NA_COLS // 2, 0, GRID_W - NA_COLS)
    valid = (m >= sj) & (m < sj + NA_COLS)
    dc = np.clip(m - p, -(NA_COLS - 1), NA_COLS - 1) + NA_COLS - 1
    rel = np.arange(NA_ROWS)[:, None] + np.arange(NA_ROWS)[None, :]
    tab = rpb[:, rel][:, :, :, dc]
    tab = jnp.where(valid[None, None, None], tab, NEG_INF)
    tab = tab.transpose(0, 1, 3, 2, 4)
    return tab.reshape(rpb.shape[0], NA_ROWS, GRID_W, NA_ROWS * GRID_W).astype(F32)


NA_UNROLL = 4


def _natten_kernel(q_ref, k_ref, v_ref, g_ref, qn_ref, kn_ref, bias_ref, o_ref, qs_ref, ks_ref):
    t = q_ref.shape[1]
    n_rows = t // GRID_W
    win = NA_ROWS * GRID_W
    ones_bd = _head_block_ones(NA_HEAD)
    lane = lax.broadcasted_iota(jnp.int32, (1, LANES), 1)
    m0 = lane < NA_HEAD

    def norm(x_ref, gain_ref, scale):
        x = x_ref[0].astype(F32)
        ms = _head_sum(x * x, ones_bd, terms=2) * (1.0 / NA_HEAD)
        return x * lax.rsqrt(ms + 1e-6) * (gain_ref[...] * scale)

    qs_ref[...] = norm(q_ref, qn_ref, NA_HEAD ** -0.5).astype(BF16)
    ks_ref[...] = norm(k_ref, kn_ref, 1.0).astype(BF16)

    def body(it, carry):
        rows = [it * NA_UNROLL + j for j in range(NA_UNROLL)]
        si = [jnp.clip(i - NA_ROWS // 2, 0, n_rows - NA_ROWS) for i in rows]
        start = [s - i + (NA_ROWS - 1) for s, i in zip(si, rows)]
        qo = [pl.multiple_of(i * GRID_W, GRID_W) for i in rows]
        ko = [pl.multiple_of(s * GRID_W, GRID_W) for s in si]
        q = [qs_ref[pl.ds(o, GRID_W), :] for o in qo]
        zero = jnp.zeros_like(q[0])
        q2 = [jnp.concatenate([jnp.where(m0, x, zero), jnp.where(m0, zero, x)], axis=0) for x in q]
        s = [_dot_nt(x, ks_ref[pl.ds(o, win), :]) for x, o in zip(q2, ko)]
        s = [x + jnp.concatenate([bias_ref[0, st], bias_ref[1, st]], axis=0) for x, st in zip(s, start)]
        e = [jnp.exp(x - jnp.max(x, axis=-1, keepdims=True)) for x in s]
        p = [(x / jnp.sum(x, axis=-1, keepdims=True)).astype(BF16) for x in e]
        o2 = [_dot(x, v_ref[0, pl.ds(o, win), :]) for x, o in zip(p, ko)]
        for x, o in zip(o2, qo):
            g = g_ref[0, pl.ds(o, GRID_W), :].astype(F32)
            val = jnp.where(m0, x[0:GRID_W], x[GRID_W:2 * GRID_W]) * _silu(g)
            o_ref[0, pl.ds(o, GRID_W), :] = val.astype(o_ref.dtype)
        return carry

    lax.fori_loop(0, n_rows // NA_UNROLL, body, 0)


def _natten(h_c, q_norm, k_norm, bias_tab):
    bsz, t, w4 = h_c.shape
    bw = BRANCH_W
    nlb = bw // LANES
    sec = lambda s: pl.BlockSpec((1, t, LANES), functools.partial(lambda b, hp, s: (b, 0, s * nlb + hp), s=s))
    two = lambda a: jnp.concatenate([a, a]).reshape(1, LANES)
    return pl.pallas_call(
        _natten_kernel,
        grid=(bsz, nlb),
        in_specs=[sec(0), sec(1), sec(2), sec(3),
                  pl.BlockSpec((1, LANES), lambda b, hp: (0, 0)), pl.BlockSpec((1, LANES), lambda b, hp: (0, 0)),
                  pl.BlockSpec((2, NA_ROWS, GRID_W, NA_ROWS * GRID_W), lambda b, hp: (hp, 0, 0, 0))],
        out_specs=pl.BlockSpec((1, t, LANES), lambda b, hp: (b, 0, hp)),
        out_shape=jax.ShapeDtypeStruct((bsz, t, bw), BF16),
        scratch_shapes=[pltpu.VMEM((t, LANES), BF16), pltpu.VMEM((t, LANES), BF16)],
        compiler_params=_cparams("parallel", "parallel"),
        name="neighbourhood_attention",
    )(h_c, h_c, h_c, h_c, two(q_norm), two(k_norm), bias_tab)


def _memattn_kernel(q_ref, g_ref, k_ref, v_ref, qn_ref, kn_ref, o_ref):
    def norm(x, gain, scale):
        ms = jnp.mean(x * x, axis=-1, keepdims=True)
        return x * lax.rsqrt(ms + 1e-6) * (gain * scale)

    q = norm(q_ref[0].astype(F32), qn_ref[...], MEM_HEAD ** -0.5).astype(BF16)
    k = norm(k_ref[0].astype(F32), kn_ref[...], 1.0).astype(BF16)
    s = _dot_nt(q, k)
    s = s - jnp.max(s, axis=-1, keepdims=True)
    e = jnp.exp(s)
    p = e / jnp.sum(e, axis=-1, keepdims=True)
    o = _dot(p.astype(BF16), v_ref[0])
    o_ref[0] = (o * _silu(g_ref[0].astype(F32))).astype(o_ref.dtype)


def _memattn(h_m, kv, q_norm, k_norm, tt=512):
    bsz, t, _ = h_m.shape
    mlen = kv.shape[1]
    tt = min(tt, t)
    hd = MEM_HEAD
    vec = pl.BlockSpec((1, hd), lambda b, i, h: (0, 0))
    return pl.pallas_call(
        _memattn_kernel,
        grid=(bsz, t // tt, MEM_HEADS),
        in_specs=[pl.BlockSpec((1, tt, hd), lambda b, i, h: (b, i, h)),
                  pl.BlockSpec((1, tt, hd), lambda b, i, h: (b, i, MEM_HEADS + h)),
                  pl.BlockSpec((1, mlen, hd), lambda b, i, h: (b, 0, h)),
                  pl.BlockSpec((1, mlen, hd), lambda b, i, h: (b, 0, MEM_HEADS + h)),
                  vec, vec],
        out_specs=pl.BlockSpec((1, tt, hd), lambda b, i, h: (b, i, h)),
        out_shape=jax.ShapeDtypeStruct((bsz, t, BRANCH_W), BF16),
        compiler_params=_cparams("parallel", "parallel", "parallel"),
        name="memory_attention",
    )(h_m, h_m, kv, kv, q_norm.reshape(1, hd), k_norm.reshape(1, hd))


def _pad_lora(up):
    z = jnp.zeros_like(up[0])
    return jnp.stack([jnp.concatenate([up[0], z], axis=0), jnp.concatenate([z, up[1]], axis=0)])


def _layer(x2d, mem2d, bsz, layer, p, big):
    m, d = x2d.shape
    t = m // bsz
    bw = BRANCH_W
    a_w = A_SHIFT_W + bw
    o1, o2, o3, o4 = a_w, a_w + 3 * bw, a_w + 7 * bw, a_w + 9 * bw
    proj = functools.partial(_matmul, _rmsnorm(x2d, p["norm_g"], BF16), big["w_in"], layer)

    h_as = proj(0, A_SHIFT_W, BF16, 1024, 256, name="proj_a_shift")
    h_ag = proj(A_SHIFT_W, bw, BF16, 1024, 1024, name="proj_a_gate")
    h_b = proj(o1, 3 * bw, BF16, 1024, 1024, name="proj_b")
    h_c = proj(o2, 4 * bw, BF16, 1024, 1024, name="proj_c")
    h_m = proj(o3, 2 * bw, BF16, 1024, 1024, name="proj_m")
    gates = proj(o4, 4 * d, BF16, 1024, 1024, act="sigmoid", name="proj_gates")

    prep = _rwkv_prep(h_as.reshape(bsz, t, A_SHIFT_W), p["a_conv"], _pad_lora(p["a_w_up"]), p["a_w0"],
                      _pad_lora(p["a_a_up"]), p["a_a0"], p["a_k_k"], p["a_k_a"], p["a_r_k"].reshape(-1))
    r, v, kk, lw0, lw1, kt0, kt1, b0, b1, bonus = prep
    yf, yb = _wkv(r, v, kk, lw0, lw1, kt0, kt1, b0, b1)
    y_a = _rwkv_post(yf, yb, bonus, h_ag.reshape(bsz, t, bw), p["a_lnx_w"], p["a_lnx_b"])

    bs_cols = jnp.repeat(p["b_b_s"].T, SG_CHUNK, axis=1)
    y_b = _sgu(h_b.reshape(bsz, t, 3 * bw), p["b_ln_g"], p["b_ln_b"], p["b_w_s"].astype(BF16), bs_cols)

    y_c = _natten(h_c.reshape(bsz, t, 4 * bw), p["c_q_norm"], p["c_k_norm"], _natten_bias_table(p["c_rpb"]))

    mem_n = _rmsnorm(mem2d, p["m_norm_g"], BF16)
    kv = _matmul(mem_n, big["m_w_kv"], layer, 0, 2 * bw, BF16, 1024, 1024, name="proj_mem_kv")
    y_d = _memattn(h_m.reshape(bsz, t, 2 * bw), kv.reshape(bsz, -1, 2 * bw), p["m_q_norm"], p["m_k_norm"])

    ys = [y.reshape(m, bw) for y in (y_a, y_b, y_c, y_d)]
    merged = _merge(ys, big["w_branch"], layer, gates, 512, 1024)
    return _matmul_residual(merged, big["w_out"], layer, x2d, 512, 1024)


def kernel(x, mem, norm_g, w_in, a_conv, a_w_up, a_w0, a_a_up, a_a0, a_k_k, a_k_a, a_r_k, a_lnx_w, a_lnx_b,
           b_ln_g, b_ln_b, b_w_s, b_b_s, c_q_norm, c_k_norm, c_rpb, m_norm_g, m_w_kv, m_q_norm, m_k_norm,
           w_branch, w_out):
    params = dict(norm_g=norm_g, a_conv=a_conv, a_w_up=a_w_up, a_w0=a_w0, a_a_up=a_a_up, a_a0=a_a0,
                  a_k_k=a_k_k, a_k_a=a_k_a, a_r_k=a_r_k, a_lnx_w=a_lnx_w, a_lnx_b=a_lnx_b, b_ln_g=b_ln_g,
                  b_ln_b=b_ln_b, b_w_s=b_w_s, b_b_s=b_b_s, c_q_norm=c_q_norm, c_k_norm=c_k_norm, c_rpb=c_rpb,
                  m_norm_g=m_norm_g, m_q_norm=m_q_norm, m_k_norm=m_k_norm)
    big = dict(w_in=w_in.astype(BF16), m_w_kv=m_w_kv.astype(BF16), w_branch=w_branch.astype(BF16),
               w_out=w_out.astype(BF16))
    bsz, t, d = x.shape
    x2d = x.reshape(bsz * t, d)
    mem2d = mem.reshape(-1, d)
    for l in range(norm_g.shape[0]):
        x2d = _layer(x2d, mem2d, bsz, l, {k: v[l] for k, v in params.items()}, big)
    return x2d.reshape(bsz, t, d)
```

```python
import functools

import numpy as np
import jax
import jax.numpy as jnp
from jax import lax
from jax.experimental import pallas as pl
from jax.experimental.pallas import tpu as pltpu

F32 = jnp.float32
BF16 = jnp.bfloat16

VMEM_LIMIT_BYTES = 56 * 1024 * 1024
LANES = 128
BF16_SUBLANES = 16

BRANCH_W = 1024
RWKV_HEAD = 64
LORA = 64
A_SHIFT_W = 3 * BRANCH_W + 4 * LORA
LNX_EPS = 64e-5
SG_CHUNK = 128
SG_GROUPS = 8
NA_HEAD = 64
NA_ROWS = 8
NA_COLS = 16
GRID_W = 64
MEM_HEADS = 4
MEM_HEAD = 256
WKV_CHUNK = 64
NEG_INF = -1e30


def _cparams(*sem):
    return pltpu.CompilerParams(dimension_semantics=sem, vmem_limit_bytes=VMEM_LIMIT_BYTES)


def _sigmoid(x):
    return 1.0 / (1.0 + jnp.exp(-x))


def _silu(x):
    return x * _sigmoid(x)


def _gelu_tanh(x):
    c = np.float32(np.sqrt(2.0 / np.pi))
    return 0.5 * x * (1.0 + jnp.tanh(c * (x + 0.044715 * (x * x * x))))


def _dot(a, b, precision=None):
    return jnp.dot(a, b, preferred_element_type=F32, precision=precision)


def _dot_nt(a, b, precision=None):
    return lax.dot_general(a, b, (((1,), (1,)), ((), ())), preferred_element_type=F32,
                           precision=precision)


def _dot_tn(a, b, precision=None):
    return lax.dot_general(a, b, (((0,), (0,)), ((), ())), preferred_element_type=F32,
                           precision=precision)


def _split_bf16(x, terms):
    parts = []
    for _ in range(terms - 1):
        hi = x.astype(BF16)
        parts.append(hi)
        x = x - hi.astype(F32)
    parts.append(x.astype(BF16))
    return parts


def _dot_exact_rhs(x, w_bf16, terms):
    acc = None
    for piece in _split_bf16(x, terms):
        d = _dot(piece, w_bf16)
        acc = d if acc is None else acc + d
    return acc


def _head_block_ones(head):
    shift = int(np.log2(head))
    r = lax.broadcasted_iota(jnp.int32, (LANES, LANES), 0) >> shift
    c = lax.broadcasted_iota(jnp.int32, (LANES, LANES), 1) >> shift
    return jnp.where(r == c, 1.0, 0.0).astype(BF16)


def _head_sum(x, ones_bd, terms=2):
    parts = [_dot_exact_rhs(x[:, j:j + LANES], ones_bd, terms) for j in range(0, x.shape[1], LANES)]
    return parts[0] if len(parts) == 1 else jnp.concatenate(parts, axis=1)


def _rmsnorm_kernel(x_ref, g_ref, o_ref, *, eps):
    x = x_ref[...].astype(F32)
    ms = jnp.mean(x * x, axis=-1, keepdims=True)
    o_ref[...] = (x * lax.rsqrt(ms + eps) * g_ref[...]).astype(o_ref.dtype)


def _rmsnorm(x2d, g, out_dtype, tm=256, eps=1e-6):
    m, d = x2d.shape
    tm = min(tm, m)
    return pl.pallas_call(
        functools.partial(_rmsnorm_kernel, eps=eps),
        grid=(m // tm,),
        in_specs=[pl.BlockSpec((tm, d), lambda i: (i, 0)), pl.BlockSpec((1, d), lambda i: (0, 0))],
        out_specs=pl.BlockSpec((tm, d), lambda i: (i, 0)),
        out_shape=jax.ShapeDtypeStruct((m, d), out_dtype),
        compiler_params=_cparams("parallel"),
        name="rmsnorm",
    )(x2d, g.reshape(1, d))


def _mm_kernel(a_ref, b_ref, o_ref, *, act):
    acc = _dot(a_ref[...], b_ref[0])
    if act == "sigmoid":
        acc = _sigmoid(acc)
    o_ref[...] = acc.astype(o_ref.dtype)


def _matmul(a, w, layer, col0, n, out_dtype, tm, tn, act=None, name="matmul"):
    m, k = a.shape
    tm, tn = min(tm, m), min(tn, n)
    return pl.pallas_call(
        functools.partial(_mm_kernel, act=act),
        grid=(m // tm, n // tn),
        in_specs=[pl.BlockSpec((tm, k), lambda i, j: (i, 0)),
                  pl.BlockSpec((pl.Element(1), pl.Element(k), pl.Element(tn)),
                               lambda i, j: (layer, 0, pl.multiple_of(col0 + j * tn, LANES)))],
        out_specs=pl.BlockSpec((tm, tn), lambda i, j: (i, j)),
        out_shape=jax.ShapeDtypeStruct((m, n), out_dtype),
        compiler_params=_cparams("parallel", "parallel"),
        name=name,
    )(a, w)


def _mm_residual_kernel(a_ref, b_ref, x_ref, o_ref):
    o_ref[...] = x_ref[...] + _dot(a_ref[...], b_ref[...])


def _matmul_residual(a, w, layer, x, tm, tn):
    m, k = a.shape
    n = w.shape[2]
    tm, tn = min(tm, m), min(tn, n)
    return pl.pallas_call(
        _mm_residual_kernel,
        grid=(n // tn, m // tm),
        in_specs=[pl.BlockSpec((tm, k), lambda j, i: (i, 0)),
                  pl.BlockSpec((None, k, tn), lambda j, i: (layer, 0, j)),
                  pl.BlockSpec((tm, tn), lambda j, i: (i, j))],
        out_specs=pl.BlockSpec((tm, tn), lambda j, i: (i, j)),
        out_shape=jax.ShapeDtypeStruct((m, n), x.dtype),
        compiler_params=_cparams("parallel", "parallel"),
        name="out_proj_residual",
    )(a, w, x)


def _merge_kernel(y0, y1, y2, y3, wb_ref, g0, g1, g2, g3, o_ref):
    acc = None
    for n, (y, g) in enumerate(((y0, g0), (y1, g1), (y2, g2), (y3, g3))):
        term = g[...].astype(F32) * _dot(y[...], wb_ref[n])
        acc = term if acc is None else acc + term
    o_ref[...] = acc.astype(o_ref.dtype)


def _merge(ys, wb, layer, gates, tm, tn):
    m, bw = ys[0].shape
    d = wb.shape[3]
    tm, tn = min(tm, m), min(tn, d)
    nj = d // tn
    y_spec = pl.BlockSpec((tm, bw), lambda j, i: (i, 0))
    g_specs = [pl.BlockSpec((tm, tn), functools.partial(lambda j, i, n: (i, n * nj + j), n=n))
               for n in range(4)]
    return pl.pallas_call(
        _merge_kernel,
        grid=(nj, m // tm),
        in_specs=[y_spec] * 4 + [pl.BlockSpec((None, 4, bw, tn), lambda j, i: (layer, 0, 0, j))] + g_specs,
        out_specs=pl.BlockSpec((tm, tn), lambda j, i: (i, j)),
        out_shape=jax.ShapeDtypeStruct((m, d), BF16),
        compiler_params=_cparams("parallel", "parallel"),
        name="gated_merge",
    )(*ys, wb, gates, gates, gates, gates)


def _token_shift(h_ref, hp_ref, hn_ref, conv_ref):
    i = pl.program_id(1)
    nt = pl.num_programs(1)
    h = h_ref[0].astype(F32)
    tt = h.shape[0]
    prev_row = hp_ref[0, BF16_SUBLANES - 1:BF16_SUBLANES, :].astype(F32) * (i > 0).astype(F32)
    next_row = hn_ref[0, 0:1, :].astype(F32) * (i < nt - 1).astype(F32)
    rows = lax.broadcasted_iota(jnp.int32, (tt, 1), 0)
    h_dn = jnp.where(rows == 0, prev_row, pltpu.roll(h, 1, 0))
    h_up = jnp.where(rows == tt - 1, next_row, pltpu.roll(h, tt - 1, 0))
    return h_dn * conv_ref[0:1, :] + h * conv_ref[1:2, :] + h_up * conv_ref[2:3, :]


def _rwkv_prep_kernel(h_ref, hp_ref, hn_ref, l_ref, lp_ref, ln_ref, conv_ref, convl_ref, wup_ref, w0_ref,
                      aup_ref, a0_ref, kk_ref, ka_ref, rk_ref,
                      r_o, v_o, kk_o, lw0_o, lw1_o, kt0_o, kt1_o, b0_o, b1_o, bonus_o):
    hs = _token_shift(h_ref, hp_ref, hn_ref, conv_ref)
    ls = _token_shift(l_ref, lp_ref, ln_ref, convl_ref)
    bw = BRANCH_W
    r = hs[:, 0:bw]
    k = hs[:, bw:2 * bw]
    v = hs[:, 2 * bw:3 * bw]
    wd = jnp.tanh(ls[:, 0:2 * LORA])
    ad = ls[:, 2 * LORA:4 * LORA]

    ones_bd = _head_block_ones(RWKV_HEAD)
    kkr = k * kk_ref[...]
    nrm = jnp.sqrt(_head_sum(kkr * kkr, ones_bd))
    kk = kkr / jnp.maximum(nrm, 1e-12)

    def up_proj(x_pieces, w_ref, z):
        (x_hi, x_lo), w_hi, w_lo = x_pieces, w_ref[0, z], w_ref[1, z]
        return _dot(x_hi, w_hi) + (_dot(x_lo, w_hi) + _dot(x_hi, w_lo))

    wd_pieces, ad_pieces = _split_bf16(wd, 2), _split_bf16(ad, 2)
    decay_scale = np.float32(np.exp(-0.5))
    kts = []
    for z, (lw_o, kt_o, b_o) in enumerate(((lw0_o, kt0_o, b0_o), (lw1_o, kt1_o, b1_o))):
        w_raw = w0_ref[z:z + 1, :] + up_proj(wd_pieces, wup_ref, z)
        lw_o[0] = -decay_scale * _sigmoid(w_raw)
        a = _sigmoid(a0_ref[z:z + 1, :] + up_proj(ad_pieces, aup_ref, z))
        kt = k * (1.0 + (a - 1.0) * ka_ref[...])
        kt_o[0] = kt
        b_o[0] = kk * a
        kts.append(kt)
    r_o[0] = r
    v_o[0] = v
    kk_o[0] = kk
    bonus_o[0] = _head_sum(r * (kts[0] + kts[1]) * rk_ref[...], ones_bd) * v


def _rwkv_prep(h_rkv, h_lora, conv, wup_pad, w0, aup_pad, a0, k_k, k_a, r_k, tt=256):
    bsz, t, w = h_rkv.shape
    wl = h_lora.shape[2]
    tt = min(tt, t)
    nt = t // tt
    hb = tt // BF16_SUBLANES
    n_halo = t // BF16_SUBLANES
    row = lambda a: a.reshape(1, -1)
    vec_spec = pl.BlockSpec((1, BRANCH_W), lambda b, i: (0, 0))
    out_spec = pl.BlockSpec((1, tt, BRANCH_W), lambda b, i: (b, i, 0))
    out_sds = jax.ShapeDtypeStruct((bsz, t, BRANCH_W), F32)
    tile = lambda width: [
        pl.BlockSpec((1, tt, width), lambda b, i: (b, i, 0)),
        pl.BlockSpec((1, BF16_SUBLANES, width), lambda b, i: (b, jnp.maximum(i * hb - 1, 0), 0)),
        pl.BlockSpec((1, BF16_SUBLANES, width), lambda b, i: (b, jnp.minimum((i + 1) * hb, n_halo - 1), 0))]
    return pl.pallas_call(
        _rwkv_prep_kernel,
        grid=(bsz, nt),
        in_specs=tile(w) + tile(wl) + [
            pl.BlockSpec((3, w), lambda b, i: (0, 0)),
            pl.BlockSpec((3, wl), lambda b, i: (0, 0)),
            pl.BlockSpec((2, 2, 2 * LORA, BRANCH_W), lambda b, i: (0, 0, 0, 0)),
            pl.BlockSpec((2, BRANCH_W), lambda b, i: (0, 0)),
            pl.BlockSpec((2, 2, 2 * LORA, BRANCH_W), lambda b, i: (0, 0, 0, 0)),
            pl.BlockSpec((2, BRANCH_W), lambda b, i: (0, 0)),
            vec_spec, vec_spec, vec_spec,
        ],
        out_specs=[out_spec] * 10,
        out_shape=[out_sds] * 10,
        compiler_params=_cparams("parallel", "parallel"),
        name="rwkv_prep",
    )(h_rkv, h_rkv, h_rkv, h_lora, h_lora, h_lora, conv[:, :w], conv[:, w:], wup_pad, w0, aup_pad, a0,
      row(k_k), row(k_a), row(r_k))


def _wkv_masks():
    c = WKV_CHUNK
    row = lax.broadcasted_iota(jnp.int32, (2 * c, 2 * c), 0)
    col = lax.broadcasted_iota(jnp.int32, (2 * c, 2 * c), 1)
    same = (row >> 6) == (col >> 6)
    rt, ct = row & (c - 1), col & (c - 1)
    f = lambda m: jnp.where(same & m, 1.0, 0.0).astype(F32)
    lane = lax.broadcasted_iota(jnp.int32, (1, LANES), 1)
    crow = lax.broadcasted_iota(jnp.int32, (c, c), 0)
    ccol = lax.broadcasted_iota(jnp.int32, (c, c), 1)
    return {
        "eye": f(rt == ct),
        "strict": (f(ct < rt), f(ct > rt)),
        "incl": (f(ct <= rt), f(ct >= rt)),
        "cum": (jnp.where(ccol <= crow, 1.0, 0.0).astype(BF16), jnp.where(ccol >= crow, 1.0, 0.0).astype(BF16)),
        "head0": jnp.where(lane < RWKV_HEAD, 1.0, 0.0).astype(F32),
        "head1": jnp.where(lane < RWKV_HEAD, 0.0, 1.0).astype(F32),
    }


def _wkv_local(problems, masks, fillers=()):
    c = WKV_CHUNK
    bf = lambda x: x.astype(BF16)
    m0, m1 = masks["head0"], masks["head1"]
    pair = lambda x: jnp.concatenate([x * m0, x * m1], axis=0)
    each = lambda fn, *lists: [fn(*xs) for xs in zip(*lists)]
    dirs = [p[6] for p in problems]
    fillers = list(fillers)
    n_points = 16
    stride = max(1, n_points // max(1, len(fillers)))
    seen = [0]

    def fill():
        seen[0] += 1
        if fillers and seen[0] % stride == 0:
            fillers.pop(0)()

    cl = [_cumsum_dot(masks["cum"][p[6]], p[0]) for p in problems]
    tot = [x[c - 1:c, :] if d == 0 else x[0:1, :] for x, d in zip(cl, dirs)]
    fill()
    zp = [pair(-p[4] * jnp.exp(x - p[0])) for p, x in zip(problems, cl)]
    rp = [pair(p[1] * jnp.exp(x)) for p, x in zip(problems, cl)]
    vpb = [bf(pair(p[3])) for p in problems]
    e_neg = [jnp.exp(-x) for x in cl]
    bk_start = [bf(jnp.concatenate([pair(p[5] * e), pair(p[2] * e)], axis=0)) for p, e in zip(problems, e_neg)]
    e_end = [jnp.exp(t - x) for t, x in zip(tot, cl)]
    bk_end = [bf(jnp.concatenate([pair(p[5] * e), pair(p[2] * e)], axis=0)) for p, e in zip(problems, e_end)]
    fill()

    scores = each(lambda z, r, bk: _dot_nt(bf(jnp.concatenate([z, r], axis=0)), bk), zp, rp, bk_start)
    fill()
    strict = [masks["strict"][d] for d in dirs]
    incl = [masks["incl"][d] for d in dirs]
    l_zb = each(lambda s, m: s[0:2 * c, 0:2 * c] * m, scores, strict)
    a_zk = each(lambda s, m: bf(s[0:2 * c, 2 * c:4 * c] * m), scores, strict)
    a_r = each(lambda s, m: bf(jnp.concatenate([s[2 * c:4 * c, 0:2 * c] * m, s[2 * c:4 * c, 2 * c:4 * c] * m],
                                               axis=1)), scores, incl)
    azk_v = each(_dot, a_zk, vpb)
    fill()

    inv = [masks["eye"] + l for l in l_zb]
    pb = [bf(l) for l in l_zb]
    for _ in range(5):
        pb = [bf(_dot(x, x)) for x in pb]
        fill()
        inv = each(lambda t, x: t + _dot(bf(t), x), inv, pb)
        fill()

    zu = each(lambda t, z, u: _dot(bf(t), bf(jnp.concatenate([z, u], axis=1))), inv, zp, azk_v)
    fill()
    stack = each(lambda x, v: jnp.concatenate([bf(x), jnp.concatenate([jnp.zeros_like(v), v], axis=1)], axis=0),
                 zu, vpb)
    ry = each(_dot, a_r, stack)
    fill()
    gh = each(_dot_tn, stack, bk_end)
    while fillers:
        fillers.pop(0)()
    out = []
    for r, y, g, t in zip(rp, ry, gh, tot):
        rb = r + y[:, 0:2 * c]
        out.append((rb[0:c] + rb[c:2 * c], y[0:c, 2 * c:4 * c] + y[c:2 * c, 2 * c:4 * c],
                    g[0:2 * c], g[2 * c:4 * c], jnp.exp(t)))
    return out


def _cumsum_dot(cum_bf16, lw):
    acc = None
    for piece in _split_bf16(lw, 3):
        d = _dot(cum_bf16, piece)
        acc = d if acc is None else acc + d
    return acc


def _wkv_kernel(r_f, v_f, kk_f, lw_f, kt_f, b_f, r_b, v_b, kk_b, lw_b, kt_b, b_b, yf_o, yb_o,
                s_ref, rb_ref, g_ref, h_ref, wc_ref):
    @pl.when(pl.program_id(2) == 0)
    def _():
        s_ref[...] = jnp.zeros_like(s_ref)

    c = WKV_CHUNK
    n_chunks = r_f.shape[1] // c
    half = n_chunks // 2
    masks = _wkv_masks()
    ins = ((lw_f, r_f, kt_f, v_f, kk_f, b_f), (lw_b, r_b, kt_b, v_b, kk_b, b_b))
    outs = (yf_o, yb_o)
    states = [s_ref[0], s_ref[1]]
    chunk_of = lambda d, step: step if d == 0 else n_chunks - 1 - step

    def local_factors(steps, fillers):
        where = [(d, chunk_of(d, s)) for s in steps for d in range(2)]
        problems = [[ref[0, ci * c:(ci + 1) * c, :] for ref in ins[d]] + [d] for d, ci in where]
        for (d, ci), (rb, yloc, g, h, wc) in zip(where, _wkv_local(problems, masks, fillers)):
            rb_ref[d, ci * c:(ci + 1) * c, :] = rb.astype(BF16)
            outs[d][0, ci * c:(ci + 1) * c, :] = yloc
            g_ref[d, ci] = g.astype(BF16)
            h_ref[d, ci] = h
            wc_ref[d, ci] = jnp.broadcast_to(wc, (8, LANES))

    def state_step(step):
        for d in range(2):
            ci = chunk_of(d, step)
            rows = slice(ci * c, (ci + 1) * c)
            sb = states[d].astype(BF16)
            outs[d][0, rows, :] = outs[d][0, rows, :] + _dot_nt(rb_ref[d, rows, :], sb)
            states[d] = states[d] * wc_ref[d, ci, 0:1, :] + _dot(sb, g_ref[d, ci]) + h_ref[d, ci]

    local_factors(range(0, half), ())
    local_factors(range(half, n_chunks), [functools.partial(state_step, s) for s in range(half)])
    for s in range(half, n_chunks):
        state_step(s)
    s_ref[0] = states[0]
    s_ref[1] = states[1]


def _wkv(r, v, kk, lw0, lw1, kt0, kt1, b0, b1, tb=512):
    bsz, t, w = r.shape
    tb = min(tb, t)
    nb = t // tb
    nc = tb // WKV_CHUNK
    fwd = pl.BlockSpec((1, tb, LANES), lambda b, h, g: (b, g, h))
    bwd = pl.BlockSpec((1, tb, LANES), lambda b, h, g: (b, nb - 1 - g, h))
    sds = jax.ShapeDtypeStruct((bsz, t, w), F32)
    return pl.pallas_call(
        _wkv_kernel,
        grid=(bsz, w // LANES, nb),
        in_specs=[fwd] * 6 + [bwd] * 6,
        out_specs=[fwd, bwd],
        out_shape=[sds, sds],
        scratch_shapes=[pltpu.VMEM((2, LANES, LANES), F32), pltpu.VMEM((2, tb, LANES), BF16),
                        pltpu.VMEM((2, nc, LANES, LANES), BF16), pltpu.VMEM((2, nc, LANES, LANES), F32),
                        pltpu.VMEM((2, nc, 8, LANES), F32)],
        compiler_params=_cparams("parallel", "parallel", "arbitrary"),
        name="wkv7_chunked",
    )(r, v, kk, lw0, kt0, b0, r, v, kk, lw1, kt1, b1)


def _rwkv_post_kernel(yf_ref, yb_ref, bonus_ref, g_ref, lw_ref, lb_ref, o_ref):
    ones_bd = _head_block_ones(RWKV_HEAD)
    wkv = yf_ref[0] + yb_ref[0]
    inv_n = 1.0 / RWKV_HEAD
    mu = _head_sum(wkv, ones_bd) * inv_n
    d = wkv - mu
    var = _head_sum(d * d, ones_bd) * inv_n
    gn = d * lax.rsqrt(var + LNX_EPS) * lw_ref[...] + lb_ref[...]
    o_ref[0] = ((gn + bonus_ref[0]) * _silu(g_ref[0].astype(F32))).astype(o_ref.dtype)


def _rwkv_post(yf, yb, bonus, g, lnx_w, lnx_b, tt=512):
    bsz, t, w = yf.shape
    tt = min(tt, t)
    spec = pl.BlockSpec((1, tt, w), lambda b, i: (b, i, 0))
    vec = pl.BlockSpec((1, w), lambda b, i: (0, 0))
    return pl.pallas_call(
        _rwkv_post_kernel,
        grid=(bsz, t // tt),
        in_specs=[spec, spec, spec, spec, vec, vec],
        out_specs=spec,
        out_shape=jax.ShapeDtypeStruct((bsz, t, w), BF16),
        compiler_params=_cparams("parallel", "parallel"),
        name="rwkv_post",
    )(yf, yb, bonus, g, lnx_w.reshape(1, w), lnx_b.reshape(1, w))


def _sgu_kernel(h_ref, lg_ref, lb_ref, ws_ref, bs_ref, o_ref):
    bw = BRANCH_W
    tt = h_ref.shape[1]
    u = _gelu_tanh(h_ref[0, :, 0:bw].astype(F32))
    vv = _gelu_tanh(h_ref[0, :, bw:2 * bw].astype(F32))
    g = h_ref[0, :, 2 * bw:3 * bw].astype(F32)
    mu = jnp.mean(vv, axis=-1, keepdims=True)
    d = vv - mu
    var = jnp.mean(d * d, axis=-1, keepdims=True)
    vn = (d * lax.rsqrt(var + 1e-5) * lg_ref[...] + lb_ref[...]).astype(BF16)
    gate = u * _silu(g)
    for ck in range(tt // SG_CHUNK):
        rs = slice(ck * SG_CHUNK, (ck + 1) * SG_CHUNK)
        for grp in range(SG_GROUPS):
            cs = slice(grp * LANES, (grp + 1) * LANES)
            sv = _dot(ws_ref[grp], vn[rs, cs]) + bs_ref[:, cs]
            o_ref[0, rs, cs] = (gate[rs, cs] * sv).astype(o_ref.dtype)


def _sgu(h_b, ln_g, ln_b, w_s, bs_cols, tt=256):
    bsz, t, w3 = h_b.shape
    bw = BRANCH_W
    tt = min(tt, t)
    vec = pl.BlockSpec((1, bw), lambda b, i: (0, 0))
    return pl.pallas_call(
        _sgu_kernel,
        grid=(bsz, t // tt),
        in_specs=[pl.BlockSpec((1, tt, w3), lambda b, i: (b, i, 0)), vec, vec,
                  pl.BlockSpec((SG_GROUPS, SG_CHUNK, SG_CHUNK), lambda b, i: (0, 0, 0)),
                  pl.BlockSpec((SG_CHUNK, bw), lambda b, i: (0, 0))],
        out_specs=pl.BlockSpec((1, tt, bw), lambda b, i: (b, i, 0)),
        out_shape=jax.ShapeDtypeStruct((bsz, t, bw), BF16),
        compiler_params=_cparams("parallel", "parallel"),
        name="spatial_gating",
    )(h_b, ln_g.reshape(1, bw), ln_b.reshape(1, bw), w_s, bs_cols)


def _natten_bias_table(rpb):
    p = np.arange(GRID_W)[:, None]
    m = np.arange(GRID_W)[None, :]
    sj = np.clip(p - NA_COLS // 2, 0, GRID_W - NA_COLS)
    valid = (m >= sj) & (m < sj + NA_COLS)
    dc = np.clip(m - p, -(NA_COLS - 1), NA_COLS - 1) + NA_COLS - 1
    by_rel = jnp.where(valid[None, None], rpb[:, :, dc], NEG_INF)
    tab = jnp.stack([by_rel[:, s:s + NA_ROWS] for s in range(NA_ROWS)], axis=1)
    tab = tab.transpose(0, 1, 3, 2, 4)
    return tab.reshape(rpb.shape[0], NA_ROWS, GRID_W, NA_ROWS * GRID_W).astype(F32)


NA_UNROLL = 8


def _natten_kernel(q_ref, k_ref, v_ref, g_ref, qn_ref, kn_ref, bias_ref, o_ref, qs_ref, ks_ref):
    t = q_ref.shape[1]
    n_rows = t // GRID_W
    win = NA_ROWS * GRID_W
    ones_bd = _head_block_ones(NA_HEAD)
    lane = lax.broadcasted_iota(jnp.int32, (1, LANES), 1)
    m0 = lane < NA_HEAD

    def norm(x_ref, gain_ref, scale):
        x = x_ref[0].astype(F32)
        ms = _head_sum(x * x, ones_bd) * (1.0 / NA_HEAD)
        return x * lax.rsqrt(ms + 1e-6) * (gain_ref[...] * scale)

    qs_ref[...] = norm(q_ref, qn_ref, NA_HEAD ** -0.5).astype(BF16)
    ks_ref[...] = norm(k_ref, kn_ref, 1.0).astype(BF16)

    def body(it, carry):
        rows = [it * NA_UNROLL + j for j in range(NA_UNROLL)]
        si = [jnp.clip(i - NA_ROWS // 2, 0, n_rows - NA_ROWS) for i in rows]
        start = [s - i + (NA_ROWS - 1) for s, i in zip(si, rows)]
        qo = [pl.multiple_of(i * GRID_W, GRID_W) for i in rows]
        ko = [pl.multiple_of(s * GRID_W, GRID_W) for s in si]
        q = [qs_ref[pl.ds(o, GRID_W), :] for o in qo]
        zero = jnp.zeros_like(q[0])
        q2 = [jnp.concatenate([jnp.where(m0, x, zero), jnp.where(m0, zero, x)], axis=0) for x in q]
        s = [_dot_nt(x, ks_ref[pl.ds(o, win), :]) for x, o in zip(q2, ko)]
        s = [x + jnp.concatenate([bias_ref[0, st], bias_ref[1, st]], axis=0) for x, st in zip(s, start)]
        e = [jnp.exp(x - jnp.max(x, axis=-1, keepdims=True)) for x in s]
        p = [(x / jnp.sum(x, axis=-1, keepdims=True)).astype(BF16) for x in e]
        o2 = [_dot(x, v_ref[0, pl.ds(o, win), :]) for x, o in zip(p, ko)]
        for x, o in zip(o2, qo):
            g = g_ref[0, pl.ds(o, GRID_W), :].astype(F32)
            val = jnp.where(m0, x[0:GRID_W], x[GRID_W:2 * GRID_W]) * _silu(g)
            o_ref[0, pl.ds(o, GRID_W), :] = val.astype(o_ref.dtype)
        return carry

    lax.fori_loop(0, n_rows // NA_UNROLL, body, 0)


def _natten(h_c, q_norm, k_norm, bias_tab):
    bsz, t, w4 = h_c.shape
    bw = BRANCH_W
    nlb = bw // LANES
    sec = lambda s: pl.BlockSpec((1, t, LANES), functools.partial(lambda b, hp, s: (b, 0, s * nlb + hp), s=s))
    two = lambda a: jnp.concatenate([a, a]).reshape(1, LANES)
    return pl.pallas_call(
        _natten_kernel,
        grid=(bsz, nlb),
        in_specs=[sec(0), sec(1), sec(2), sec(3),
                  pl.BlockSpec((1, LANES), lambda b, hp: (0, 0)), pl.BlockSpec((1, LANES), lambda b, hp: (0, 0)),
                  pl.BlockSpec((2, NA_ROWS, GRID_W, NA_ROWS * GRID_W), lambda b, hp: (hp, 0, 0, 0))],
        out_specs=pl.BlockSpec((1, t, LANES), lambda b, hp: (b, 0, hp)),
        out_shape=jax.ShapeDtypeStruct((bsz, t, bw), BF16),
        scratch_shapes=[pltpu.VMEM((t, LANES), BF16), pltpu.VMEM((t, LANES), BF16)],
        compiler_params=_cparams("parallel", "parallel"),
        name="neighbourhood_attention",
    )(h_c, h_c, h_c, h_c, two(q_norm), two(k_norm), bias_tab)


def _memattn_kernel(q_ref, g_ref, k_ref, v_ref, qn_ref, kn_ref, o_ref):
    def norm(x, gain, scale):
        ms = jnp.mean(x * x, axis=-1, keepdims=True)
        return x * lax.rsqrt(ms + 1e-6) * (gain * scale)

    q = norm(q_ref[0].astype(F32), qn_ref[...], MEM_HEAD ** -0.5).astype(BF16)
    k = norm(k_ref[0].astype(F32), kn_ref[...], 1.0).astype(BF16)
    s = _dot_nt(q, k)
    s = s - jnp.max(s, axis=-1, keepdims=True)
    e = jnp.exp(s)
    p = e / jnp.sum(e, axis=-1, keepdims=True)
    o = _dot(p.astype(BF16), v_ref[0])
    o_ref[0] = (o * _silu(g_ref[0].astype(F32))).astype(o_ref.dtype)


def _memattn(h_m, kv, q_norm, k_norm, tt=512):
    bsz, t, _ = h_m.shape
    mlen = kv.shape[1]
    tt = min(tt, t)
    hd = MEM_HEAD
    vec = pl.BlockSpec((1, hd), lambda b, i, h: (0, 0))
    return pl.pallas_call(
        _memattn_kernel,
        grid=(bsz, t // tt, MEM_HEADS),
        in_specs=[pl.BlockSpec((1, tt, hd), lambda b, i, h: (b, i, h)),
                  pl.BlockSpec((1, tt, hd), lambda b, i, h: (b, i, MEM_HEADS + h)),
                  pl.BlockSpec((1, mlen, hd), lambda b, i, h: (b, 0, h)),
                  pl.BlockSpec((1, mlen, hd), lambda b, i, h: (b, 0, MEM_HEADS + h)),
                  vec, vec],
        out_specs=pl.BlockSpec((1, tt, hd), lambda b, i, h: (b, i, h)),
        out_shape=jax.ShapeDtypeStruct((bsz, t, BRANCH_W), BF16),
        compiler_params=_cparams("parallel", "parallel", "parallel"),
        name="memory_attention",
    )(h_m, h_m, kv, kv, q_norm.reshape(1, hd), k_norm.reshape(1, hd))


def _pad_lora(up):
    z = jnp.zeros_like(up[0])
    w = jnp.stack([jnp.concatenate([up[0], z], axis=0), jnp.concatenate([z, up[1]], axis=0)])
    hi = w.astype(BF16)
    return jnp.stack([hi, (w - hi.astype(F32)).astype(BF16)])


def _layer(x2d, mem2d, bsz, layer, p, big):
    m, d = x2d.shape
    t = m // bsz
    bw = BRANCH_W
    a_w = A_SHIFT_W + bw
    o1, o2, o3, o4 = a_w, a_w + 3 * bw, a_w + 7 * bw, a_w + 9 * bw
    proj = functools.partial(_matmul, _rmsnorm(x2d, p["norm_g"], BF16), big["w_in"], layer)

    h_rkv = proj(0, 3 * bw, BF16, 1024, 1024, name="proj_a_rkv")
    h_lora = proj(3 * bw, 4 * LORA, BF16, 1024, 4 * LORA, name="proj_a_lora")
    h_ag = proj(A_SHIFT_W, bw, BF16, 1024, 1024, name="proj_a_gate")
    h_b = proj(o1, 3 * bw, BF16, 1024, 1024, name="proj_b")
    h_c = proj(o2, 4 * bw, BF16, 1024, 1024, name="proj_c")
    h_m = proj(o3, 2 * bw, BF16, 1024, 1024, name="proj_m")
    gates = proj(o4, 4 * d, BF16, 1024, 1024, act="sigmoid", name="proj_gates")

    prep = _rwkv_prep(h_rkv.reshape(bsz, t, 3 * bw), h_lora.reshape(bsz, t, 4 * LORA), p["a_conv"],
                      _pad_lora(p["a_w_up"]), p["a_w0"], _pad_lora(p["a_a_up"]), p["a_a0"],
                      p["a_k_k"], p["a_k_a"], p["a_r_k"].reshape(-1))
    r, v, kk, lw0, lw1, kt0, kt1, b0, b1, bonus = prep
    yf, yb = _wkv(r, v, kk, lw0, lw1, kt0, kt1, b0, b1)
    y_a = _rwkv_post(yf, yb, bonus, h_ag.reshape(bsz, t, bw), p["a_lnx_w"], p["a_lnx_b"])

    bs_cols = jnp.repeat(p["b_b_s"].T, SG_CHUNK, axis=1)
    y_b = _sgu(h_b.reshape(bsz, t, 3 * bw), p["b_ln_g"], p["b_ln_b"], p["b_w_s"].astype(BF16), bs_cols)

    y_c = _natten(h_c.reshape(bsz, t, 4 * bw), p["c_q_norm"], p["c_k_norm"], _natten_bias_table(p["c_rpb"]))

    mem_n = _rmsnorm(mem2d, p["m_norm_g"], BF16)
    kv = _matmul(mem_n, big["m_w_kv"], layer, 0, 2 * bw, BF16, 1024, 1024, name="proj_mem_kv")
    y_d = _memattn(h_m.reshape(bsz, t, 2 * bw), kv.reshape(bsz, -1, 2 * bw), p["m_q_norm"], p["m_k_norm"])

    ys = [y.reshape(m, bw) for y in (y_a, y_b, y_c, y_d)]
    merged = _merge(ys, big["w_branch"], layer, gates, 512, 1024)
    return _matmul_residual(merged, big["w_out"], layer, x2d, 512, 1024)


def kernel(x, mem, norm_g, w_in, a_conv, a_w_up, a_w0, a_a_up, a_a0, a_k_k, a_k_a, a_r_k, a_lnx_w, a_lnx_b,
           b_ln_g, b_ln_b, b_w_s, b_b_s, c_q_norm, c_k_norm, c_rpb, m_norm_g, m_w_kv, m_q_norm, m_k_norm,
           w_branch, w_out):
    params = dict(norm_g=norm_g, a_conv=a_conv, a_w_up=a_w_up, a_w0=a_w0, a_a_up=a_a_up, a_a0=a_a0,
                  a_k_k=a_k_k, a_k_a=a_k_a, a_r_k=a_r_k, a_lnx_w=a_lnx_w, a_lnx_b=a_lnx_b, b_ln_g=b_ln_g,
                  b_ln_b=b_ln_b, b_w_s=b_w_s, b_b_s=b_b_s, c_q_norm=c_q_norm, c_k_norm=c_k_norm, c_rpb=c_rpb,
                  m_norm_g=m_norm_g, m_q_norm=m_q_norm, m_k_norm=m_k_norm)
    big = dict(w_in=w_in.astype(BF16), m_w_kv=m_w_kv.astype(BF16), w_branch=w_branch.astype(BF16),
               w_out=w_out.astype(BF16))
    bsz, t, d = x.shape
    x2d = x.reshape(bsz * t, d)
    mem2d = mem.reshape(-1, d)
    for l in range(norm_g.shape[0]):
        x2d = _layer(x2d, mem2d, bsz, l, {k: v[l] for k, v in params.items()}, big)
    return x2d.reshape(bsz, t, d)
```

```python
import functools

import numpy as np
import jax
import jax.numpy as jnp
from jax import lax
from jax.experimental import pallas as pl
from jax.experimental.pallas import tpu as pltpu

F32 = jnp.float32
BF16 = jnp.bfloat16

VMEM_LIMIT_BYTES = 56 * 1024 * 1024
LANES = 128
BF16_SUBLANES = 16

BRANCH_W = 1024
RWKV_HEAD = 64
LORA = 64
A_SHIFT_W = 3 * BRANCH_W + 4 * LORA
LNX_EPS = 64e-5
SG_CHUNK = 128
SG_GROUPS = 8
NA_HEAD = 64
NA_ROWS = 8
NA_COLS = 16
GRID_W = 64
MEM_HEADS = 4
MEM_HEAD = 256
WKV_CHUNK = 64
NEG_INF = -1e30


def _cparams(*sem):
    return pltpu.CompilerParams(dimension_semantics=sem, vmem_limit_bytes=VMEM_LIMIT_BYTES)


def _sigmoid(x):
    return 1.0 / (1.0 + jnp.exp(-x))


def _silu(x):
    return x * _sigmoid(x)


def _gelu_tanh(x):
    c = np.float32(np.sqrt(2.0 / np.pi))
    return 0.5 * x * (1.0 + jnp.tanh(c * (x + 0.044715 * (x * x * x))))


def _dot(a, b, precision=None):
    return jnp.dot(a, b, preferred_element_type=F32, precision=precision)


def _dot_nt(a, b, precision=None):
    return lax.dot_general(a, b, (((1,), (1,)), ((), ())), preferred_element_type=F32,
                           precision=precision)


def _dot_tn(a, b, precision=None):
    return lax.dot_general(a, b, (((0,), (0,)), ((), ())), preferred_element_type=F32,
                           precision=precision)


def _split_bf16(x, terms):
    parts = []
    for _ in range(terms - 1):
        hi = x.astype(BF16)
        parts.append(hi)
        x = x - hi.astype(F32)
    parts.append(x.astype(BF16))
    return parts


def _dot_exact_rhs(x, w_bf16, terms):
    acc = None
    for piece in _split_bf16(x, terms):
        d = _dot(piece, w_bf16)
        acc = d if acc is None else acc + d
    return acc


def _head_block_ones(head):
    shift = int(np.log2(head))
    r = lax.broadcasted_iota(jnp.int32, (LANES, LANES), 0) >> shift
    c = lax.broadcasted_iota(jnp.int32, (LANES, LANES), 1) >> shift
    return jnp.where(r == c, 1.0, 0.0).astype(BF16)


def _head_sum(x, ones_bd, terms=1):
    parts = [_dot_exact_rhs(x[:, j:j + LANES], ones_bd, terms) for j in range(0, x.shape[1], LANES)]
    return parts[0] if len(parts) == 1 else jnp.concatenate(parts, axis=1)


def _rmsnorm_kernel(x_ref, g_ref, o_ref, *, eps):
    x = x_ref[...].astype(F32)
    ms = jnp.mean(x * x, axis=-1, keepdims=True)
    o_ref[...] = (x * lax.rsqrt(ms + eps) * g_ref[...]).astype(o_ref.dtype)


def _rmsnorm(x2d, g, out_dtype, tm=256, eps=1e-6):
    m, d = x2d.shape
    tm = min(tm, m)
    return pl.pallas_call(
        functools.partial(_rmsnorm_kernel, eps=eps),
        grid=(m // tm,),
        in_specs=[pl.BlockSpec((tm, d), lambda i: (i, 0)), pl.BlockSpec((1, d), lambda i: (0, 0))],
        out_specs=pl.BlockSpec((tm, d), lambda i: (i, 0)),
        out_shape=jax.ShapeDtypeStruct((m, d), out_dtype),
        compiler_params=_cparams("parallel"),
        name="rmsnorm",
    )(x2d, g.reshape(1, d))


def _mm_kernel(a_ref, b_ref, o_ref, *, act):
    acc = _dot(a_ref[...], b_ref[0])
    if act == "sigmoid":
        acc = _sigmoid(acc)
    o_ref[...] = acc.astype(o_ref.dtype)


def _matmul(a, w, layer, col0, n, out_dtype, tm, tn, act=None, name="matmul"):
    m, k = a.shape
    tm, tn = min(tm, m), min(tn, n)
    return pl.pallas_call(
        functools.partial(_mm_kernel, act=act),
        grid=(m // tm, n // tn),
        in_specs=[pl.BlockSpec((tm, k), lambda i, j: (i, 0)),
                  pl.BlockSpec((pl.Element(1), pl.Element(k), pl.Element(tn)),
                               lambda i, j: (layer, 0, pl.multiple_of(col0 + j * tn, LANES)))],
        out_specs=pl.BlockSpec((tm, tn), lambda i, j: (i, j)),
        out_shape=jax.ShapeDtypeStruct((m, n), out_dtype),
        compiler_params=_cparams("parallel", "parallel"),
        name=name,
    )(a, w)


def _mm_residual_kernel(a_ref, b_ref, x_ref, o_ref):
    o_ref[...] = x_ref[...] + _dot(a_ref[...], b_ref[...])


def _matmul_residual(a, w, layer, x, tm, tn):
    m, k = a.shape
    n = w.shape[2]
    tm, tn = min(tm, m), min(tn, n)
    return pl.pallas_call(
        _mm_residual_kernel,
        grid=(n // tn, m // tm),
        in_specs=[pl.BlockSpec((tm, k), lambda j, i: (i, 0)),
                  pl.BlockSpec((None, k, tn), lambda j, i: (layer, 0, j)),
                  pl.BlockSpec((tm, tn), lambda j, i: (i, j))],
        out_specs=pl.BlockSpec((tm, tn), lambda j, i: (i, j)),
        out_shape=jax.ShapeDtypeStruct((m, n), x.dtype),
        compiler_params=_cparams("parallel", "parallel"),
        name="out_proj_residual",
    )(a, w, x)


def _merge_kernel(y0, y1, y2, y3, wb_ref, g0, g1, g2, g3, o_ref):
    acc = None
    for n, (y, g) in enumerate(((y0, g0), (y1, g1), (y2, g2), (y3, g3))):
        term = g[...].astype(F32) * _dot(y[...], wb_ref[n])
        acc = term if acc is None else acc + term
    o_ref[...] = acc.astype(o_ref.dtype)


def _merge(ys, wb, layer, gates, tm, tn):
    m, bw = ys[0].shape
    d = wb.shape[3]
    tm, tn = min(tm, m), min(tn, d)
    nj = d // tn
    y_spec = pl.BlockSpec((tm, bw), lambda j, i: (i, 0))
    g_specs = [pl.BlockSpec((tm, tn), functools.partial(lambda j, i, n: (i, n * nj + j), n=n))
               for n in range(4)]
    return pl.pallas_call(
        _merge_kernel,
        grid=(nj, m // tm),
        in_specs=[y_spec] * 4 + [pl.BlockSpec((None, 4, bw, tn), lambda j, i: (layer, 0, 0, j))] + g_specs,
        out_specs=pl.BlockSpec((tm, tn), lambda j, i: (i, j)),
        out_shape=jax.ShapeDtypeStruct((m, d), BF16),
        compiler_params=_cparams("parallel", "parallel"),
        name="gated_merge",
    )(*ys, wb, gates, gates, gates, gates)


def _token_shift(h_ref, hp_ref, hn_ref, conv_ref):
    i = pl.program_id(1)
    nt = pl.num_programs(1)
    h = h_ref[0].astype(F32)
    tt = h.shape[0]
    prev_row = hp_ref[0, BF16_SUBLANES - 1:BF16_SUBLANES, :].astype(F32) * (i > 0).astype(F32)
    next_row = hn_ref[0, 0:1, :].astype(F32) * (i < nt - 1).astype(F32)
    rows = lax.broadcasted_iota(jnp.int32, (tt, 1), 0)
    h_dn = jnp.where(rows == 0, prev_row, pltpu.roll(h, 1, 0))
    h_up = jnp.where(rows == tt - 1, next_row, pltpu.roll(h, tt - 1, 0))
    return h_dn * conv_ref[0:1, :] + h * conv_ref[1:2, :] + h_up * conv_ref[2:3, :]


def _rwkv_prep_kernel(h_ref, hp_ref, hn_ref, l_ref, lp_ref, ln_ref, conv_ref, convl_ref, wup_ref, w0_ref,
                      aup_ref, a0_ref, kk_ref, ka_ref, rk_ref,
                      r_o, v_o, kk_o, lw0_o, lw1_o, kt0_o, kt1_o, b0_o, b1_o, bonus_o):
    hs = _token_shift(h_ref, hp_ref, hn_ref, conv_ref)
    ls = _token_shift(l_ref, lp_ref, ln_ref, convl_ref)
    bw = BRANCH_W
    r = hs[:, 0:bw]
    k = hs[:, bw:2 * bw]
    v = hs[:, 2 * bw:3 * bw]
    wd = jnp.tanh(ls[:, 0:2 * LORA])
    ad = ls[:, 2 * LORA:4 * LORA]

    ones_bd = _head_block_ones(RWKV_HEAD)
    kkr = k * kk_ref[...]
    nrm = jnp.sqrt(_head_sum(kkr * kkr, ones_bd))
    kk = kkr / jnp.maximum(nrm, 1e-12)

    def up_proj(x_pieces, w_ref, z):
        (x_hi, x_lo), w_hi, w_lo = x_pieces, w_ref[0, z], w_ref[1, z]
        return _dot(x_hi, w_hi) + (_dot(x_lo, w_hi) + _dot(x_hi, w_lo))

    wd_pieces, ad_pieces = _split_bf16(wd, 2), _split_bf16(ad, 2)
    decay_scale = np.float32(np.exp(-0.5))
    kts = []
    for z, (lw_o, kt_o, b_o) in enumerate(((lw0_o, kt0_o, b0_o), (lw1_o, kt1_o, b1_o))):
        w_raw = w0_ref[z:z + 1, :] + up_proj(wd_pieces, wup_ref, z)
        lw_o[0] = -decay_scale * _sigmoid(w_raw)
        a = _sigmoid(a0_ref[z:z + 1, :] + up_proj(ad_pieces, aup_ref, z))
        kt = k * (1.0 + (a - 1.0) * ka_ref[...])
        kt_o[0] = kt
        b_o[0] = kk * a
        kts.append(kt)
    r_o[0] = r
    v_o[0] = v
    kk_o[0] = kk
    bonus_o[0] = _head_sum(r * (kts[0] + kts[1]) * rk_ref[...], ones_bd) * v


def _rwkv_prep(h_rkv, h_lora, conv, wup_pad, w0, aup_pad, a0, k_k, k_a, r_k, tt=256):
    bsz, t, w = h_rkv.shape
    wl = h_lora.shape[2]
    tt = min(tt, t)
    nt = t // tt
    hb = tt // BF16_SUBLANES
    n_halo = t // BF16_SUBLANES
    row = lambda a: a.reshape(1, -1)
    vec_spec = pl.BlockSpec((1, BRANCH_W), lambda b, i: (0, 0))
    out_spec = pl.BlockSpec((1, tt, BRANCH_W), lambda b, i: (b, i, 0))
    out_sds = jax.ShapeDtypeStruct((bsz, t, BRANCH_W), F32)
    tile = lambda width: [
        pl.BlockSpec((1, tt, width), lambda b, i: (b, i, 0)),
        pl.BlockSpec((1, BF16_SUBLANES, width), lambda b, i: (b, jnp.maximum(i * hb - 1, 0), 0)),
        pl.BlockSpec((1, BF16_SUBLANES, width), lambda b, i: (b, jnp.minimum((i + 1) * hb, n_halo - 1), 0))]
    return pl.pallas_call(
        _rwkv_prep_kernel,
        grid=(bsz, nt),
        in_specs=tile(w) + tile(wl) + [
            pl.BlockSpec((3, w), lambda b, i: (0, 0)),
            pl.BlockSpec((3, wl), lambda b, i: (0, 0)),
            pl.BlockSpec((2, 2, 2 * LORA, BRANCH_W), lambda b, i: (0, 0, 0, 0)),
            pl.BlockSpec((2, BRANCH_W), lambda b, i: (0, 0)),
            pl.BlockSpec((2, 2, 2 * LORA, BRANCH_W), lambda b, i: (0, 0, 0, 0)),
            pl.BlockSpec((2, BRANCH_W), lambda b, i: (0, 0)),
            vec_spec, vec_spec, vec_spec,
        ],
        out_specs=[out_spec] * 10,
        out_shape=[out_sds] * 10,
        compiler_params=_cparams("parallel", "parallel"),
        name="rwkv_prep",
    )(h_rkv, h_rkv, h_rkv, h_lora, h_lora, h_lora, conv[:, :w], conv[:, w:], wup_pad, w0, aup_pad, a0,
      row(k_k), row(k_a), row(r_k))


def _wkv_masks():
    c = WKV_CHUNK
    row = lax.broadcasted_iota(jnp.int32, (2 * c, 2 * c), 0)
    col = lax.broadcasted_iota(jnp.int32, (2 * c, 2 * c), 1)
    same = (row >> 6) == (col >> 6)
    rt, ct = row & (c - 1), col & (c - 1)
    f = lambda m: jnp.where(same & m, 1.0, 0.0).astype(F32)
    lane = lax.broadcasted_iota(jnp.int32, (1, LANES), 1)
    crow = lax.broadcasted_iota(jnp.int32, (c, c), 0)
    ccol = lax.broadcasted_iota(jnp.int32, (c, c), 1)
    return {
        "eye": f(rt == ct),
        "strict": (f(ct < rt), f(ct > rt)),
        "incl": (f(ct <= rt), f(ct >= rt)),
        "cum": (jnp.where(ccol <= crow, 1.0, 0.0).astype(BF16), jnp.where(ccol >= crow, 1.0, 0.0).astype(BF16)),
        "head0": jnp.where(lane < RWKV_HEAD, 1.0, 0.0).astype(F32),
        "head1": jnp.where(lane < RWKV_HEAD, 0.0, 1.0).astype(F32),
    }


def _wkv_local(problems, masks, fillers=()):
    c = WKV_CHUNK
    bf = lambda x: x.astype(BF16)
    m0, m1 = masks["head0"], masks["head1"]
    pair = lambda x: jnp.concatenate([x * m0, x * m1], axis=0)
    each = lambda fn, *lists: [fn(*xs) for xs in zip(*lists)]
    dirs = [p[6] for p in problems]
    fillers = list(fillers)
    n_points = 16
    stride = max(1, n_points // max(1, len(fillers)))
    seen = [0]

    def fill():
        seen[0] += 1
        if fillers and seen[0] % stride == 0:
            fillers.pop(0)()

    cl = [_cumsum_dot(masks["cum"][p[6]], p[0]) for p in problems]
    tot = [x[c - 1:c, :] if d == 0 else x[0:1, :] for x, d in zip(cl, dirs)]
    fill()
    zp = [pair(-p[4] * jnp.exp(x - p[0])) for p, x in zip(problems, cl)]
    rp = [pair(p[1] * jnp.exp(x)) for p, x in zip(problems, cl)]
    vpb = [bf(pair(p[3])) for p in problems]
    e_neg = [jnp.exp(-x) for x in cl]
    bk_start = [bf(jnp.concatenate([pair(p[5] * e), pair(p[2] * e)], axis=0)) for p, e in zip(problems, e_neg)]
    e_end = [jnp.exp(t - x) for t, x in zip(tot, cl)]
    bk_end = [bf(jnp.concatenate([pair(p[5] * e), pair(p[2] * e)], axis=0)) for p, e in zip(problems, e_end)]
    fill()

    scores = each(lambda z, r, bk: _dot_nt(bf(jnp.concatenate([z, r], axis=0)), bk), zp, rp, bk_start)
    fill()
    strict = [masks["strict"][d] for d in dirs]
    incl = [masks["incl"][d] for d in dirs]
    l_zb = each(lambda s, m: s[0:2 * c, 0:2 * c] * m, scores, strict)
    a_zk = each(lambda s, m: bf(s[0:2 * c, 2 * c:4 * c] * m), scores, strict)
    a_r = each(lambda s, m: bf(jnp.concatenate([s[2 * c:4 * c, 0:2 * c] * m, s[2 * c:4 * c, 2 * c:4 * c] * m],
                                               axis=1)), scores, incl)
    azk_v = each(_dot, a_zk, vpb)
    fill()

    inv = [masks["eye"] + l for l in l_zb]
    pb = [bf(l) for l in l_zb]
    for _ in range(5):
        pb = [bf(_dot(x, x)) for x in pb]
        fill()
        inv = each(lambda t, x: t + _dot(bf(t), x), inv, pb)
        fill()

    zu = each(lambda t, z, u: _dot(bf(t), bf(jnp.concatenate([z, u], axis=1))), inv, zp, azk_v)
    fill()
    stack = each(lambda x, v: jnp.concatenate([bf(x), jnp.concatenate([jnp.zeros_like(v), v], axis=1)], axis=0),
                 zu, vpb)
    ry = each(_dot, a_r, stack)
    fill()
    gh = each(_dot_tn, stack, bk_end)
    while fillers:
        fillers.pop(0)()
    out = []
    for r, y, g, t in zip(rp, ry, gh, tot):
        rb = r + y[:, 0:2 * c]
        out.append((rb[0:c] + rb[c:2 * c], y[0:c, 2 * c:4 * c] + y[c:2 * c, 2 * c:4 * c],
                    g[0:2 * c], g[2 * c:4 * c], jnp.exp(t)))
    return out


def _cumsum_dot(cum_bf16, lw):
    acc = None
    for piece in _split_bf16(lw, 3):
        d = _dot(cum_bf16, piece)
        acc = d if acc is None else acc + d
    return acc


def _wkv_kernel(r_f, v_f, kk_f, lw_f, kt_f, b_f, r_b, v_b, kk_b, lw_b, kt_b, b_b, yf_o, yb_o,
                s_ref, rb_ref, g_ref, h_ref, wc_ref):
    @pl.when(pl.program_id(2) == 0)
    def _():
        s_ref[...] = jnp.zeros_like(s_ref)

    c = WKV_CHUNK
    n_chunks = r_f.shape[1] // c
    half = (5 * n_chunks) // 8
    masks = _wkv_masks()
    ins = ((lw_f, r_f, kt_f, v_f, kk_f, b_f), (lw_b, r_b, kt_b, v_b, kk_b, b_b))
    outs = (yf_o, yb_o)
    states = [s_ref[0], s_ref[1]]
    chunk_of = lambda d, step: step if d == 0 else n_chunks - 1 - step

    def local_factors(steps, fillers):
        where = [(d, chunk_of(d, s)) for s in steps for d in range(2)]
        problems = [[ref[0, ci * c:(ci + 1) * c, :] for ref in ins[d]] + [d] for d, ci in where]
        for (d, ci), (rb, yloc, g, h, wc) in zip(where, _wkv_local(problems, masks, fillers)):
            rb_ref[d, ci * c:(ci + 1) * c, :] = rb.astype(BF16)
            outs[d][0, ci * c:(ci + 1) * c, :] = yloc
            g_ref[d, ci] = g.astype(BF16)
            h_ref[d, ci] = h
            wc_ref[d, ci] = jnp.broadcast_to(wc, (8, LANES))

    def state_step(step):
        for d in range(2):
            ci = chunk_of(d, step)
            rows = slice(ci * c, (ci + 1) * c)
            sb = states[d].astype(BF16)
            outs[d][0, rows, :] = outs[d][0, rows, :] + _dot_nt(rb_ref[d, rows, :], sb)
            states[d] = states[d] * wc_ref[d, ci, 0:1, :] + _dot(sb, g_ref[d, ci]) + h_ref[d, ci]

    local_factors(range(0, half), ())
    local_factors(range(half, n_chunks), [functools.partial(state_step, s) for s in range(half)])
    for s in range(half, n_chunks):
        state_step(s)
    s_ref[0] = states[0]
    s_ref[1] = states[1]


def _wkv(r, v, kk, lw0, lw1, kt0, kt1, b0, b1, tb=512):
    bsz, t, w = r.shape
    tb = min(tb, t)
    nb = t // tb
    nc = tb // WKV_CHUNK
    fwd = pl.BlockSpec((1, tb, LANES), lambda b, h, g: (b, g, h))
    bwd = pl.BlockSpec((1, tb, LANES), lambda b, h, g: (b, nb - 1 - g, h))
    sds = jax.ShapeDtypeStruct((bsz, t, w), F32)
    return pl.pallas_call(
        _wkv_kernel,
        grid=(bsz, w // LANES, nb),
        in_specs=[fwd] * 6 + [bwd] * 6,
        out_specs=[fwd, bwd],
        out_shape=[sds, sds],
        scratch_shapes=[pltpu.VMEM((2, LANES, LANES), F32), pltpu.VMEM((2, tb, LANES), BF16),
                        pltpu.VMEM((2, nc, LANES, LANES), BF16), pltpu.VMEM((2, nc, LANES, LANES), F32),
                        pltpu.VMEM((2, nc, 8, LANES), F32)],
        compiler_params=_cparams("parallel", "parallel", "arbitrary"),
        name="wkv7_chunked",
    )(r, v, kk, lw0, kt0, b0, r, v, kk, lw1, kt1, b1)


def _rwkv_post_kernel(yf_ref, yb_ref, bonus_ref, g_ref, lw_ref, lb_ref, o_ref):
    ones_bd = _head_block_ones(RWKV_HEAD)
    wkv = yf_ref[0] + yb_ref[0]
    inv_n = 1.0 / RWKV_HEAD
    mu = _head_sum(wkv, ones_bd) * inv_n
    d = wkv - mu
    var = _head_sum(d * d, ones_bd) * inv_n
    gn = d * lax.rsqrt(var + LNX_EPS) * lw_ref[...] + lb_ref[...]
    o_ref[0] = ((gn + bonus_ref[0]) * _silu(g_ref[0].astype(F32))).astype(o_ref.dtype)


def _rwkv_post(yf, yb, bonus, g, lnx_w, lnx_b, tt=512):
    bsz, t, w = yf.shape
    tt = min(tt, t)
    spec = pl.BlockSpec((1, tt, w), lambda b, i: (b, i, 0))
    vec = pl.BlockSpec((1, w), lambda b, i: (0, 0))
    return pl.pallas_call(
        _rwkv_post_kernel,
        grid=(bsz, t // tt),
        in_specs=[spec, spec, spec, spec, vec, vec],
        out_specs=spec,
        out_shape=jax.ShapeDtypeStruct((bsz, t, w), BF16),
        compiler_params=_cparams("parallel", "parallel"),
        name="rwkv_post",
    )(yf, yb, bonus, g, lnx_w.reshape(1, w), lnx_b.reshape(1, w))


def _sgu_kernel(h_ref, lg_ref, lb_ref, ws_ref, bs_ref, o_ref):
    bw = BRANCH_W
    tt = h_ref.shape[1]
    u = _gelu_tanh(h_ref[0, :, 0:bw].astype(F32))
    vv = _gelu_tanh(h_ref[0, :, bw:2 * bw].astype(F32))
    g = h_ref[0, :, 2 * bw:3 * bw].astype(F32)
    mu = jnp.mean(vv, axis=-1, keepdims=True)
    d = vv - mu
    var = jnp.mean(d * d, axis=-1, keepdims=True)
    vn = (d * lax.rsqrt(var + 1e-5) * lg_ref[...] + lb_ref[...]).astype(BF16)
    gate = u * _silu(g)
    for ck in range(tt // SG_CHUNK):
        rs = slice(ck * SG_CHUNK, (ck + 1) * SG_CHUNK)
        for grp in range(SG_GROUPS):
            cs = slice(grp * LANES, (grp + 1) * LANES)
            sv = _dot(ws_ref[grp], vn[rs, cs]) + bs_ref[:, cs]
            o_ref[0, rs, cs] = (gate[rs, cs] * sv).astype(o_ref.dtype)


def _sgu(h_b, ln_g, ln_b, w_s, bs_cols, tt=256):
    bsz, t, w3 = h_b.shape
    bw = BRANCH_W
    tt = min(tt, t)
    vec = pl.BlockSpec((1, bw), lambda b, i: (0, 0))
    return pl.pallas_call(
        _sgu_kernel,
        grid=(bsz, t // tt),
        in_specs=[pl.BlockSpec((1, tt, w3), lambda b, i: (b, i, 0)), vec, vec,
                  pl.BlockSpec((SG_GROUPS, SG_CHUNK, SG_CHUNK), lambda b, i: (0, 0, 0)),
                  pl.BlockSpec((SG_CHUNK, bw), lambda b, i: (0, 0))],
        out_specs=pl.BlockSpec((1, tt, bw), lambda b, i: (b, i, 0)),
        out_shape=jax.ShapeDtypeStruct((bsz, t, bw), BF16),
        compiler_params=_cparams("parallel", "parallel"),
        name="spatial_gating",
    )(h_b, ln_g.reshape(1, bw), ln_b.reshape(1, bw), w_s, bs_cols)


def _natten_bias_table(rpb):
    p = np.arange(GRID_W)[:, None]
    m = np.arange(GRID_W)[None, :]
    sj = np.clip(p - NA_COLS // 2, 0, GRID_W - NA_COLS)
    valid = (m >= sj) & (m < sj + NA_COLS)
    dc = np.clip(m - p, -(NA_COLS - 1), NA_COLS - 1) + NA_COLS - 1
    by_rel = jnp.where(valid[None, None], rpb[:, :, dc], NEG_INF)
    tab = jnp.stack([by_rel[:, s:s + NA_ROWS] for s in range(NA_ROWS)], axis=1)
    tab = tab.transpose(0, 1, 3, 2, 4)
    return tab.reshape(rpb.shape[0], NA_ROWS, GRID_W, NA_ROWS * GRID_W).astype(F32)


NA_UNROLL = 8


def _natten_kernel(q_ref, k_ref, v_ref, g_ref, qn_ref, kn_ref, bias_ref, o_ref, qs_ref, ks_ref):
    t = q_ref.shape[1]
    n_rows = t // GRID_W
    win = NA_ROWS * GRID_W
    ones_bd = _head_block_ones(NA_HEAD)
    lane = lax.broadcasted_iota(jnp.int32, (1, LANES), 1)
    m0 = lane < NA_HEAD

    def norm(x_ref, gain_ref, scale):
        x = x_ref[0].astype(F32)
        ms = _head_sum(x * x, ones_bd) * (1.0 / NA_HEAD)
        return x * lax.rsqrt(ms + 1e-6) * (gain_ref[...] * scale)

    qs_ref[...] = norm(q_ref, qn_ref, NA_HEAD ** -0.5).astype(BF16)
    ks_ref[...] = norm(k_ref, kn_ref, 1.0).astype(BF16)

    def body(it, carry):
        rows = [it * NA_UNROLL + j for j in range(NA_UNROLL)]
        si = [jnp.clip(i - NA_ROWS // 2, 0, n_rows - NA_ROWS) for i in rows]
        start = [s - i + (NA_ROWS - 1) for s, i in zip(si, rows)]
        qo = [pl.multiple_of(i * GRID_W, GRID_W) for i in rows]
        ko = [pl.multiple_of(s * GRID_W, GRID_W) for s in si]
        q = [qs_ref[pl.ds(o, GRID_W), :] for o in qo]
        zero = jnp.zeros_like(q[0])
        q2 = [jnp.concatenate([jnp.where(m0, x, zero), jnp.where(m0, zero, x)], axis=0) for x in q]
        s = [_dot_nt(x, ks_ref[pl.ds(o, win), :]) for x, o in zip(q2, ko)]
        s = [x + jnp.concatenate([bias_ref[0, st], bias_ref[1, st]], axis=0) for x, st in zip(s, start)]
        e = [jnp.exp(x - jnp.max(x, axis=-1, keepdims=True)).astype(BF16) for x in s]
        o2 = [_dot(x, jnp.concatenate([v_ref[0, pl.ds(o, win), :], ones_blk], axis=1))
              for x, o in zip(e, ko)]
        for x, o in zip(o2, qo):
            x = x[:, 0:LANES] / x[:, LANES:2 * LANES]
            g = g_ref[0, pl.ds(o, GRID_W), :].astype(F32)
            val = jnp.where(m0, x[0:GRID_W], x[GRID_W:2 * GRID_W]) * _silu(g)
            o_ref[0, pl.ds(o, GRID_W), :] = val.astype(o_ref.dtype)
        return carry

    ones_blk = jnp.ones((win, LANES), BF16)

    lax.fori_loop(0, n_rows // NA_UNROLL, body, 0)


def _natten(h_c, q_norm, k_norm, bias_tab):
    bsz, t, w4 = h_c.shape
    bw = BRANCH_W
    nlb = bw // LANES
    sec = lambda s: pl.BlockSpec((1, t, LANES), functools.partial(lambda b, hp, s: (b, 0, s * nlb + hp), s=s))
    two = lambda a: jnp.concatenate([a, a]).reshape(1, LANES)
    return pl.pallas_call(
        _natten_kernel,
        grid=(bsz, nlb),
        in_specs=[sec(0), sec(1), sec(2), sec(3),
                  pl.BlockSpec((1, LANES), lambda b, hp: (0, 0)), pl.BlockSpec((1, LANES), lambda b, hp: (0, 0)),
                  pl.BlockSpec((2, NA_ROWS, GRID_W, NA_ROWS * GRID_W), lambda b, hp: (hp, 0, 0, 0))],
        out_specs=pl.BlockSpec((1, t, LANES), lambda b, hp: (b, 0, hp)),
        out_shape=jax.ShapeDtypeStruct((bsz, t, bw), BF16),
        scratch_shapes=[pltpu.VMEM((t, LANES), BF16), pltpu.VMEM((t, LANES), BF16)],
        compiler_params=_cparams("parallel", "parallel"),
        name="neighbourhood_attention",
    )(h_c, h_c, h_c, h_c, two(q_norm), two(k_norm), bias_tab)


def _memattn_kernel(qg_ref, kv_ref, qn_ref, kn_ref, o_ref):
    hd, bw = MEM_HEAD, BRANCH_W

    def norm(x, gain, scale):
        ms = jnp.mean(x * x, axis=-1, keepdims=True)
        return (x * lax.rsqrt(ms + 1e-6) * (gain * scale)).astype(BF16)

    heads = [slice(h * hd, (h + 1) * hd) for h in range(MEM_HEADS)]
    q = [norm(qg_ref[0, :, c].astype(F32), qn_ref[...], hd ** -0.5) for c in heads]
    k = [norm(kv_ref[0, :, c].astype(F32), kn_ref[...], 1.0) for c in heads]
    s = [_dot_nt(a, b) for a, b in zip(q, k)]
    e = [jnp.exp(x - jnp.max(x, axis=-1, keepdims=True)) for x in s]
    p = [(x / jnp.sum(x, axis=-1, keepdims=True)).astype(BF16) for x in e]
    o = [_dot(x, kv_ref[0, :, bw + h * hd:bw + (h + 1) * hd]) for h, x in enumerate(p)]
    for c, x in zip(heads, o):
        g = qg_ref[0, :, bw + c.start:bw + c.stop].astype(F32)
        o_ref[0, :, c] = (x * _silu(g)).astype(o_ref.dtype)


def _memattn(h_m, kv, q_norm, k_norm, tt=512):
    bsz, t, w2 = h_m.shape
    mlen = kv.shape[1]
    tt = min(tt, t)
    vec = pl.BlockSpec((1, MEM_HEAD), lambda b, i: (0, 0))
    return pl.pallas_call(
        _memattn_kernel,
        grid=(bsz, t // tt),
        in_specs=[pl.BlockSpec((1, tt, w2), lambda b, i: (b, i, 0)),
                  pl.BlockSpec((1, mlen, w2), lambda b, i: (b, 0, 0)),
                  vec, vec],
        out_specs=pl.BlockSpec((1, tt, BRANCH_W), lambda b, i: (b, i, 0)),
        out_shape=jax.ShapeDtypeStruct((bsz, t, BRANCH_W), BF16),
        compiler_params=_cparams("parallel", "parallel"),
        name="memory_attention",
    )(h_m, kv, q_norm.reshape(1, MEM_HEAD), k_norm.reshape(1, MEM_HEAD))


def _pad_lora(up):
    z = jnp.zeros_like(up[0])
    w = jnp.stack([jnp.concatenate([up[0], z], axis=0), jnp.concatenate([z, up[1]], axis=0)])
    hi = w.astype(BF16)
    return jnp.stack([hi, (w - hi.astype(F32)).astype(BF16)])


def _layer(x2d, mem2d, bsz, layer, p, big):
    m, d = x2d.shape
    t = m // bsz
    bw = BRANCH_W
    a_w = A_SHIFT_W + bw
    o1, o2, o3, o4 = a_w, a_w + 3 * bw, a_w + 7 * bw, a_w + 9 * bw
    proj = functools.partial(_matmul, _rmsnorm(x2d, p["norm_g"], BF16), big["w_in"], layer)

    h_rkv = proj(0, 3 * bw, BF16, 1024, 1024, name="proj_a_rkv")
    h_lora = proj(3 * bw, 4 * LORA, BF16, 1024, 4 * LORA, name="proj_a_lora")
    h_ag = proj(A_SHIFT_W, bw, BF16, 1024, 1024, name="proj_a_gate")
    h_b = proj(o1, 3 * bw, BF16, 1024, 1024, name="proj_b")
    h_c = proj(o2, 4 * bw, BF16, 1024, 1024, name="proj_c")
    h_m = proj(o3, 2 * bw, BF16, 1024, 1024, name="proj_m")
    gates = proj(o4, 4 * d, BF16, 1024, 1024, act="sigmoid", name="proj_gates")

    prep = _rwkv_prep(h_rkv.reshape(bsz, t, 3 * bw), h_lora.reshape(bsz, t, 4 * LORA), p["a_conv"],
                      _pad_lora(p["a_w_up"]), p["a_w0"], _pad_lora(p["a_a_up"]), p["a_a0"],
                      p["a_k_k"], p["a_k_a"], p["a_r_k"].reshape(-1))
    r, v, kk, lw0, lw1, kt0, kt1, b0, b1, bonus = prep
    yf, yb = _wkv(r, v, kk, lw0, lw1, kt0, kt1, b0, b1)
    y_a = _rwkv_post(yf, yb, bonus, h_ag.reshape(bsz, t, bw), p["a_lnx_w"], p["a_lnx_b"])

    bs_cols = jnp.repeat(p["b_b_s"].T, SG_CHUNK, axis=1)
    y_b = _sgu(h_b.reshape(bsz, t, 3 * bw), p["b_ln_g"], p["b_ln_b"], p["b_w_s"].astype(BF16), bs_cols)

    y_c = _natten(h_c.reshape(bsz, t, 4 * bw), p["c_q_norm"], p["c_k_norm"], _natten_bias_table(p["c_rpb"]))

    mem_n = _rmsnorm(mem2d, p["m_norm_g"], BF16)
    kv = _matmul(mem_n, big["m_w_kv"], layer, 0, 2 * bw, BF16, 1024, 1024, name="proj_mem_kv")
    y_d = _memattn(h_m.reshape(bsz, t, 2 * bw), kv.reshape(bsz, -1, 2 * bw), p["m_q_norm"], p["m_k_norm"])

    ys = [y.reshape(m, bw) for y in (y_a, y_b, y_c, y_d)]
    merged = _merge(ys, big["w_branch"], layer, gates, 512, 1024)
    return _matmul_residual(merged, big["w_out"], layer, x2d, 512, 1024)


def kernel(x, mem, norm_g, w_in, a_conv, a_w_up, a_w0, a_a_up, a_a0, a_k_k, a_k_a, a_r_k, a_lnx_w, a_lnx_b,
           b_ln_g, b_ln_b, b_w_s, b_b_s, c_q_norm, c_k_norm, c_rpb, m_norm_g, m_w_kv, m_q_norm, m_k_norm,
           w_branch, w_out):
    params = dict(norm_g=norm_g, a_conv=a_conv, a_w_up=a_w_up, a_w0=a_w0, a_a_up=a_a_up, a_a0=a_a0,
                  a_k_k=a_k_k, a_k_a=a_k_a, a_r_k=a_r_k, a_lnx_w=a_lnx_w, a_lnx_b=a_lnx_b, b_ln_g=b_ln_g,
                  b_ln_b=b_ln_b, b_w_s=b_w_s, b_b_s=b_b_s, c_q_norm=c_q_norm, c_k_norm=c_k_norm, c_rpb=c_rpb,
                  m_norm_g=m_norm_g, m_q_norm=m_q_norm, m_k_norm=m_k_norm)
    big = dict(w_in=w_in.astype(BF16), m_w_kv=m_w_kv.astype(BF16), w_branch=w_branch.astype(BF16),
               w_out=w_out.astype(BF16))
    bsz, t, d = x.shape
    x2d = x.reshape(bsz * t, d)
    mem2d = mem.reshape(-1, d)
    for l in range(norm_g.shape[0]):
        x2d = _layer(x2d, mem2d, bsz, l, {k: v[l] for k, v in params.items()}, big)
    return x2d.reshape(bsz, t, d)
```

```python
import functools

import numpy as np
import jax
import jax.numpy as jnp
from jax import lax
from jax.experimental import pallas as pl
from jax.experimental.pallas import tpu as pltpu

F32 = jnp.float32
BF16 = jnp.bfloat16

VMEM_LIMIT_BYTES = 56 * 1024 * 1024
LANES = 128
BF16_SUBLANES = 16

BRANCH_W = 1024
RWKV_HEAD = 64
LORA = 64
A_SHIFT_W = 3 * BRANCH_W + 4 * LORA
LNX_EPS = 64e-5
SG_CHUNK = 128
SG_GROUPS = 8
NA_HEAD = 64
NA_ROWS = 8
NA_COLS = 16
GRID_W = 64
MEM_HEADS = 4
MEM_HEAD = 256
WKV_CHUNK = 64
NEG_INF = -1e30


def _cparams(*sem):
    return pltpu.CompilerParams(dimension_semantics=sem, vmem_limit_bytes=VMEM_LIMIT_BYTES)


def _sigmoid(x):
    return 0.5 * jnp.tanh(0.5 * x) + 0.5


def _silu(x):
    return x * _sigmoid(x)


def _gelu_tanh(x):
    c = np.float32(np.sqrt(2.0 / np.pi))
    return 0.5 * x * (1.0 + jnp.tanh(c * (x + 0.044715 * (x * x * x))))


def _dot(a, b, precision=None):
    return jnp.dot(a, b, preferred_element_type=F32, precision=precision)


def _dot_nt(a, b, precision=None):
    return lax.dot_general(a, b, (((1,), (1,)), ((), ())), preferred_element_type=F32,
                           precision=precision)


def _dot_tn(a, b, precision=None):
    return lax.dot_general(a, b, (((0,), (0,)), ((), ())), preferred_element_type=F32,
                           precision=precision)


def _split_bf16(x, terms):
    parts = []
    for _ in range(terms - 1):
        hi = x.astype(BF16)
        parts.append(hi)
        x = x - hi.astype(F32)
    parts.append(x.astype(BF16))
    return parts


def _dot_exact_rhs(x, w_bf16, terms):
    acc = None
    for piece in _split_bf16(x, terms):
        d = _dot(piece, w_bf16)
        acc = d if acc is None else acc + d
    return acc


def _head_block_ones(head):
    shift = int(np.log2(head))
    r = lax.broadcasted_iota(jnp.int32, (LANES, LANES), 0) >> shift
    c = lax.broadcasted_iota(jnp.int32, (LANES, LANES), 1) >> shift
    return jnp.where(r == c, 1.0, 0.0).astype(BF16)


def _head_sum(x, ones_bd, terms=1):
    parts = [_dot_exact_rhs(x[:, j:j + LANES], ones_bd, terms) for j in range(0, x.shape[1], LANES)]
    return parts[0] if len(parts) == 1 else jnp.concatenate(parts, axis=1)


def _rmsnorm_kernel(x_ref, g_ref, o_ref, *, eps):
    x = x_ref[...].astype(F32)
    ms = jnp.mean(x * x, axis=-1, keepdims=True)
    o_ref[...] = (x * lax.rsqrt(ms + eps) * g_ref[...]).astype(o_ref.dtype)


def _rmsnorm(x2d, g, out_dtype, tm=256, eps=1e-6):
    m, d = x2d.shape
    tm = min(tm, m)
    return pl.pallas_call(
        functools.partial(_rmsnorm_kernel, eps=eps),
        grid=(m // tm,),
        in_specs=[pl.BlockSpec((tm, d), lambda i: (i, 0)), pl.BlockSpec((1, d), lambda i: (0, 0))],
        out_specs=pl.BlockSpec((tm, d), lambda i: (i, 0)),
        out_shape=jax.ShapeDtypeStruct((m, d), out_dtype),
        compiler_params=_cparams("parallel"),
        name="rmsnorm",
    )(x2d, g.reshape(1, d))


def _mm_kernel(a_ref, b_ref, o_ref, *, act):
    acc = _dot(a_ref[...], b_ref[0])
    if act == "sigmoid":
        acc = _sigmoid(acc)
    o_ref[...] = acc.astype(o_ref.dtype)


def _matmul(a, w, layer, col0, n, out_dtype, tm, tn, act=None, name="matmul"):
    m, k = a.shape
    tm, tn = min(tm, m), min(tn, n)
    return pl.pallas_call(
        functools.partial(_mm_kernel, act=act),
        grid=(m // tm, n // tn),
        in_specs=[pl.BlockSpec((tm, k), lambda i, j: (i, 0)),
                  pl.BlockSpec((pl.Element(1), pl.Element(k), pl.Element(tn)),
                               lambda i, j: (layer, 0, pl.multiple_of(col0 + j * tn, LANES)))],
        out_specs=pl.BlockSpec((tm, tn), lambda i, j: (i, j)),
        out_shape=jax.ShapeDtypeStruct((m, n), out_dtype),
        compiler_params=_cparams("parallel", "parallel"),
        name=name,
    )(a, w)


def _mm_residual_kernel(a_ref, b_ref, x_ref, o_ref):
    o_ref[...] = x_ref[...] + _dot(a_ref[...], b_ref[...])


def _matmul_residual(a, w, layer, x, tm, tn):
    m, k = a.shape
    n = w.shape[2]
    tm, tn = min(tm, m), min(tn, n)
    return pl.pallas_call(
        _mm_residual_kernel,
        grid=(n // tn, m // tm),
        in_specs=[pl.BlockSpec((tm, k), lambda j, i: (i, 0)),
                  pl.BlockSpec((None, k, tn), lambda j, i: (layer, 0, j)),
                  pl.BlockSpec((tm, tn), lambda j, i: (i, j))],
        out_specs=pl.BlockSpec((tm, tn), lambda j, i: (i, j)),
        out_shape=jax.ShapeDtypeStruct((m, n), x.dtype),
        compiler_params=_cparams("parallel", "parallel"),
        name="out_proj_residual",
    )(a, w, x)


def _merge_kernel(y0, y1, y2, y3, wb_ref, g0, g1, g2, g3, o_ref):
    acc = None
    for n, (y, g) in enumerate(((y0, g0), (y1, g1), (y2, g2), (y3, g3))):
        term = g[...].astype(F32) * _dot(y[...], wb_ref[n])
        acc = term if acc is None else acc + term
    o_ref[...] = acc.astype(o_ref.dtype)


def _merge(ys, wb, layer, gates, tm, tn):
    m, bw = ys[0].shape
    d = wb.shape[3]
    tm, tn = min(tm, m), min(tn, d)
    nj = d // tn
    y_spec = pl.BlockSpec((tm, bw), lambda j, i: (i, 0))
    g_specs = [pl.BlockSpec((tm, tn), functools.partial(lambda j, i, n: (i, n * nj + j), n=n))
               for n in range(4)]
    return pl.pallas_call(
        _merge_kernel,
        grid=(nj, m // tm),
        in_specs=[y_spec] * 4 + [pl.BlockSpec((None, 4, bw, tn), lambda j, i: (layer, 0, 0, j))] + g_specs,
        out_specs=pl.BlockSpec((tm, tn), lambda j, i: (i, j)),
        out_shape=jax.ShapeDtypeStruct((m, d), BF16),
        compiler_params=_cparams("parallel", "parallel"),
        name="gated_merge",
    )(*ys, wb, gates, gates, gates, gates)


def _token_shift(h_ref, hp_ref, hn_ref, conv_ref):
    i = pl.program_id(1)
    nt = pl.num_programs(1)
    h = h_ref[0].astype(F32)
    tt = h.shape[0]
    prev_row = hp_ref[0, BF16_SUBLANES - 1:BF16_SUBLANES, :].astype(F32) * (i > 0).astype(F32)
    next_row = hn_ref[0, 0:1, :].astype(F32) * (i < nt - 1).astype(F32)
    rows = lax.broadcasted_iota(jnp.int32, (tt, 1), 0)
    h_dn = jnp.where(rows == 0, prev_row, pltpu.roll(h, 1, 0))
    h_up = jnp.where(rows == tt - 1, next_row, pltpu.roll(h, tt - 1, 0))
    return h_dn * conv_ref[0:1, :] + h * conv_ref[1:2, :] + h_up * conv_ref[2:3, :]


def _rwkv_prep_kernel(h_ref, hp_ref, hn_ref, l_ref, lp_ref, ln_ref, conv_ref, convl_ref, wup_ref, w0_ref,
                      aup_ref, a0_ref, kk_ref, ka_ref, rk_ref,
                      r_o, v_o, kk_o, lw0_o, lw1_o, kt0_o, kt1_o, b0_o, b1_o, bonus_o):
    hs = _token_shift(h_ref, hp_ref, hn_ref, conv_ref)
    ls = _token_shift(l_ref, lp_ref, ln_ref, convl_ref)
    bw = BRANCH_W
    r = hs[:, 0:bw]
    k = hs[:, bw:2 * bw]
    v = hs[:, 2 * bw:3 * bw]
    wd = jnp.tanh(ls[:, 0:2 * LORA])
    ad = ls[:, 2 * LORA:4 * LORA]

    ones_bd = _head_block_ones(RWKV_HEAD)
    kkr = k * kk_ref[...]
    nrm = jnp.sqrt(_head_sum(kkr * kkr, ones_bd))
    kk = kkr / jnp.maximum(nrm, 1e-12)

    def up_proj(x_pieces, w_ref, z):
        (x_hi, x_lo), w_hi, w_lo = x_pieces, w_ref[0, z], w_ref[1, z]
        return _dot(x_hi, w_hi) + (_dot(x_lo, w_hi) + _dot(x_hi, w_lo))

    wd_pieces, ad_pieces = _split_bf16(wd, 2), _split_bf16(ad, 2)
    decay_scale = np.float32(np.exp(-0.5))
    kts = []
    for z, (lw_o, kt_o, b_o) in enumerate(((lw0_o, kt0_o, b0_o), (lw1_o, kt1_o, b1_o))):
        w_raw = w0_ref[z:z + 1, :] + up_proj(wd_pieces, wup_ref, z)
        lw_o[0] = -decay_scale * _sigmoid(w_raw)
        a = _sigmoid(a0_ref[z:z + 1, :] + up_proj(ad_pieces, aup_ref, z))
        kt = k * (1.0 + (a - 1.0) * ka_ref[...])
        kt_o[0] = kt
        b_o[0] = kk * a
        kts.append(kt)
    r_o[0] = r
    v_o[0] = v
    kk_o[0] = kk
    bonus_o[0] = _head_sum(r * (kts[0] + kts[1]) * rk_ref[...], ones_bd) * v


def _rwkv_prep(h_rkv, h_lora, conv, wup_pad, w0, aup_pad, a0, k_k, k_a, r_k, tt=256):
    bsz, t, w = h_rkv.shape
    wl = h_lora.shape[2]
    tt = min(tt, t)
    nt = t // tt
    hb = tt // BF16_SUBLANES
    n_halo = t // BF16_SUBLANES
    row = lambda a: a.reshape(1, -1)
    vec_spec = pl.BlockSpec((1, BRANCH_W), lambda b, i: (0, 0))
    out_spec = pl.BlockSpec((1, tt, BRANCH_W), lambda b, i: (b, i, 0))
    out_sds = jax.ShapeDtypeStruct((bsz, t, BRANCH_W), F32)
    tile = lambda width: [
        pl.BlockSpec((1, tt, width), lambda b, i: (b, i, 0)),
        pl.BlockSpec((1, BF16_SUBLANES, width), lambda b, i: (b, jnp.maximum(i * hb - 1, 0), 0)),
        pl.BlockSpec((1, BF16_SUBLANES, width), lambda b, i: (b, jnp.minimum((i + 1) * hb, n_halo - 1), 0))]
    return pl.pallas_call(
        _rwkv_prep_kernel,
        grid=(bsz, nt),
        in_specs=tile(w) + tile(wl) + [
            pl.BlockSpec((3, w), lambda b, i: (0, 0)),
            pl.BlockSpec((3, wl), lambda b, i: (0, 0)),
            pl.BlockSpec((2, 2, 2 * LORA, BRANCH_W), lambda b, i: (0, 0, 0, 0)),
            pl.BlockSpec((2, BRANCH_W), lambda b, i: (0, 0)),
            pl.BlockSpec((2, 2, 2 * LORA, BRANCH_W), lambda b, i: (0, 0, 0, 0)),
            pl.BlockSpec((2, BRANCH_W), lambda b, i: (0, 0)),
            vec_spec, vec_spec, vec_spec,
        ],
        out_specs=[out_spec] * 10,
        out_shape=[out_sds] * 10,
        compiler_params=_cparams("parallel", "parallel"),
        name="rwkv_prep",
    )(h_rkv, h_rkv, h_rkv, h_lora, h_lora, h_lora, conv[:, :w], conv[:, w:], wup_pad, w0, aup_pad, a0,
      row(k_k), row(k_a), row(r_k))


def _wkv_masks():
    c = WKV_CHUNK
    row = lax.broadcasted_iota(jnp.int32, (2 * c, 2 * c), 0)
    col = lax.broadcasted_iota(jnp.int32, (2 * c, 2 * c), 1)
    same = (row >> 6) == (col >> 6)
    rt, ct = row & (c - 1), col & (c - 1)
    f = lambda m: jnp.where(same & m, 1.0, 0.0).astype(F32)
    lane = lax.broadcasted_iota(jnp.int32, (1, LANES), 1)
    crow = lax.broadcasted_iota(jnp.int32, (c, c), 0)
    ccol = lax.broadcasted_iota(jnp.int32, (c, c), 1)
    return {
        "eye": f(rt == ct),
        "strict": (f(ct < rt), f(ct > rt)),
        "incl": (f(ct <= rt), f(ct >= rt)),
        "cum": (jnp.where(ccol <= crow, 1.0, 0.0).astype(BF16), jnp.where(ccol >= crow, 1.0, 0.0).astype(BF16)),
        "head0": jnp.where(lane < RWKV_HEAD, 1.0, 0.0).astype(F32),
        "head1": jnp.where(lane < RWKV_HEAD, 0.0, 1.0).astype(F32),
    }


def _wkv_local(problems, masks, fillers=()):
    c = WKV_CHUNK
    bf = lambda x: x.astype(BF16)
    m0, m1 = masks["head0"], masks["head1"]
    pair = lambda x: jnp.concatenate([x * m0, x * m1], axis=0)
    each = lambda fn, *lists: [fn(*xs) for xs in zip(*lists)]
    dirs = [p[6] for p in problems]
    fillers = list(fillers)
    n_points = 16
    stride = max(1, n_points // max(1, len(fillers)))
    seen = [0]

    def fill():
        seen[0] += 1
        if fillers and seen[0] % stride == 0:
            fillers.pop(0)()

    cl = [_cumsum_dot(masks["cum"][p[6]], p[0]) for p in problems]
    tot = [x[c - 1:c, :] if d == 0 else x[0:1, :] for x, d in zip(cl, dirs)]
    fill()
    zp = [pair(-p[4] * jnp.exp(x - p[0])) for p, x in zip(problems, cl)]
    rp = [pair(p[1] * jnp.exp(x)) for p, x in zip(problems, cl)]
    vpb = [bf(pair(p[3])) for p in problems]
    e_neg = [jnp.exp(-x) for x in cl]
    bk_start = [bf(jnp.concatenate([pair(p[5] * e), pair(p[2] * e)], axis=0)) for p, e in zip(problems, e_neg)]
    e_end = [jnp.exp(t - x) for t, x in zip(tot, cl)]
    bk_end = [bf(jnp.concatenate([pair(p[5] * e), pair(p[2] * e)], axis=0)) for p, e in zip(problems, e_end)]
    fill()

    scores = each(lambda z, r, bk: _dot_nt(bf(jnp.concatenate([z, r], axis=0)), bk), zp, rp, bk_start)
    fill()
    strict = [masks["strict"][d] for d in dirs]
    incl = [masks["incl"][d] for d in dirs]
    l_zb = each(lambda s, m: s[0:2 * c, 0:2 * c] * m, scores, strict)
    a_zk = each(lambda s, m: bf(s[0:2 * c, 2 * c:4 * c] * m), scores, strict)
    a_r = each(lambda s, m: bf(jnp.concatenate([s[2 * c:4 * c, 0:2 * c] * m, s[2 * c:4 * c, 2 * c:4 * c] * m],
                                               axis=1)), scores, incl)
    azk_v = each(_dot, a_zk, vpb)
    fill()

    inv = [masks["eye"] + l for l in l_zb]
    pb = [bf(l) for l in l_zb]
    for _ in range(5):
        pb = [bf(_dot(x, x)) for x in pb]
        fill()
        inv = each(lambda t, x: t + _dot(bf(t), x), inv, pb)
        fill()

    zu = each(lambda t, z, u: _dot(bf(t), bf(jnp.concatenate([z, u], axis=1))), inv, zp, azk_v)
    fill()
    stack = each(lambda x, v: jnp.concatenate([bf(x), jnp.concatenate([jnp.zeros_like(v), v], axis=1)], axis=0),
                 zu, vpb)
    ry = each(_dot, a_r, stack)
    fill()
    gh = each(_dot_tn, stack, bk_end)
    while fillers:
        fillers.pop(0)()
    out = []
    for r, y, g, t in zip(rp, ry, gh, tot):
        rb = r + y[:, 0:2 * c]
        out.append((rb[0:c] + rb[c:2 * c], y[0:c, 2 * c:4 * c] + y[c:2 * c, 2 * c:4 * c],
                    g[0:2 * c], g[2 * c:4 * c], jnp.exp(t)))
    return out


def _cumsum_dot(cum_bf16, lw):
    acc = None
    for piece in _split_bf16(lw, 2):
        d = _dot(cum_bf16, piece)
        acc = d if acc is None else acc + d
    return acc


def _wkv_kernel(r_f, v_f, kk_f, lw_f, kt_f, b_f, r_b, v_b, kk_b, lw_b, kt_b, b_b, yf_o, yb_o,
                s_ref, rb_ref, g_ref, h_ref, wc_ref):
    @pl.when(pl.program_id(2) == 0)
    def _():
        s_ref[...] = jnp.zeros_like(s_ref)

    c = WKV_CHUNK
    n_chunks = r_f.shape[1] // c
    half = (5 * n_chunks) // 8
    masks = _wkv_masks()
    ins = ((lw_f, r_f, kt_f, v_f, kk_f, b_f), (lw_b, r_b, kt_b, v_b, kk_b, b_b))
    outs = (yf_o, yb_o)
    streams = [(d, p) for d in range(2) for p in range(WKV_PAIRS)]
    states = [s_ref[q] for q in range(len(streams))]
    chunk_of = lambda d, step: step if d == 0 else n_chunks - 1 - step
    lanes = lambda p: slice(p * LANES, (p + 1) * LANES)

    def local_factors(steps, fillers):
        where = [(q, chunk_of(streams[q][0], s)) for s in steps for q in range(len(streams))]
        problems = [[ref[0, ci * c:(ci + 1) * c, lanes(streams[q][1])] for ref in ins[streams[q][0]]]
                    + [streams[q][0]] for q, ci in where]
        for (q, ci), (rb, yloc, g, h, wc) in zip(where, _wkv_local(problems, masks, fillers)):
            d, p = streams[q]
            rb_ref[q, ci * c:(ci + 1) * c, :] = rb.astype(BF16)
            outs[d][0, ci * c:(ci + 1) * c, lanes(p)] = yloc
            g_ref[q, ci] = g.astype(BF16)
            h_ref[q, ci] = h
            wc_ref[q, ci] = jnp.broadcast_to(wc, (8, LANES))

    def state_step(step):
        for q, (d, p) in enumerate(streams):
            ci = chunk_of(d, step)
            rows = slice(ci * c, (ci + 1) * c)
            sb = states[q].astype(BF16)
            outs[d][0, rows, lanes(p)] = outs[d][0, rows, lanes(p)] + _dot_nt(rb_ref[q, rows, :], sb)
            states[q] = states[q] * wc_ref[q, ci, 0:1, :] + _dot(sb, g_ref[q, ci]) + h_ref[q, ci]

    local_factors(range(0, half), ())
    local_factors(range(half, n_chunks), [functools.partial(state_step, s) for s in range(half)])
    for s in range(half, n_chunks):
        state_step(s)
    for q in range(len(streams)):
        s_ref[q] = states[q]


WKV_PAIRS = 2


def _wkv(r, v, kk, lw0, lw1, kt0, kt1, b0, b1, tb=512):
    bsz, t, w = r.shape
    tb = min(tb, t)
    nb = t // tb
    nc = tb // WKV_CHUNK
    wl = WKV_PAIRS * LANES
    ns = 2 * WKV_PAIRS
    fwd = pl.BlockSpec((1, tb, wl), lambda b, h, g: (b, g, h))
    bwd = pl.BlockSpec((1, tb, wl), lambda b, h, g: (b, nb - 1 - g, h))
    sds = jax.ShapeDtypeStruct((bsz, t, w), F32)
    return pl.pallas_call(
        _wkv_kernel,
        grid=(bsz, w // wl, nb),
        in_specs=[fwd] * 6 + [bwd] * 6,
        out_specs=[fwd, bwd],
        out_shape=[sds, sds],
        scratch_shapes=[pltpu.VMEM((ns, LANES, LANES), F32), pltpu.VMEM((ns, tb, LANES), BF16),
                        pltpu.VMEM((ns, nc, LANES, LANES), BF16), pltpu.VMEM((ns, nc, LANES, LANES), F32),
                        pltpu.VMEM((ns, nc, 8, LANES), F32)],
        compiler_params=_cparams("parallel", "parallel", "arbitrary"),
        name="wkv7_chunked",
    )(r, v, kk, lw0, kt0, b0, r, v, kk, lw1, kt1, b1)


def _rwkv_post_kernel(yf_ref, yb_ref, bonus_ref, g_ref, lw_ref, lb_ref, o_ref):
    ones_bd = _head_block_ones(RWKV_HEAD)
    wkv = yf_ref[0] + yb_ref[0]
    inv_n = 1.0 / RWKV_HEAD
    mu = _head_sum(wkv, ones_bd) * inv_n
    d = wkv - mu
    var = _head_sum(d * d, ones_bd) * inv_n
    gn = d * lax.rsqrt(var + LNX_EPS) * lw_ref[...] + lb_ref[...]
    o_ref[0] = ((gn + bonus_ref[0]) * _silu(g_ref[0].astype(F32))).astype(o_ref.dtype)


def _rwkv_post(yf, yb, bonus, g, lnx_w, lnx_b, tt=512):
    bsz, t, w = yf.shape
    tt = min(tt, t)
    spec = pl.BlockSpec((1, tt, w), lambda b, i: (b, i, 0))
    vec = pl.BlockSpec((1, w), lambda b, i: (0, 0))
    return pl.pallas_call(
        _rwkv_post_kernel,
        grid=(bsz, t // tt),
        in_specs=[spec, spec, spec, spec, vec, vec],
        out_specs=spec,
        out_shape=jax.ShapeDtypeStruct((bsz, t, w), BF16),
        compiler_params=_cparams("parallel", "parallel"),
        name="rwkv_post",
    )(yf, yb, bonus, g, lnx_w.reshape(1, w), lnx_b.reshape(1, w))


def _sgu_kernel(h_ref, lg_ref, lb_ref, ws_ref, bs_ref, o_ref):
    bw = BRANCH_W
    tt = h_ref.shape[1]
    u = _gelu_tanh(h_ref[0, :, 0:bw].astype(F32))
    vv = _gelu_tanh(h_ref[0, :, bw:2 * bw].astype(F32))
    g = h_ref[0, :, 2 * bw:3 * bw].astype(F32)
    mu = jnp.mean(vv, axis=-1, keepdims=True)
    d = vv - mu
    var = jnp.mean(d * d, axis=-1, keepdims=True)
    vn = (d * lax.rsqrt(var + 1e-5) * lg_ref[...] + lb_ref[...]).astype(BF16)
    gate = u * _silu(g)
    for ck in range(tt // SG_CHUNK):
        rs = slice(ck * SG_CHUNK, (ck + 1) * SG_CHUNK)
        for grp in range(SG_GROUPS):
            cs = slice(grp * LANES, (grp + 1) * LANES)
            sv = _dot(ws_ref[grp], vn[rs, cs]) + bs_ref[:, cs]
            o_ref[0, rs, cs] = (gate[rs, cs] * sv).astype(o_ref.dtype)


def _sgu(h_b, ln_g, ln_b, w_s, bs_cols, tt=256):
    bsz, t, w3 = h_b.shape
    bw = BRANCH_W
    tt = min(tt, t)
    vec = pl.BlockSpec((1, bw), lambda b, i: (0, 0))
    return pl.pallas_call(
        _sgu_kernel,
        grid=(bsz, t // tt),
        in_specs=[pl.BlockSpec((1, tt, w3), lambda b, i: (b, i, 0)), vec, vec,
                  pl.BlockSpec((SG_GROUPS, SG_CHUNK, SG_CHUNK), lambda b, i: (0, 0, 0)),
                  pl.BlockSpec((SG_CHUNK, bw), lambda b, i: (0, 0))],
        out_specs=pl.BlockSpec((1, tt, bw), lambda b, i: (b, i, 0)),
        out_shape=jax.ShapeDtypeStruct((bsz, t, bw), BF16),
        compiler_params=_cparams("parallel", "parallel"),
        name="spatial_gating",
    )(h_b, ln_g.reshape(1, bw), ln_b.reshape(1, bw), w_s, bs_cols)


def _natten_bias_table(rpb):
    p = np.arange(GRID_W)[:, None]
    m = np.arange(GRID_W)[None, :]
    sj = np.clip(p - NA_COLS // 2, 0, GRID_W - NA_COLS)
    valid = (m >= sj) & (m < sj + NA_COLS)
    dc = np.clip(m - p, -(NA_COLS - 1), NA_COLS - 1) + NA_COLS - 1
    by_rel = jnp.where(valid[None, None], rpb[:, :, dc], NEG_INF)
    tab = jnp.stack([by_rel[:, s:s + NA_ROWS] for s in range(NA_ROWS)], axis=1)
    tab = tab.transpose(0, 1, 3, 2, 4)
    return tab.reshape(rpb.shape[0], NA_ROWS, GRID_W, NA_ROWS * GRID_W).astype(F32)


NA_UNROLL = 8


def _natten_kernel(q_ref, k_ref, v_ref, g_ref, qn_ref, kn_ref, bias_ref, o_ref, qs_ref, ks_ref):
    t = q_ref.shape[1]
    n_rows = t // GRID_W
    win = NA_ROWS * GRID_W
    ones_bd = _head_block_ones(NA_HEAD)
    lane = lax.broadcasted_iota(jnp.int32, (1, LANES), 1)
    m0 = lane < NA_HEAD

    def norm(x_ref, gain_ref, scale):
        x = x_ref[0].astype(F32)
        ms = _head_sum(x * x, ones_bd) * (1.0 / NA_HEAD)
        return x * lax.rsqrt(ms + 1e-6) * (gain_ref[...] * scale)

    qs_ref[...] = norm(q_ref, qn_ref, NA_HEAD ** -0.5).astype(BF16)
    ks_ref[...] = norm(k_ref, kn_ref, 1.0).astype(BF16)

    def body(it, carry):
        rows = [it * NA_UNROLL + j for j in range(NA_UNROLL)]
        si = [jnp.clip(i - NA_ROWS // 2, 0, n_rows - NA_ROWS) for i in rows]
        start = [s - i + (NA_ROWS - 1) for s, i in zip(si, rows)]
        qo = [pl.multiple_of(i * GRID_W, GRID_W) for i in rows]
        ko = [pl.multiple_of(s * GRID_W, GRID_W) for s in si]
        q = [qs_ref[pl.ds(o, GRID_W), :] for o in qo]
        zero = jnp.zeros_like(q[0])
        q2 = [jnp.concatenate([jnp.where(m0, x, zero), jnp.where(m0, zero, x)], axis=0) for x in q]
        s = [_dot_nt(x, ks_ref[pl.ds(o, win), :]) for x, o in zip(q2, ko)]
        s = [x + jnp.concatenate([bias_ref[0, st], bias_ref[1, st]], axis=0) for x, st in zip(s, start)]
        e = [jnp.exp(x - jnp.max(x, axis=-1, keepdims=True)).astype(BF16) for x in s]
        o2 = [_dot(x, jnp.concatenate([v_ref[0, pl.ds(o, win), :], ones_blk], axis=1))
              for x, o in zip(e, ko)]
        for x, o in zip(o2, qo):
            x = x[:, 0:LANES] / x[:, LANES:2 * LANES]
            g = g_ref[0, pl.ds(o, GRID_W), :].astype(F32)
            val = jnp.where(m0, x[0:GRID_W], x[GRID_W:2 * GRID_W]) * _silu(g)
            o_ref[0, pl.ds(o, GRID_W), :] = val.astype(o_ref.dtype)
        return carry

    ones_blk = jnp.ones((win, LANES), BF16)

    lax.fori_loop(0, n_rows // NA_UNROLL, body, 0)


def _natten(h_c, q_norm, k_norm, bias_tab):
    bsz, t, w4 = h_c.shape
    bw = BRANCH_W
    nlb = bw // LANES
    sec = lambda s: pl.BlockSpec((1, t, LANES), functools.partial(lambda b, hp, s: (b, 0, s * nlb + hp), s=s))
    two = lambda a: jnp.concatenate([a, a]).reshape(1, LANES)
    return pl.pallas_call(
        _natten_kernel,
        grid=(bsz, nlb),
        in_specs=[sec(0), sec(1), sec(2), sec(3),
                  pl.BlockSpec((1, LANES), lambda b, hp: (0, 0)), pl.BlockSpec((1, LANES), lambda b, hp: (0, 0)),
                  pl.BlockSpec((2, NA_ROWS, GRID_W, NA_ROWS * GRID_W), lambda b, hp: (hp, 0, 0, 0))],
        out_specs=pl.BlockSpec((1, t, LANES), lambda b, hp: (b, 0, hp)),
        out_shape=jax.ShapeDtypeStruct((bsz, t, bw), BF16),
        scratch_shapes=[pltpu.VMEM((t, LANES), BF16), pltpu.VMEM((t, LANES), BF16)],
        compiler_params=_cparams("parallel", "parallel"),
        name="neighbourhood_attention",
    )(h_c, h_c, h_c, h_c, two(q_norm), two(k_norm), bias_tab)


def _memattn_kernel(qg_ref, kv_ref, qn_ref, kn_ref, o_ref):
    hd, bw = MEM_HEAD, BRANCH_W

    def norm(x, gain, scale):
        ms = jnp.mean(x * x, axis=-1, keepdims=True)
        return (x * lax.rsqrt(ms + 1e-6) * (gain * scale)).astype(BF16)

    heads = [slice(h * hd, (h + 1) * hd) for h in range(MEM_HEADS)]
    q = [norm(qg_ref[0, :, c].astype(F32), qn_ref[...], hd ** -0.5) for c in heads]
    k = [norm(kv_ref[0, :, c].astype(F32), kn_ref[...], 1.0) for c in heads]
    s = [_dot_nt(a, b) for a, b in zip(q, k)]
    e = [jnp.exp(x - jnp.max(x, axis=-1, keepdims=True)) for x in s]
    p = [(x / jnp.sum(x, axis=-1, keepdims=True)).astype(BF16) for x in e]
    o = [_dot(x, kv_ref[0, :, bw + h * hd:bw + (h + 1) * hd]) for h, x in enumerate(p)]
    for c, x in zip(heads, o):
        g = qg_ref[0, :, bw + c.start:bw + c.stop].astype(F32)
        o_ref[0, :, c] = (x * _silu(g)).astype(o_ref.dtype)


def _memattn(h_m, kv, q_norm, k_norm, tt=512):
    bsz, t, w2 = h_m.shape
    mlen = kv.shape[1]
    tt = min(tt, t)
    vec = pl.BlockSpec((1, MEM_HEAD), lambda b, i: (0, 0))
    return pl.pallas_call(
        _memattn_kernel,
        grid=(bsz, t // tt),
        in_specs=[pl.BlockSpec((1, tt, w2), lambda b, i: (b, i, 0)),
                  pl.BlockSpec((1, mlen, w2), lambda b, i: (b, 0, 0)),
                  vec, vec],
        out_specs=pl.BlockSpec((1, tt, BRANCH_W), lambda b, i: (b, i, 0)),
        out_shape=jax.ShapeDtypeStruct((bsz, t, BRANCH_W), BF16),
        compiler_params=_cparams("parallel", "parallel"),
        name="memory_attention",
    )(h_m, kv, q_norm.reshape(1, MEM_HEAD), k_norm.reshape(1, MEM_HEAD))


def _pad_lora(up):
    z = jnp.zeros_like(up[0])
    w = jnp.stack([jnp.concatenate([up[0], z], axis=0), jnp.concatenate([z, up[1]], axis=0)])
    hi = w.astype(BF16)
    return jnp.stack([hi, (w - hi.astype(F32)).astype(BF16)])


def _layer(x2d, mem2d, bsz, layer, p, big):
    m, d = x2d.shape
    t = m // bsz
    bw = BRANCH_W
    a_w = A_SHIFT_W + bw
    o1, o2, o3, o4 = a_w, a_w + 3 * bw, a_w + 7 * bw, a_w + 9 * bw
    proj = functools.partial(_matmul, _rmsnorm(x2d, p["norm_g"], BF16), big["w_in"], layer)

    h_rkv = proj(0, 3 * bw, BF16, 1024, 1024, name="proj_a_rkv")
    h_lora = proj(3 * bw, 4 * LORA, BF16, 1024, 4 * LORA, name="proj_a_lora")
    h_ag = proj(A_SHIFT_W, bw, BF16, 1024, 1024, name="proj_a_gate")
    h_b = proj(o1, 3 * bw, BF16, 1024, 1024, name="proj_b")
    h_c = proj(o2, 4 * bw, BF16, 1024, 1024, name="proj_c")
    h_m = proj(o3, 2 * bw, BF16, 1024, 1024, name="proj_m")
    gates = proj(o4, 4 * d, BF16, 1024, 1024, act="sigmoid", name="proj_gates")

    prep = _rwkv_prep(h_rkv.reshape(bsz, t, 3 * bw), h_lora.reshape(bsz, t, 4 * LORA), p["a_conv"],
                      _pad_lora(p["a_w_up"]), p["a_w0"], _pad_lora(p["a_a_up"]), p["a_a0"],
                      p["a_k_k"], p["a_k_a"], p["a_r_k"].reshape(-1))
    r, v, kk, lw0, lw1, kt0, kt1, b0, b1, bonus = prep
    yf, yb = _wkv(r, v, kk, lw0, lw1, kt0, kt1, b0, b1)
    y_a = _rwkv_post(yf, yb, bonus, h_ag.reshape(bsz, t, bw), p["a_lnx_w"], p["a_lnx_b"])

    bs_cols = jnp.repeat(p["b_b_s"].T, SG_CHUNK, axis=1)
    y_b = _sgu(h_b.reshape(bsz, t, 3 * bw), p["b_ln_g"], p["b_ln_b"], p["b_w_s"].astype(BF16), bs_cols)

    y_c = _natten(h_c.reshape(bsz, t, 4 * bw), p["c_q_norm"], p["c_k_norm"], _natten_bias_table(p["c_rpb"]))

    mem_n = _rmsnorm(mem2d, p["m_norm_g"], BF16)
    kv = _matmul(mem_n, big["m_w_kv"], layer, 0, 2 * bw, BF16, 1024, 1024, name="proj_mem_kv")
    y_d = _memattn(h_m.reshape(bsz, t, 2 * bw), kv.reshape(bsz, -1, 2 * bw), p["m_q_norm"], p["m_k_norm"])

    ys = [y.reshape(m, bw) for y in (y_a, y_b, y_c, y_d)]
    merged = _merge(ys, big["w_branch"], layer, gates, 512, 1024)
    return _matmul_residual(merged, big["w_out"], layer, x2d, 512, 1024)


def kernel(x, mem, norm_g, w_in, a_conv, a_w_up, a_w0, a_a_up, a_a0, a_k_k, a_k_a, a_r_k, a_lnx_w, a_lnx_b,
           b_ln_g, b_ln_b, b_w_s, b_b_s, c_q_norm, c_k_norm, c_rpb, m_norm_g, m_w_kv, m_q_norm, m_k_norm,
           w_branch, w_out):
    params = dict(norm_g=norm_g, a_conv=a_conv, a_w_up=a_w_up, a_w0=a_w0, a_a_up=a_a_up, a_a0=a_a0,
                  a_k_k=a_k_k, a_k_a=a_k_a, a_r_k=a_r_k, a_lnx_w=a_lnx_w, a_lnx_b=a_lnx_b, b_ln_g=b_ln_g,
                  b_ln_b=b_ln_b, b_w_s=b_w_s, b_b_s=b_b_s, c_q_norm=c_q_norm, c_k_norm=c_k_norm, c_rpb=c_rpb,
                  m_norm_g=m_norm_g, m_q_norm=m_q_norm, m_k_norm=m_k_norm)
    big = dict(w_in=w_in.astype(BF16), m_w_kv=m_w_kv.astype(BF16), w_branch=w_branch.astype(BF16),
               w_out=w_out.astype(BF16))
    bsz, t, d = x.shape
    x2d = x.reshape(bsz * t, d)
    mem2d = mem.reshape(-1, d)
    for l in range(norm_g.shape[0]):
        x2d = _layer(x2d, mem2d, bsz, l, {k: v[l] for k, v in params.items()}, big)
    return x2d.reshape(bsz, t, d)
```

```python
import functools

import numpy as np
import jax
import jax.numpy as jnp
from jax import lax
from jax.experimental import pallas as pl
from jax.experimental.pallas import tpu as pltpu

F32 = jnp.float32
BF16 = jnp.bfloat16

VMEM_LIMIT_BYTES = 56 * 1024 * 1024
LANES = 128
BF16_SUBLANES = 16

BRANCH_W = 1024
N_BRANCH = 4
RWKV_HEAD = 64
LORA = 64
A_SHIFT_W = 3 * BRANCH_W + 4 * LORA
LNX_EPS = 64e-5
SG_CHUNK = 128
SG_GROUPS = 8
NA_HEAD = 64
NA_ROWS = 8
NA_COLS = 16
GRID_W = 64
MEM_HEADS = 4
MEM_HEAD = 256
WKV_CHUNK = 64
NEG_INF = -1e30


def _cparams(*sem):
    return pltpu.CompilerParams(dimension_semantics=sem, vmem_limit_bytes=VMEM_LIMIT_BYTES)


def _sigmoid(x):
    return 0.5 * jnp.tanh(0.5 * x) + 0.5


def _silu(x):
    return x * _sigmoid(x)


def _gelu_tanh(x):
    c = np.float32(np.sqrt(2.0 / np.pi))
    return 0.5 * x * (1.0 + jnp.tanh(c * (x + 0.044715 * (x * x * x))))


def _dot(a, b, precision=None):
    return jnp.dot(a, b, preferred_element_type=F32, precision=precision)


def _dot_nt(a, b, precision=None):
    return lax.dot_general(a, b, (((1,), (1,)), ((), ())), preferred_element_type=F32,
                           precision=precision)


def _dot_tn(a, b, precision=None):
    return lax.dot_general(a, b, (((0,), (0,)), ((), ())), preferred_element_type=F32,
                           precision=precision)


def _split_bf16(x, terms):
    parts = []
    for _ in range(terms - 1):
        hi = x.astype(BF16)
        parts.append(hi)
        x = x - hi.astype(F32)
    parts.append(x.astype(BF16))
    return parts


def _dot_exact_rhs(x, w_bf16, terms):
    acc = None
    for piece in _split_bf16(x, terms):
        d = _dot(piece, w_bf16)
        acc = d if acc is None else acc + d
    return acc


def _head_block_ones(head):
    shift = int(np.log2(head))
    r = lax.broadcasted_iota(jnp.int32, (LANES, LANES), 0) >> shift
    c = lax.broadcasted_iota(jnp.int32, (LANES, LANES), 1) >> shift
    return jnp.where(r == c, 1.0, 0.0).astype(BF16)


def _head_sum(x, ones_bd, terms=1):
    parts = [_dot_exact_rhs(x[:, j:j + LANES], ones_bd, terms) for j in range(0, x.shape[1], LANES)]
    return parts[0] if len(parts) == 1 else jnp.concatenate(parts, axis=1)


def _rmsnorm_kernel(x_ref, g_ref, o_ref, *, eps):
    x = x_ref[...].astype(F32)
    ms = jnp.mean(x * x, axis=-1, keepdims=True)
    o_ref[...] = (x * lax.rsqrt(ms + eps) * g_ref[...]).astype(o_ref.dtype)


def _rmsnorm(x2d, g, out_dtype, tm=256, eps=1e-6):
    m, d = x2d.shape
    tm = min(tm, m)
    return pl.pallas_call(
        functools.partial(_rmsnorm_kernel, eps=eps),
        grid=(m // tm,),
        in_specs=[pl.BlockSpec((tm, d), lambda i: (i, 0)), pl.BlockSpec((1, d), lambda i: (0, 0))],
        out_specs=pl.BlockSpec((tm, d), lambda i: (i, 0)),
        out_shape=jax.ShapeDtypeStruct((m, d), out_dtype),
        compiler_params=_cparams("parallel"),
        name="rmsnorm",
    )(x2d, g.reshape(1, d))


def _mm_kernel(a_ref, b_ref, o_ref, *, act):
    acc = _dot(a_ref[...], b_ref[0])
    if act == "sigmoid":
        acc = _sigmoid(acc)
    o_ref[...] = acc.astype(o_ref.dtype)


def _matmul(a, w, layer, col0, n, out_dtype, tm, tn, act=None, name="matmul"):
    m, k = a.shape
    tm, tn = min(tm, m), min(tn, n)
    return pl.pallas_call(
        functools.partial(_mm_kernel, act=act),
        grid=(m // tm, n // tn),
        in_specs=[pl.BlockSpec((tm, k), lambda i, j: (i, 0)),
                  pl.BlockSpec((pl.Element(1), pl.Element(k), pl.Element(tn)),
                               lambda i, j: (layer, 0, pl.multiple_of(col0 + j * tn, LANES)))],
        out_specs=pl.BlockSpec((tm, tn), lambda i, j: (i, j)),
        out_shape=jax.ShapeDtypeStruct((m, n), out_dtype),
        compiler_params=_cparams("parallel", "parallel"),
        name=name,
    )(a, w)


def _mm_cast_kernel(a_ref, b_ref, src_ref, o_ref, dst_ref, *, act):
    _mm_kernel(a_ref, b_ref, o_ref, act=act)
    dst_ref[0] = src_ref[...].astype(dst_ref.dtype)


def _matmul_and_cast(a, w, layer, col0, n, out_dtype, tm, tn, src, src_layer, act=None, name="matmul_cast"):
    m, k = a.shape
    tm, tn = min(tm, m), min(tn, n)
    ni, nj = m // tm, n // tn
    rows, cols = src.shape[1] // (ni * nj), src.shape[2]
    assert rows * ni * nj == src.shape[1] and rows % BF16_SUBLANES == 0
    return pl.pallas_call(
        functools.partial(_mm_cast_kernel, act=act),
        grid=(ni, nj),
        in_specs=[pl.BlockSpec((tm, k), lambda i, j: (i, 0)),
                  pl.BlockSpec((pl.Element(1), pl.Element(k), pl.Element(tn)),
                               lambda i, j: (layer, 0, pl.multiple_of(col0 + j * tn, LANES))),
                  pl.BlockSpec((None, rows, cols), lambda i, j: (src_layer, i * nj + j, 0))],
        out_specs=[pl.BlockSpec((tm, tn), lambda i, j: (i, j)),
                   pl.BlockSpec((1, rows, cols), lambda i, j: (0, i * nj + j, 0))],
        out_shape=[jax.ShapeDtypeStruct((m, n), out_dtype),
                   jax.ShapeDtypeStruct((1, src.shape[1], cols), BF16)],
        compiler_params=_cparams("parallel", "parallel"),
        name=name,
    )(a, w, src)


def _mm_residual_kernel(a_ref, b_ref, x_ref, o_ref):
    o_ref[...] = x_ref[...] + _dot(a_ref[...], b_ref[...])


def _matmul_residual(a, w, layer, x, tm, tn):
    m, k = a.shape
    n = w.shape[2]
    tm, tn = min(tm, m), min(tn, n)
    return pl.pallas_call(
        _mm_residual_kernel,
        grid=(n // tn, m // tm),
        in_specs=[pl.BlockSpec((tm, k), lambda j, i: (i, 0)),
                  pl.BlockSpec((None, k, tn), lambda j, i: (layer, 0, j)),
                  pl.BlockSpec((tm, tn), lambda j, i: (i, j))],
        out_specs=pl.BlockSpec((tm, tn), lambda j, i: (i, j)),
        out_shape=jax.ShapeDtypeStruct((m, n), x.dtype),
        compiler_params=_cparams("parallel", "parallel"),
        name="out_proj_residual",
    )(a, w, x)


def _merge_kernel(y0, y1, y2, y3, wb_ref, g0, g1, g2, g3, o_ref):
    acc = None
    for n, (y, g) in enumerate(((y0, g0), (y1, g1), (y2, g2), (y3, g3))):
        term = g[...].astype(F32) * _dot(y[...], wb_ref[n])
        acc = term if acc is None else acc + term
    o_ref[...] = acc.astype(o_ref.dtype)


def _merge(ys, wb, layer, gates, tm, tn):
    m, bw = ys[0].shape
    d = wb.shape[3]
    tm, tn = min(tm, m), min(tn, d)
    nj = d // tn
    y_spec = pl.BlockSpec((tm, bw), lambda j, i: (i, 0))
    g_specs = [pl.BlockSpec((tm, tn), functools.partial(lambda j, i, n: (i, n * nj + j), n=n))
               for n in range(4)]
    return pl.pallas_call(
        _merge_kernel,
        grid=(nj, m // tm),
        in_specs=[y_spec] * 4 + [pl.BlockSpec((None, 4, bw, tn), lambda j, i: (layer, 0, 0, j))] + g_specs,
        out_specs=pl.BlockSpec((tm, tn), lambda j, i: (i, j)),
        out_shape=jax.ShapeDtypeStruct((m, d), BF16),
        compiler_params=_cparams("parallel", "parallel"),
        name="gated_merge",
    )(*ys, wb, gates, gates, gates, gates)


def _token_shift(h_ref, hp_ref, hn_ref, conv_ref):
    i = pl.program_id(1)
    nt = pl.num_programs(1)
    h = h_ref[0].astype(F32)
    tt = h.shape[0]
    prev_row = hp_ref[0, BF16_SUBLANES - 1:BF16_SUBLANES, :].astype(F32) * (i > 0).astype(F32)
    next_row = hn_ref[0, 0:1, :].astype(F32) * (i < nt - 1).astype(F32)
    rows = lax.broadcasted_iota(jnp.int32, (tt, 1), 0)
    h_dn = jnp.where(rows == 0, prev_row, pltpu.roll(h, 1, 0))
    h_up = jnp.where(rows == tt - 1, next_row, pltpu.roll(h, tt - 1, 0))
    return h_dn * conv_ref[0:1, :] + h * conv_ref[1:2, :] + h_up * conv_ref[2:3, :]


def _rwkv_prep_kernel(h_ref, hp_ref, hn_ref, l_ref, lp_ref, ln_ref, conv_ref, convl_ref, wup_ref, w0_ref,
                      aup_ref, a0_ref, kk_ref, ka_ref, rk_ref,
                      r_o, v_o, kk_o, lw0_o, lw1_o, kt0_o, kt1_o, b0_o, b1_o, bonus_o):
    hs = _token_shift(h_ref, hp_ref, hn_ref, conv_ref)
    ls = _token_shift(l_ref, lp_ref, ln_ref, convl_ref)
    bw = BRANCH_W
    r = hs[:, 0:bw]
    k = hs[:, bw:2 * bw]
    v = hs[:, 2 * bw:3 * bw]
    wd = jnp.tanh(ls[:, 0:2 * LORA])
    ad = ls[:, 2 * LORA:4 * LORA]

    ones_bd = _head_block_ones(RWKV_HEAD)
    kkr = k * kk_ref[...]
    nrm = jnp.sqrt(_head_sum(kkr * kkr, ones_bd))
    kk = kkr / jnp.maximum(nrm, 1e-12)

    def up_proj(x_pieces, w_ref, z):
        (x_hi, x_lo), w_hi, w_lo = x_pieces, w_ref[0, z], w_ref[1, z]
        return _dot(x_hi, w_hi) + (_dot(x_lo, w_hi) + _dot(x_hi, w_lo))

    wd_pieces, ad_pieces = _split_bf16(wd, 2), _split_bf16(ad, 2)
    decay_scale = np.float32(np.exp(-0.5))
    kts = []
    for z, (lw_o, kt_o, b_o) in enumerate(((lw0_o, kt0_o, b0_o), (lw1_o, kt1_o, b1_o))):
        w_raw = w0_ref[z:z + 1, :] + up_proj(wd_pieces, wup_ref, z)
        lw_o[0] = -decay_scale * _sigmoid(w_raw)
        a = _sigmoid(a0_ref[z:z + 1, :] + up_proj(ad_pieces, aup_ref, z))
        kt = k * (1.0 + (a - 1.0) * ka_ref[...])
        kt_o[0] = kt
        b_o[0] = kk * a
        kts.append(kt)
    r_o[0] = r
    v_o[0] = v
    kk_o[0] = kk
    bonus_o[0] = _head_sum(r * (kts[0] + kts[1]) * rk_ref[...], ones_bd) * v


def _rwkv_prep(h_rkv, h_lora, conv, wup_pad, w0, aup_pad, a0, k_k, k_a, r_k, tt=256):
    bsz, t, w = h_rkv.shape
    wl = h_lora.shape[2]
    tt = min(tt, t)
    nt = t // tt
    hb = tt // BF16_SUBLANES
    n_halo = t // BF16_SUBLANES
    row = lambda a: a.reshape(1, -1)
    vec_spec = pl.BlockSpec((1, BRANCH_W), lambda b, i: (0, 0))
    out_spec = pl.BlockSpec((1, tt, BRANCH_W), lambda b, i: (b, i, 0))
    out_sds = jax.ShapeDtypeStruct((bsz, t, BRANCH_W), F32)
    tile = lambda width: [
        pl.BlockSpec((1, tt, width), lambda b, i: (b, i, 0)),
        pl.BlockSpec((1, BF16_SUBLANES, width), lambda b, i: (b, jnp.maximum(i * hb - 1, 0), 0)),
        pl.BlockSpec((1, BF16_SUBLANES, width), lambda b, i: (b, jnp.minimum((i + 1) * hb, n_halo - 1), 0))]
    return pl.pallas_call(
        _rwkv_prep_kernel,
        grid=(bsz, nt),
        in_specs=tile(w) + tile(wl) + [
            pl.BlockSpec((3, w), lambda b, i: (0, 0)),
            pl.BlockSpec((3, wl), lambda b, i: (0, 0)),
            pl.BlockSpec((2, 2, 2 * LORA, BRANCH_W), lambda b, i: (0, 0, 0, 0)),
            pl.BlockSpec((2, BRANCH_W), lambda b, i: (0, 0)),
            pl.BlockSpec((2, 2, 2 * LORA, BRANCH_W), lambda b, i: (0, 0, 0, 0)),
            pl.BlockSpec((2, BRANCH_W), lambda b, i: (0, 0)),
            vec_spec, vec_spec, vec_spec,
        ],
        out_specs=[out_spec] * 10,
        out_shape=[out_sds] * 10,
        compiler_params=_cparams("parallel", "parallel"),
        name="rwkv_prep",
    )(h_rkv, h_rkv, h_rkv, h_lora, h_lora, h_lora, conv[:, :w], conv[:, w:], wup_pad, w0, aup_pad, a0,
      row(k_k), row(k_a), row(r_k))


def _wkv_masks():
    c = WKV_CHUNK
    row = lax.broadcasted_iota(jnp.int32, (2 * c, 2 * c), 0)
    col = lax.broadcasted_iota(jnp.int32, (2 * c, 2 * c), 1)
    same = (row >> 6) == (col >> 6)
    rt, ct = row & (c - 1), col & (c - 1)
    f = lambda m: jnp.where(same & m, 1.0, 0.0).astype(F32)
    lane = lax.broadcasted_iota(jnp.int32, (1, LANES), 1)
    crow = lax.broadcasted_iota(jnp.int32, (c, c), 0)
    ccol = lax.broadcasted_iota(jnp.int32, (c, c), 1)
    return {
        "eye": f(rt == ct),
        "strict": (f(ct < rt), f(ct > rt)),
        "incl": (f(ct <= rt), f(ct >= rt)),
        "cum": (jnp.where(ccol <= crow, 1.0, 0.0).astype(BF16), jnp.where(ccol >= crow, 1.0, 0.0).astype(BF16)),
        "head0": jnp.where(lane < RWKV_HEAD, 1.0, 0.0).astype(F32),
        "head1": jnp.where(lane < RWKV_HEAD, 0.0, 1.0).astype(F32),
    }


def _wkv_local(problems, masks, fillers=()):
    c = WKV_CHUNK
    bf = lambda x: x.astype(BF16)
    m0, m1 = masks["head0"], masks["head1"]
    pair = lambda x: jnp.concatenate([x * m0, x * m1], axis=0)
    each = lambda fn, *lists: [fn(*xs) for xs in zip(*lists)]
    dirs = [p[6] for p in problems]
    fillers = list(fillers)
    n_points = 16
    stride = max(1, n_points // max(1, len(fillers)))
    seen = [0]

    def fill():
        seen[0] += 1
        if fillers and seen[0] % stride == 0:
            fillers.pop(0)()

    cl = [_cumsum_dot(masks["cum"][p[6]], p[0]) for p in problems]
    tot = [x[c - 1:c, :] if d == 0 else x[0:1, :] for x, d in zip(cl, dirs)]
    fill()
    zp = [pair(-p[4] * jnp.exp(x - p[0])) for p, x in zip(problems, cl)]
    rp = [pair(p[1] * jnp.exp(x)) for p, x in zip(problems, cl)]
    vpb = [bf(pair(p[3])) for p in problems]
    e_neg = [jnp.exp(-x) for x in cl]
    bk_start = [bf(jnp.concatenate([pair(p[5] * e), pair(p[2] * e)], axis=0)) for p, e in zip(problems, e_neg)]
    e_end = [jnp.exp(t - x) for t, x in zip(tot, cl)]
    bk_end = [bf(jnp.concatenate([pair(p[5] * e), pair(p[2] * e)], axis=0)) for p, e in zip(problems, e_end)]
    fill()

    scores = each(lambda z, r, bk: _dot_nt(bf(jnp.concatenate([z, r], axis=0)), bk), zp, rp, bk_start)
    fill()
    strict = [masks["strict"][d] for d in dirs]
    incl = [masks["incl"][d] for d in dirs]
    l_zb = each(lambda s, m: s[0:2 * c, 0:2 * c] * m, scores, strict)
    a_zk = each(lambda s, m: bf(s[0:2 * c, 2 * c:4 * c] * m), scores, strict)
    a_r = each(lambda s, m: bf(jnp.concatenate([s[2 * c:4 * c, 0:2 * c] * m, s[2 * c:4 * c, 2 * c:4 * c] * m],
                                               axis=1)), scores, incl)
    azk_v = each(_dot, a_zk, vpb)
    fill()

    inv = [masks["eye"] + l for l in l_zb]
    pb = [bf(l) for l in l_zb]
    for _ in range(5):
        pb = [bf(_dot(x, x)) for x in pb]
        fill()
        inv = each(lambda t, x: t + _dot(bf(t), x), inv, pb)
        fill()

    zu = each(lambda t, z, u: _dot(bf(t), bf(jnp.concatenate([z, u], axis=1))), inv, zp, azk_v)
    fill()
    stack = each(lambda x, v: jnp.concatenate([bf(x), jnp.concatenate([jnp.zeros_like(v), v], axis=1)], axis=0),
                 zu, vpb)
    ry = each(_dot, a_r, stack)
    fill()
    gh = each(_dot_tn, stack, bk_end)
    while fillers:
        fillers.pop(0)()
    out = []
    for r, y, g, t in zip(rp, ry, gh, tot):
        rb = r + y[:, 0:2 * c]
        out.append((rb[0:c] + rb[c:2 * c], y[0:c, 2 * c:4 * c] + y[c:2 * c, 2 * c:4 * c],
                    g[0:2 * c], g[2 * c:4 * c], jnp.exp(t)))
    return out


def _cumsum_dot(cum_bf16, lw):
    acc = None
    for piece in _split_bf16(lw, 2):
        d = _dot(cum_bf16, piece)
        acc = d if acc is None else acc + d
    return acc


def _wkv_kernel(r_f, v_f, kk_f, lw_f, kt_f, b_f, r_b, v_b, kk_b, lw_b, kt_b, b_b, yf_o, yb_o,
                s_ref, rb_ref, g_ref, h_ref, wc_ref):
    @pl.when(pl.program_id(2) == 0)
    def _():
        s_ref[...] = jnp.zeros_like(s_ref)

    c = WKV_CHUNK
    n_chunks = r_f.shape[1] // c
    half = (5 * n_chunks) // 8
    masks = _wkv_masks()
    ins = ((lw_f, r_f, kt_f, v_f, kk_f, b_f), (lw_b, r_b, kt_b, v_b, kk_b, b_b))
    outs = (yf_o, yb_o)
    streams = [(d, p) for d in range(2) for p in range(WKV_PAIRS)]
    states = [s_ref[q] for q in range(len(streams))]
    chunk_of = lambda d, step: step if d == 0 else n_chunks - 1 - step
    lanes = lambda p: slice(p * LANES, (p + 1) * LANES)

    def local_factors(steps, fillers):
        where = [(q, chunk_of(streams[q][0], s)) for s in steps for q in range(len(streams))]
        problems = [[ref[0, ci * c:(ci + 1) * c, lanes(streams[q][1])] for ref in ins[streams[q][0]]]
                    + [streams[q][0]] for q, ci in where]
        for (q, ci), (rb, yloc, g, h, wc) in zip(where, _wkv_local(problems, masks, fillers)):
            d, p = streams[q]
            rb_ref[q, ci * c:(ci + 1) * c, :] = rb.astype(BF16)
            outs[d][0, ci * c:(ci + 1) * c, lanes(p)] = yloc
            g_ref[q, ci] = g.astype(BF16)
            h_ref[q, ci] = h
            wc_ref[q, ci] = jnp.broadcast_to(wc, (8, LANES))

    def state_step(step):
        for q, (d, p) in enumerate(streams):
            ci = chunk_of(d, step)
            rows = slice(ci * c, (ci + 1) * c)
            sb = states[q].astype(BF16)
            outs[d][0, rows, lanes(p)] = outs[d][0, rows, lanes(p)] + _dot_nt(rb_ref[q, rows, :], sb)
            states[q] = states[q] * wc_ref[q, ci, 0:1, :] + _dot(sb, g_ref[q, ci]) + h_ref[q, ci]

    local_factors(range(0, half), ())
    local_factors(range(half, n_chunks), [functools.partial(state_step, s) for s in range(half)])
    for s in range(half, n_chunks):
        state_step(s)
    for q in range(len(streams)):
        s_ref[q] = states[q]


WKV_PAIRS = 2


def _wkv(r, v, kk, lw0, lw1, kt0, kt1, b0, b1, tb=512):
    bsz, t, w = r.shape
    tb = min(tb, t)
    nb = t // tb
    nc = tb // WKV_CHUNK
    wl = WKV_PAIRS * LANES
    ns = 2 * WKV_PAIRS
    fwd = pl.BlockSpec((1, tb, wl), lambda b, h, g: (b, g, h))
    bwd = pl.BlockSpec((1, tb, wl), lambda b, h, g: (b, nb - 1 - g, h))
    sds = jax.ShapeDtypeStruct((bsz, t, w), F32)
    return pl.pallas_call(
        _wkv_kernel,
        grid=(bsz, w // wl, nb),
        in_specs=[fwd] * 6 + [bwd] * 6,
        out_specs=[fwd, bwd],
        out_shape=[sds, sds],
        scratch_shapes=[pltpu.VMEM((ns, LANES, LANES), F32), pltpu.VMEM((ns, tb, LANES), BF16),
                        pltpu.VMEM((ns, nc, LANES, LANES), BF16), pltpu.VMEM((ns, nc, LANES, LANES), F32),
                        pltpu.VMEM((ns, nc, 8, LANES), F32)],
        compiler_params=_cparams("parallel", "parallel", "arbitrary"),
        name="wkv7_chunked",
    )(r, v, kk, lw0, kt0, b0, r, v, kk, lw1, kt1, b1)


def _rwkv_post_kernel(yf_ref, yb_ref, bonus_ref, g_ref, lw_ref, lb_ref, o_ref):
    ones_bd = _head_block_ones(RWKV_HEAD)
    wkv = yf_ref[0] + yb_ref[0]
    inv_n = 1.0 / RWKV_HEAD
    mu = _head_sum(wkv, ones_bd) * inv_n
    d = wkv - mu
    var = _head_sum(d * d, ones_bd) * inv_n
    gn = d * lax.rsqrt(var + LNX_EPS) * lw_ref[...] + lb_ref[...]
    o_ref[0] = ((gn + bonus_ref[0]) * _silu(g_ref[0].astype(F32))).astype(o_ref.dtype)


def _rwkv_post(yf, yb, bonus, g, lnx_w, lnx_b, tt=512):
    bsz, t, w = yf.shape
    tt = min(tt, t)
    spec = pl.BlockSpec((1, tt, w), lambda b, i: (b, i, 0))
    vec = pl.BlockSpec((1, w), lambda b, i: (0, 0))
    return pl.pallas_call(
        _rwkv_post_kernel,
        grid=(bsz, t // tt),
        in_specs=[spec, spec, spec, spec, vec, vec],
        out_specs=spec,
        out_shape=jax.ShapeDtypeStruct((bsz, t, w), BF16),
        compiler_params=_cparams("parallel", "parallel"),
        name="rwkv_post",
    )(yf, yb, bonus, g, lnx_w.reshape(1, w), lnx_b.reshape(1, w))


def _sgu_kernel(h_ref, lg_ref, lb_ref, ws_ref, bs_ref, o_ref):
    bw = BRANCH_W
    tt = h_ref.shape[1]
    u = _gelu_tanh(h_ref[0, :, 0:bw].astype(F32))
    vv = _gelu_tanh(h_ref[0, :, bw:2 * bw].astype(F32))
    g = h_ref[0, :, 2 * bw:3 * bw].astype(F32)
    mu = jnp.mean(vv, axis=-1, keepdims=True)
    d = vv - mu
    var = jnp.mean(d * d, axis=-1, keepdims=True)
    vn = (d * lax.rsqrt(var + 1e-5) * lg_ref[...] + lb_ref[...]).astype(BF16)
    gate = u * _silu(g)
    for ck in range(tt // SG_CHUNK):
        rs = slice(ck * SG_CHUNK, (ck + 1) * SG_CHUNK)
        for grp in range(SG_GROUPS):
            cs = slice(grp * LANES, (grp + 1) * LANES)
            sv = _dot(ws_ref[grp], vn[rs, cs]) + bs_ref[:, cs]
            o_ref[0, rs, cs] = (gate[rs, cs] * sv).astype(o_ref.dtype)


def _sgu(h_b, ln_g, ln_b, w_s, bs_cols, tt=256):
    bsz, t, w3 = h_b.shape
    bw = BRANCH_W
    tt = min(tt, t)
    vec = pl.BlockSpec((1, bw), lambda b, i: (0, 0))
    return pl.pallas_call(
        _sgu_kernel,
        grid=(bsz, t // tt),
        in_specs=[pl.BlockSpec((1, tt, w3), lambda b, i: (b, i, 0)), vec, vec,
                  pl.BlockSpec((SG_GROUPS, SG_CHUNK, SG_CHUNK), lambda b, i: (0, 0, 0)),
                  pl.BlockSpec((SG_CHUNK, bw), lambda b, i: (0, 0))],
        out_specs=pl.BlockSpec((1, tt, bw), lambda b, i: (b, i, 0)),
        out_shape=jax.ShapeDtypeStruct((bsz, t, bw), BF16),
        compiler_params=_cparams("parallel", "parallel"),
        name="spatial_gating",
    )(h_b, ln_g.reshape(1, bw), ln_b.reshape(1, bw), w_s, bs_cols)


def _natten_bias_table(rpb):
    p = np.arange(GRID_W)[:, None]
    m = np.arange(GRID_W)[None, :]
    sj = np.clip(p - NA_COLS // 2, 0, GRID_W - NA_COLS)
    valid = (m >= sj) & (m < sj + NA_COLS)
    dc = np.clip(m - p, -(NA_COLS - 1), NA_COLS - 1) + NA_COLS - 1
    by_rel = jnp.where(valid[None, None], rpb[:, :, dc], NEG_INF)
    tab = jnp.stack([by_rel[:, s:s + NA_ROWS] for s in range(NA_ROWS)], axis=1)
    tab = tab.transpose(0, 1, 3, 2, 4)
    return tab.reshape(rpb.shape[0], NA_ROWS, GRID_W, NA_ROWS * GRID_W).astype(F32)


NA_UNROLL = 16


def _natten_kernel(q_ref, k_ref, v_ref, g_ref, qn_ref, kn_ref, bias_ref, o_ref, qs_ref, ks_ref):
    t = q_ref.shape[1]
    n_rows = t // GRID_W
    win = NA_ROWS * GRID_W
    ones_bd = _head_block_ones(NA_HEAD)
    lane = lax.broadcasted_iota(jnp.int32, (1, LANES), 1)
    m0 = lane < NA_HEAD

    def norm(x_ref, gain_ref, scale):
        x = x_ref[0].astype(F32)
        ms = _head_sum(x * x, ones_bd) * (1.0 / NA_HEAD)
        return x * lax.rsqrt(ms + 1e-6) * (gain_ref[...] * scale)

    qs_ref[...] = norm(q_ref, qn_ref, NA_HEAD ** -0.5).astype(BF16)
    ks_ref[...] = norm(k_ref, kn_ref, 1.0).astype(BF16)

    def body(it, carry):
        rows = [it * NA_UNROLL + j for j in range(NA_UNROLL)]
        si = [jnp.clip(i - NA_ROWS // 2, 0, n_rows - NA_ROWS) for i in rows]
        start = [s - i + (NA_ROWS - 1) for s, i in zip(si, rows)]
        qo = [pl.multiple_of(i * GRID_W, GRID_W) for i in rows]
        ko = [pl.multiple_of(s * GRID_W, GRID_W) for s in si]
        q = [qs_ref[pl.ds(o, GRID_W), :] for o in qo]
        zero = jnp.zeros_like(q[0])
        q2 = [jnp.concatenate([jnp.where(m0, x, zero), jnp.where(m0, zero, x)], axis=0) for x in q]
        s = [_dot_nt(x, ks_ref[pl.ds(o, win), :]) for x, o in zip(q2, ko)]
        s = [x + jnp.concatenate([bias_ref[0, st], bias_ref[1, st]], axis=0) for x, st in zip(s, start)]
        e = [jnp.exp(x - jnp.max(x, axis=-1, keepdims=True)).astype(BF16) for x in s]
        o2 = [_dot(x, jnp.concatenate([v_ref[0, pl.ds(o, win), :], ones_blk], axis=1))
              for x, o in zip(e, ko)]
        for x, o in zip(o2, qo):
            x = x[:, 0:LANES] / x[:, LANES:2 * LANES]
            g = g_ref[0, pl.ds(o, GRID_W), :].astype(F32)
            val = jnp.where(m0, x[0:GRID_W], x[GRID_W:2 * GRID_W]) * _silu(g)
            o_ref[0, pl.ds(o, GRID_W), :] = val.astype(o_ref.dtype)
        return carry

    ones_blk = jnp.ones((win, LANES), BF16)

    lax.fori_loop(0, n_rows // NA_UNROLL, body, 0)


def _natten(h_c, q_norm, k_norm, bias_tab):
    bsz, t, w4 = h_c.shape
    bw = BRANCH_W
    nlb = bw // LANES
    sec = lambda s: pl.BlockSpec((1, t, LANES), functools.partial(lambda b, hp, s: (b, 0, s * nlb + hp), s=s))
    two = lambda a: jnp.concatenate([a, a]).reshape(1, LANES)
    return pl.pallas_call(
        _natten_kernel,
        grid=(bsz, nlb),
        in_specs=[sec(0), sec(1), sec(2), sec(3),
                  pl.BlockSpec((1, LANES), lambda b, hp: (0, 0)), pl.BlockSpec((1, LANES), lambda b, hp: (0, 0)),
                  pl.BlockSpec((2, NA_ROWS, GRID_W, NA_ROWS * GRID_W), lambda b, hp: (hp, 0, 0, 0))],
        out_specs=pl.BlockSpec((1, t, LANES), lambda b, hp: (b, 0, hp)),
        out_shape=jax.ShapeDtypeStruct((bsz, t, bw), BF16),
        scratch_shapes=[pltpu.VMEM((t, LANES), BF16), pltpu.VMEM((t, LANES), BF16)],
        compiler_params=_cparams("parallel", "parallel"),
        name="neighbourhood_attention",
    )(h_c, h_c, h_c, h_c, two(q_norm), two(k_norm), bias_tab)


def _memattn_kernel(qg_ref, kv_ref, qn_ref, kn_ref, o_ref):
    hd, bw = MEM_HEAD, BRANCH_W

    def norm(x, gain, scale):
        ms = jnp.mean(x * x, axis=-1, keepdims=True)
        return (x * lax.rsqrt(ms + 1e-6) * (gain * scale)).astype(BF16)

    heads = [slice(h * hd, (h + 1) * hd) for h in range(MEM_HEADS)]
    q = [norm(qg_ref[0, :, c].astype(F32), qn_ref[...], hd ** -0.5) for c in heads]
    k = [norm(kv_ref[0, :, c].astype(F32), kn_ref[...], 1.0) for c in heads]
    s = [_dot_nt(a, b) for a, b in zip(q, k)]
    e = [jnp.exp(x - jnp.max(x, axis=-1, keepdims=True)) for x in s]
    p = [(x / jnp.sum(x, axis=-1, keepdims=True)).astype(BF16) for x in e]
    o = [_dot(x, kv_ref[0, :, bw + h * hd:bw + (h + 1) * hd]) for h, x in enumerate(p)]
    for c, x in zip(heads, o):
        g = qg_ref[0, :, bw + c.start:bw + c.stop].astype(F32)
        o_ref[0, :, c] = (x * _silu(g)).astype(o_ref.dtype)


def _memattn(h_m, kv, q_norm, k_norm, tt=512):
    bsz, t, w2 = h_m.shape
    mlen = kv.shape[1]
    tt = min(tt, t)
    vec = pl.BlockSpec((1, MEM_HEAD), lambda b, i: (0, 0))
    return pl.pallas_call(
        _memattn_kernel,
        grid=(bsz, t // tt),
        in_specs=[pl.BlockSpec((1, tt, w2), lambda b, i: (b, i, 0)),
                  pl.BlockSpec((1, mlen, w2), lambda b, i: (b, 0, 0)),
                  vec, vec],
        out_specs=pl.BlockSpec((1, tt, BRANCH_W), lambda b, i: (b, i, 0)),
        out_shape=jax.ShapeDtypeStruct((bsz, t, BRANCH_W), BF16),
        compiler_params=_cparams("parallel", "parallel"),
        name="memory_attention",
    )(h_m, kv, q_norm.reshape(1, MEM_HEAD), k_norm.reshape(1, MEM_HEAD))


def _pad_lora(up):
    z = jnp.zeros_like(up[0])
    w = jnp.stack([jnp.concatenate([up[0], z], axis=0), jnp.concatenate([z, up[1]], axis=0)])
    hi = w.astype(BF16)
    return jnp.stack([hi, (w - hi.astype(F32)).astype(BF16)])


def _layer(x2d, mem2d, bsz, layer, p, big, w_in_bf16, w_in_f32):
    m, d = x2d.shape
    t = m // bsz
    bw = BRANCH_W
    a_w = A_SHIFT_W + bw
    o1, o2, o3, o4 = a_w, a_w + 3 * bw, a_w + 7 * bw, a_w + 9 * bw
    xn = _rmsnorm(x2d, p["norm_g"], BF16)
    proj = functools.partial(_matmul, xn, w_in_bf16, 0)
    proj_cast = functools.partial(_matmul_and_cast, xn, w_in_bf16, 0)
    n_layers = w_in_f32.shape[0]

    h_rkv = proj(0, 3 * bw, BF16, 1024, 1024, name="proj_a_rkv")
    h_lora = proj(3 * bw, 4 * LORA, BF16, 1024, 4 * LORA, name="proj_a_lora")
    h_ag, w_kv = proj_cast(A_SHIFT_W, bw, BF16, 1024, 1024, big["m_w_kv"], layer, name="proj_a_gate_cast_kv")
    h_b = proj(o1, 3 * bw, BF16, 1024, 1024, name="proj_b")
    h_c, w_br = proj_cast(o2, 4 * bw, BF16, 1024, 1024, big["w_branch"].reshape(n_layers, N_BRANCH * bw, d),
                          layer, name="proj_c_cast_branch")
    h_m, w_o = proj_cast(o3, 2 * bw, BF16, 1024, 1024, big["w_out"], layer, name="proj_m_cast_out")
    if layer + 1 < n_layers:
        gates, w_next = _matmul_and_cast(xn, w_in_bf16, 0, o4, 4 * d, BF16, 1024, 1024, w_in_f32, layer + 1,
                                         act="sigmoid", name="proj_gates_cast_next")
    else:
        gates, w_next = proj(o4, 4 * d, BF16, 1024, 1024, act="sigmoid", name="proj_gates"), None

    prep = _rwkv_prep(h_rkv.reshape(bsz, t, 3 * bw), h_lora.reshape(bsz, t, 4 * LORA), p["a_conv"],
                      _pad_lora(p["a_w_up"]), p["a_w0"], _pad_lora(p["a_a_up"]), p["a_a0"],
                      p["a_k_k"], p["a_k_a"], p["a_r_k"].reshape(-1))
    r, v, kk, lw0, lw1, kt0, kt1, b0, b1, bonus = prep
    yf, yb = _wkv(r, v, kk, lw0, lw1, kt0, kt1, b0, b1)
    y_a = _rwkv_post(yf, yb, bonus, h_ag.reshape(bsz, t, bw), p["a_lnx_w"], p["a_lnx_b"])

    bs_cols = jnp.repeat(p["b_b_s"].T, SG_CHUNK, axis=1)
    y_b = _sgu(h_b.reshape(bsz, t, 3 * bw), p["b_ln_g"], p["b_ln_b"], p["b_w_s"].astype(BF16), bs_cols)

    y_c = _natten(h_c.reshape(bsz, t, 4 * bw), p["c_q_norm"], p["c_k_norm"], _natten_bias_table(p["c_rpb"]))

    mem_n = _rmsnorm(mem2d, p["m_norm_g"], BF16)
    kv = _matmul(mem_n, w_kv, 0, 0, 2 * bw, BF16, 1024, 1024, name="proj_mem_kv")
    y_d = _memattn(h_m.reshape(bsz, t, 2 * bw), kv.reshape(bsz, -1, 2 * bw), p["m_q_norm"], p["m_k_norm"])

    ys = [y.reshape(m, bw) for y in (y_a, y_b, y_c, y_d)]
    merged = _merge(ys, w_br.reshape(1, N_BRANCH, bw, d), 0, gates, 512, 1024)
    return _matmul_residual(merged, w_o, 0, x2d, 512, 1024), w_next


def kernel(x, mem, norm_g, w_in, a_conv, a_w_up, a_w0, a_a_up, a_a0, a_k_k, a_k_a, a_r_k, a_lnx_w, a_lnx_b,
           b_ln_g, b_ln_b, b_w_s, b_b_s, c_q_norm, c_k_norm, c_rpb, m_norm_g, m_w_kv, m_q_norm, m_k_norm,
           w_branch, w_out):
    params = dict(norm_g=norm_g, a_conv=a_conv, a_w_up=a_w_up, a_w0=a_w0, a_a_up=a_a_up, a_a0=a_a0,
                  a_k_k=a_k_k, a_k_a=a_k_a, a_r_k=a_r_k, a_lnx_w=a_lnx_w, a_lnx_b=a_lnx_b, b_ln_g=b_ln_g,
                  b_ln_b=b_ln_b, b_w_s=b_w_s, b_b_s=b_b_s, c_q_norm=c_q_norm, c_k_norm=c_k_norm, c_rpb=c_rpb,
                  m_norm_g=m_norm_g, m_q_norm=m_q_norm, m_k_norm=m_k_norm)
    big = dict(m_w_kv=m_w_kv, w_branch=w_branch, w_out=w_out)
    bsz, t, d = x.shape
    x2d = x.reshape(bsz * t, d)
    mem2d = mem.reshape(-1, d)
    w_cur = w_in[0:1].astype(BF16)
    for l in range(norm_g.shape[0]):
        x2d, w_cur = _layer(x2d, mem2d, bsz, l, {k: v[l] for k, v in params.items()}, big, w_cur, w_in)
    return x2d.reshape(bsz, t, d)
```

```python
import functools

import numpy as np
import jax
import jax.numpy as jnp
from jax import lax
from jax.experimental import pallas as pl
from jax.experimental.pallas import tpu as pltpu

F32 = jnp.float32
BF16 = jnp.bfloat16

VMEM_LIMIT_BYTES = 56 * 1024 * 1024
LANES = 128
BF16_SUBLANES = 16
F32_SUBLANES = 8

BRANCH_W = 1024
N_BRANCH = 4
RWKV_HEAD = 64
LORA = 64
A_SHIFT_W = 3 * BRANCH_W + 4 * LORA
LNX_EPS = 64e-5
SG_CHUNK = 128
SG_GROUPS = 8
NA_HEAD = 64
NA_ROWS = 8
NA_COLS = 16
GRID_W = 64
MEM_HEADS = 4
MEM_HEAD = 256
WKV_CHUNK = 64
NEG_INF = -1e30


def _cparams(*sem):
    return pltpu.CompilerParams(dimension_semantics=sem, vmem_limit_bytes=VMEM_LIMIT_BYTES)


def _sigmoid(x):
    return 0.5 * jnp.tanh(0.5 * x) + 0.5


def _silu(x):
    return x * _sigmoid(x)


def _gelu_tanh(x):
    c = np.float32(np.sqrt(2.0 / np.pi))
    half = 0.5 * x
    return half + half * jnp.tanh(x * (c + np.float32(c * 0.044715) * (x * x)))


def _dot(a, b, precision=None):
    return jnp.dot(a, b, preferred_element_type=F32, precision=precision)


def _dot_nt(a, b, precision=None):
    return lax.dot_general(a, b, (((1,), (1,)), ((), ())), preferred_element_type=F32,
                           precision=precision)


def _dot_tn(a, b, precision=None):
    return lax.dot_general(a, b, (((0,), (0,)), ((), ())), preferred_element_type=F32,
                           precision=precision)


def _split_bf16(x, terms):
    parts = []
    for _ in range(terms - 1):
        hi = x.astype(BF16)
        parts.append(hi)
        x = x - hi.astype(F32)
    parts.append(x.astype(BF16))
    return parts


def _dot_exact_rhs(x, w_bf16, terms):
    acc = None
    for piece in _split_bf16(x, terms):
        d = _dot(piece, w_bf16)
        acc = d if acc is None else acc + d
    return acc


def _head_block_ones(head):
    shift = int(np.log2(head))
    r = lax.broadcasted_iota(jnp.int32, (LANES, LANES), 0) >> shift
    c = lax.broadcasted_iota(jnp.int32, (LANES, LANES), 1) >> shift
    return jnp.where(r == c, 1.0, 0.0).astype(BF16)


def _head_sum(x, ones_bd, terms=1):
    parts = [_dot_exact_rhs(x[:, j:j + LANES], ones_bd, terms) for j in range(0, x.shape[1], LANES)]
    return parts[0] if len(parts) == 1 else jnp.concatenate(parts, axis=1)


def _rmsnorm_kernel(x_ref, g_ref, o_ref, *, eps):
    d = x_ref.shape[1]
    chunks = [slice(c, c + NORM_CHUNK) for c in range(0, d, NORM_CHUNK)]
    ssq = None
    for cols in chunks:
        x = x_ref[:, cols].astype(F32)
        s = jnp.sum(x * x, axis=-1, keepdims=True)
        ssq = s if ssq is None else ssq + s
    scale = lax.rsqrt(ssq * (1.0 / d) + eps)
    for cols in chunks:
        o_ref[:, cols] = (x_ref[:, cols].astype(F32) * scale * g_ref[:, cols]).astype(o_ref.dtype)


NORM_CHUNK = 512


def _rmsnorm(x2d, g, out_dtype, tm=512, eps=1e-6):
    m, d = x2d.shape
    tm = min(tm, m)
    return pl.pallas_call(
        functools.partial(_rmsnorm_kernel, eps=eps),
        grid=(m // tm,),
        in_specs=[pl.BlockSpec((tm, d), lambda i: (i, 0)), pl.BlockSpec((1, d), lambda i: (0, 0))],
        out_specs=pl.BlockSpec((tm, d), lambda i: (i, 0)),
        out_shape=jax.ShapeDtypeStruct((m, d), out_dtype),
        compiler_params=_cparams("parallel"),
        name="rmsnorm",
    )(x2d, g.reshape(1, d))


def _mm_kernel(a_ref, b_ref, o_ref, *, act):
    acc = _dot(a_ref[...], b_ref[0])
    if act == "sigmoid":
        acc = _sigmoid(acc)
    o_ref[...] = acc.astype(o_ref.dtype)


def _matmul(a, w, layer, col0, n, out_dtype, tm, tn, act=None, name="matmul"):
    m, k = a.shape
    tm, tn = min(tm, m), min(tn, n)
    return pl.pallas_call(
        functools.partial(_mm_kernel, act=act),
        grid=(m // tm, n // tn),
        in_specs=[pl.BlockSpec((tm, k), lambda i, j: (i, 0)),
                  pl.BlockSpec((pl.Element(1), pl.Element(k), pl.Element(tn)),
                               lambda i, j: (layer, 0, pl.multiple_of(col0 + j * tn, LANES)))],
        out_specs=pl.BlockSpec((tm, tn), lambda i, j: (i, j)),
        out_shape=jax.ShapeDtypeStruct((m, n), out_dtype),
        compiler_params=_cparams("parallel", "parallel"),
        name=name,
    )(a, w)


def _mm_cast_kernel(a_ref, b_ref, src_ref, o_ref, dst_ref, *, act):
    _mm_kernel(a_ref, b_ref, o_ref, act=act)
    dst_ref[0] = src_ref[...].astype(dst_ref.dtype)


def _matmul_and_cast(a, w, layer, col0, n, out_dtype, tm, tn, src, src_layer, act=None, name="matmul_cast"):
    m, k = a.shape
    tm, tn = min(tm, m), min(tn, n)
    ni, nj = m // tm, n // tn
    rows, cols = src.shape[1] // (ni * nj), src.shape[2]
    assert rows * ni * nj == src.shape[1] and rows % BF16_SUBLANES == 0
    return pl.pallas_call(
        functools.partial(_mm_cast_kernel, act=act),
        grid=(ni, nj),
        in_specs=[pl.BlockSpec((tm, k), lambda i, j: (i, 0)),
                  pl.BlockSpec((pl.Element(1), pl.Element(k), pl.Element(tn)),
                               lambda i, j: (layer, 0, pl.multiple_of(col0 + j * tn, LANES))),
                  pl.BlockSpec((None, rows, cols), lambda i, j: (src_layer, i * nj + j, 0))],
        out_specs=[pl.BlockSpec((tm, tn), lambda i, j: (i, j)),
                   pl.BlockSpec((1, rows, cols), lambda i, j: (0, i * nj + j, 0))],
        out_shape=[jax.ShapeDtypeStruct((m, n), out_dtype),
                   jax.ShapeDtypeStruct((1, src.shape[1], cols), BF16)],
        compiler_params=_cparams("parallel", "parallel"),
        name=name,
    )(a, w, src)


def _mm_residual_kernel(a_ref, b_ref, x_ref, o_ref):
    o_ref[...] = x_ref[...] + _dot(a_ref[...], b_ref[...])


def _matmul_residual(a, w, layer, x, tm, tn):
    m, k = a.shape
    n = w.shape[2]
    tm, tn = min(tm, m), min(tn, n)
    return pl.pallas_call(
        _mm_residual_kernel,
        grid=(n // tn, m // tm),
        in_specs=[pl.BlockSpec((tm, k), lambda j, i: (i, 0)),
                  pl.BlockSpec((None, k, tn), lambda j, i: (layer, 0, j)),
                  pl.BlockSpec((tm, tn), lambda j, i: (i, j))],
        out_specs=pl.BlockSpec((tm, tn), lambda j, i: (i, j)),
        out_shape=jax.ShapeDtypeStruct((m, n), x.dtype),
        compiler_params=_cparams("parallel", "parallel"),
        name="out_proj_residual",
    )(a, w, x)


def _merge_kernel(y0, y1, y2, y3, wb_ref, g0, g1, g2, g3, o_ref):
    acc = None
    for n, (y, g) in enumerate(((y0, g0), (y1, g1), (y2, g2), (y3, g3))):
        term = g[...].astype(F32) * _dot(y[...], wb_ref[n])
        acc = term if acc is None else acc + term
    o_ref[...] = acc.astype(o_ref.dtype)


def _merge(ys, wb, layer, gates, tm, tn):
    m, bw = ys[0].shape
    d = wb.shape[3]
    tm, tn = min(tm, m), min(tn, d)
    nj = d // tn
    y_spec = pl.BlockSpec((tm, bw), lambda j, i: (i, 0))
    g_specs = [pl.BlockSpec((tm, tn), functools.partial(lambda j, i, n: (i, n * nj + j), n=n))
               for n in range(4)]
    return pl.pallas_call(
        _merge_kernel,
        grid=(nj, m // tm),
        in_specs=[y_spec] * 4 + [pl.BlockSpec((None, 4, bw, tn), lambda j, i: (layer, 0, 0, j))] + g_specs,
        out_specs=pl.BlockSpec((tm, tn), lambda j, i: (i, j)),
        out_shape=jax.ShapeDtypeStruct((m, d), BF16),
        compiler_params=_cparams("parallel", "parallel"),
        name="gated_merge",
    )(*ys, wb, gates, gates, gates, gates)


def _token_shift(h_ref, hp_ref, hn_ref, conv_ref, cols):
    i = pl.program_id(1)
    nt = pl.num_programs(1)
    h = h_ref[0, :, cols].astype(F32)
    tt = h.shape[0]
    prev_row = hp_ref[0, BF16_SUBLANES - 1:BF16_SUBLANES, cols].astype(F32) * (i > 0).astype(F32)
    next_row = hn_ref[0, 0:1, cols].astype(F32) * (i < nt - 1).astype(F32)
    rows = lax.broadcasted_iota(jnp.int32, (F32_SUBLANES, 1), 0)
    dn, up = pltpu.roll(h, 1, 0), pltpu.roll(h, tt - 1, 0)
    h_dn = jnp.concatenate([jnp.where(rows == 0, prev_row, dn[0:F32_SUBLANES]), dn[F32_SUBLANES:]], axis=0)
    h_up = jnp.concatenate([up[0:tt - F32_SUBLANES],
                            jnp.where(rows == F32_SUBLANES - 1, next_row, up[tt - F32_SUBLANES:])], axis=0)
    return h_dn * conv_ref[0:1, cols] + h * conv_ref[1:2, cols] + h_up * conv_ref[2:3, cols]


def _rwkv_prep_kernel(h_ref, hp_ref, hn_ref, l_ref, lp_ref, ln_ref, conv_ref, convl_ref, wup_ref, w0_ref,
                      aup_ref, a0_ref, kk_ref, ka_ref, rk_ref,
                      r_o, v_o, kk_o, lw0_o, lw1_o, kt0_o, kt1_o, b0_o, b1_o, bonus_o):
    bw = BRANCH_W
    ls = _token_shift(l_ref, lp_ref, ln_ref, convl_ref, slice(0, 4 * LORA))
    wd_pieces = _split_bf16(jnp.tanh(ls[:, 0:2 * LORA]), 2)
    ad_pieces = _split_bf16(ls[:, 2 * LORA:4 * LORA], 2)
    ones_bd = _head_block_ones(RWKV_HEAD)
    decay_scale = np.float32(np.exp(-0.5))

    def up_proj(x_pieces, w_ref, z, cols):
        (x_hi, x_lo), w_hi, w_lo = x_pieces, w_ref[0, z, :, cols], w_ref[1, z, :, cols]
        return _dot(x_hi, w_hi) + (_dot(x_lo, w_hi) + _dot(x_hi, w_lo))

    for j in range(bw // LANES):
        cols = slice(j * LANES, (j + 1) * LANES)
        r, k, v = (_token_shift(h_ref, hp_ref, hn_ref, conv_ref, slice(s * bw + j * LANES, s * bw + (j + 1) * LANES))
                   for s in range(3))
        kkr = k * kk_ref[:, cols]
        kk = kkr * lax.rsqrt(jnp.maximum(_head_sum(kkr * kkr, ones_bd), 1e-24))
        kts = []
        for z, (lw_o, kt_o, b_o) in enumerate(((lw0_o, kt0_o, b0_o), (lw1_o, kt1_o, b1_o))):
            w_raw = w0_ref[z:z + 1, cols] + up_proj(wd_pieces, wup_ref, z, cols)
            lw_o[0, :, cols] = -decay_scale * _sigmoid(w_raw)
            a = _sigmoid(a0_ref[z:z + 1, cols] + up_proj(ad_pieces, aup_ref, z, cols))
            kt = k * (1.0 + (a - 1.0) * ka_ref[:, cols])
            kt_o[0, :, cols] = kt
            b_o[0, :, cols] = kk * a
            kts.append(kt)
        r_o[0, :, cols] = r
        v_o[0, :, cols] = v
        kk_o[0, :, cols] = kk
        bonus_o[0, :, cols] = _head_sum(r * (kts[0] + kts[1]) * rk_ref[:, cols], ones_bd) * v


def _rwkv_prep(h_rkv, h_lora, conv, wup_pad, w0, aup_pad, a0, k_k, k_a, r_k, tt=256):
    bsz, t, w = h_rkv.shape
    wl = h_lora.shape[2]
    tt = min(tt, t)
    nt = t // tt
    hb = tt // BF16_SUBLANES
    n_halo = t // BF16_SUBLANES
    row = lambda a: a.reshape(1, -1)
    vec_spec = pl.BlockSpec((1, BRANCH_W), lambda b, i: (0, 0))
    out_spec = pl.BlockSpec((1, tt, BRANCH_W), lambda b, i: (b, i, 0))
    out_sds = jax.ShapeDtypeStruct((bsz, t, BRANCH_W), F32)
    tile = lambda width: [
        pl.BlockSpec((1, tt, width), lambda b, i: (b, i, 0)),
        pl.BlockSpec((1, BF16_SUBLANES, width), lambda b, i: (b, jnp.maximum(i * hb - 1, 0), 0)),
        pl.BlockSpec((1, BF16_SUBLANES, width), lambda b, i: (b, jnp.minimum((i + 1) * hb, n_halo - 1), 0))]
    return pl.pallas_call(
        _rwkv_prep_kernel,
        grid=(bsz, nt),
        in_specs=tile(w) + tile(wl) + [
            pl.BlockSpec((3, w), lambda b, i: (0, 0)),
            pl.BlockSpec((3, wl), lambda b, i: (0, 0)),
            pl.BlockSpec((2, 2, 2 * LORA, BRANCH_W), lambda b, i: (0, 0, 0, 0)),
            pl.BlockSpec((2, BRANCH_W), lambda b, i: (0, 0)),
            pl.BlockSpec((2, 2, 2 * LORA, BRANCH_W), lambda b, i: (0, 0, 0, 0)),
            pl.BlockSpec((2, BRANCH_W), lambda b, i: (0, 0)),
            vec_spec, vec_spec, vec_spec,
        ],
        out_specs=[out_spec] * 10,
        out_shape=[out_sds] * 10,
        compiler_params=_cparams("parallel", "parallel"),
        name="rwkv_prep",
    )(h_rkv, h_rkv, h_rkv, h_lora, h_lora, h_lora, conv[:, :w], conv[:, w:], wup_pad, w0, aup_pad, a0,
      row(k_k), row(k_a), row(r_k))


def _wkv_masks():
    c = WKV_CHUNK
    row = lax.broadcasted_iota(jnp.int32, (2 * c, 2 * c), 0)
    col = lax.broadcasted_iota(jnp.int32, (2 * c, 2 * c), 1)
    same = (row >> 6) == (col >> 6)
    rt, ct = row & (c - 1), col & (c - 1)
    f = lambda m: jnp.where(same & m, 1.0, 0.0).astype(F32)
    lane = lax.broadcasted_iota(jnp.int32, (1, LANES), 1)
    crow = lax.broadcasted_iota(jnp.int32, (c, c), 0)
    ccol = lax.broadcasted_iota(jnp.int32, (c, c), 1)
    return {
        "eye": f(rt == ct),
        "strict": (f(ct < rt), f(ct > rt)),
        "incl": (f(ct <= rt), f(ct >= rt)),
        "cum": (jnp.where(ccol <= crow, 1.0, 0.0).astype(BF16), jnp.where(ccol >= crow, 1.0, 0.0).astype(BF16)),
        "head0": jnp.where(lane < RWKV_HEAD, 1.0, 0.0).astype(F32),
        "head1": jnp.where(lane < RWKV_HEAD, 0.0, 1.0).astype(F32),
    }


def _wkv_local(problems, masks, fillers=()):
    c = WKV_CHUNK
    bf = lambda x: x.astype(BF16)
    m0, m1 = masks["head0"], masks["head1"]
    pair = lambda x: jnp.concatenate([x * m0, x * m1], axis=0)
    each = lambda fn, *lists: [fn(*xs) for xs in zip(*lists)]
    dirs = [p[6] for p in problems]
    fillers = list(fillers)
    n_points = 16
    stride = max(1, n_points // max(1, len(fillers)))
    seen = [0]

    def fill():
        seen[0] += 1
        if fillers and seen[0] % stride == 0:
            fillers.pop(0)()

    cl = [_cumsum_dot(masks["cum"][p[6]], p[0]) for p in problems]
    tot = [x[c - 1:c, :] if d == 0 else x[0:1, :] for x, d in zip(cl, dirs)]
    fill()
    zp = [pair(-p[4] * jnp.exp(x - p[0])) for p, x in zip(problems, cl)]
    rp = [pair(p[1] * jnp.exp(x)) for p, x in zip(problems, cl)]
    vpb = [bf(pair(p[3])) for p in problems]
    e_neg = [jnp.exp(-x) for x in cl]
    bk_start = [bf(jnp.concatenate([pair(p[5] * e), pair(p[2] * e)], axis=0)) for p, e in zip(problems, e_neg)]
    e_end = [jnp.exp(t - x) for t, x in zip(tot, cl)]
    bk_end = [bf(jnp.concatenate([pair(p[5] * e), pair(p[2] * e)], axis=0)) for p, e in zip(problems, e_end)]
    fill()

    scores = each(lambda z, r, bk: _dot_nt(bf(jnp.concatenate([z, r], axis=0)), bk), zp, rp, bk_start)
    fill()
    strict = [masks["strict"][d] for d in dirs]
    incl = [masks["incl"][d] for d in dirs]
    l_zb = each(lambda s, m: s[0:2 * c, 0:2 * c] * m, scores, strict)
    a_zk = each(lambda s, m: bf(s[0:2 * c, 2 * c:4 * c] * m), scores, strict)
    a_r = each(lambda s, m: bf(jnp.concatenate([s[2 * c:4 * c, 0:2 * c] * m, s[2 * c:4 * c, 2 * c:4 * c] * m],
                                               axis=1)), scores, incl)
    azk_v = each(_dot, a_zk, vpb)
    fill()

    inv = [masks["eye"] + l for l in l_zb]
    pb = [bf(l) for l in l_zb]
    for _ in range(5):
        pb = [bf(_dot(x, x)) for x in pb]
        fill()
        inv = each(lambda t, x: t + _dot(bf(t), x), inv, pb)
        fill()

    zu = each(lambda t, z, u: _dot(bf(t), bf(jnp.concatenate([z, u], axis=1))), inv, zp, azk_v)
    fill()
    stack = each(lambda x, v: jnp.concatenate([bf(x), jnp.concatenate([jnp.zeros_like(v), v], axis=1)], axis=0),
                 zu, vpb)
    ry = each(_dot, a_r, stack)
    fill()
    gh = each(_dot_tn, stack, bk_end)
    while fillers:
        fillers.pop(0)()
    out = []
    for r, y, g, t in zip(rp, ry, gh, tot):
        rb = r + y[:, 0:2 * c]
        out.append((rb[0:c] + rb[c:2 * c], y[0:c, 2 * c:4 * c] + y[c:2 * c, 2 * c:4 * c],
                    g[0:2 * c], g[2 * c:4 * c], jnp.exp(t)))
    return out


def _cumsum_dot(cum_bf16, lw):
    acc = None
    for piece in _split_bf16(lw, 2):
        d = _dot(cum_bf16, piece)
        acc = d if acc is None else acc + d
    return acc


def _wkv_kernel(r_f, v_f, kk_f, lw_f, kt_f, b_f, r_b, v_b, kk_b, lw_b, kt_b, b_b, yf_o, yb_o,
                s_ref, rb_ref, g_ref, h_ref, wc_ref):
    @pl.when(pl.program_id(2) == 0)
    def _():
        s_ref[...] = jnp.zeros_like(s_ref)

    c = WKV_CHUNK
    n_chunks = r_f.shape[1] // c
    half = (5 * n_chunks) // 8
    masks = _wkv_masks()
    ins = ((lw_f, r_f, kt_f, v_f, kk_f, b_f), (lw_b, r_b, kt_b, v_b, kk_b, b_b))
    outs = (yf_o, yb_o)
    streams = [(d, p) for d in range(2) for p in range(WKV_PAIRS)]
    states = [s_ref[q] for q in range(len(streams))]
    chunk_of = lambda d, step: step if d == 0 else n_chunks - 1 - step
    lanes = lambda p: slice(p * LANES, (p + 1) * LANES)

    def local_factors(steps, fillers):
        where = [(q, chunk_of(streams[q][0], s)) for s in steps for q in range(len(streams))]
        problems = [[ref[0, ci * c:(ci + 1) * c, lanes(streams[q][1])] for ref in ins[streams[q][0]]]
                    + [streams[q][0]] for q, ci in where]
        for (q, ci), (rb, yloc, g, h, wc) in zip(where, _wkv_local(problems, masks, fillers)):
            d, p = streams[q]
            rb_ref[q, ci * c:(ci + 1) * c, :] = rb.astype(BF16)
            outs[d][0, ci * c:(ci + 1) * c, lanes(p)] = yloc
            g_ref[q, ci] = g.astype(BF16)
            h_ref[q, ci] = h
            wc_ref[q, ci] = jnp.broadcast_to(wc, (8, LANES))

    def state_step(step):
        for q, (d, p) in enumerate(streams):
            ci = chunk_of(d, step)
            rows = slice(ci * c, (ci + 1) * c)
            sb = states[q].astype(BF16)
            outs[d][0, rows, lanes(p)] = outs[d][0, rows, lanes(p)] + _dot_nt(rb_ref[q, rows, :], sb)
            states[q] = states[q] * wc_ref[q, ci, 0:1, :] + _dot(sb, g_ref[q, ci]) + h_ref[q, ci]

    local_factors(range(0, half), ())
    local_factors(range(half, n_chunks), [functools.partial(state_step, s) for s in range(half)])
    for s in range(half, n_chunks):
        state_step(s)
    for q in range(len(streams)):
        s_ref[q] = states[q]


WKV_PAIRS = 2


def _wkv(r, v, kk, lw0, lw1, kt0, kt1, b0, b1, tb=512):
    bsz, t, w = r.shape
    tb = min(tb, t)
    nb = t // tb
    nc = tb // WKV_CHUNK
    wl = WKV_PAIRS * LANES
    ns = 2 * WKV_PAIRS
    fwd = pl.BlockSpec((1, tb, wl), lambda b, h, g: (b, g, h))
    bwd = pl.BlockSpec((1, tb, wl), lambda b, h, g: (b, nb - 1 - g, h))
    sds = jax.ShapeDtypeStruct((bsz, t, w), F32)
    return pl.pallas_call(
        _wkv_kernel,
        grid=(bsz, w // wl, nb),
        in_specs=[fwd] * 6 + [bwd] * 6,
        out_specs=[fwd, bwd],
        out_shape=[sds, sds],
        scratch_shapes=[pltpu.VMEM((ns, LANES, LANES), F32), pltpu.VMEM((ns, tb, LANES), BF16),
                        pltpu.VMEM((ns, nc, LANES, LANES), BF16), pltpu.VMEM((ns, nc, LANES, LANES), F32),
                        pltpu.VMEM((ns, nc, 8, LANES), F32)],
        compiler_params=_cparams("parallel", "parallel", "arbitrary"),
        name="wkv7_chunked",
    )(r, v, kk, lw0, kt0, b0, r, v, kk, lw1, kt1, b1)


def _rwkv_post_kernel(yf_ref, yb_ref, bonus_ref, g_ref, lw_ref, lb_ref, o_ref):
    ones_bd = _head_block_ones(RWKV_HEAD)
    inv_n = 1.0 / RWKV_HEAD
    for j in range(o_ref.shape[2] // LANES):
        cols = slice(j * LANES, (j + 1) * LANES)
        wkv = yf_ref[0, :, cols] + yb_ref[0, :, cols]
        mu = _head_sum(wkv, ones_bd) * inv_n
        d = wkv - mu
        var = _head_sum(d * d, ones_bd) * inv_n
        gn = d * lax.rsqrt(var + LNX_EPS) * lw_ref[:, cols] + lb_ref[:, cols]
        gate = _silu(g_ref[0, :, cols].astype(F32))
        o_ref[0, :, cols] = ((gn + bonus_ref[0, :, cols]) * gate).astype(o_ref.dtype)


def _rwkv_post(yf, yb, bonus, g, lnx_w, lnx_b, tt=512):
    bsz, t, w = yf.shape
    tt = min(tt, t)
    spec = pl.BlockSpec((1, tt, w), lambda b, i: (b, i, 0))
    vec = pl.BlockSpec((1, w), lambda b, i: (0, 0))
    return pl.pallas_call(
        _rwkv_post_kernel,
        grid=(bsz, t // tt),
        in_specs=[spec, spec, spec, spec, vec, vec],
        out_specs=spec,
        out_shape=jax.ShapeDtypeStruct((bsz, t, w), BF16),
        compiler_params=_cparams("parallel", "parallel"),
        name="rwkv_post",
    )(yf, yb, bonus, g, lnx_w.reshape(1, w), lnx_b.reshape(1, w))


def _sgu_kernel(h_ref, lg_ref, lb_ref, ws_ref, bs_ref, o_ref):
    bw = BRANCH_W
    tt = h_ref.shape[1]
    u = _gelu_tanh(h_ref[0, :, 0:bw].astype(F32))
    vv = _gelu_tanh(h_ref[0, :, bw:2 * bw].astype(F32))
    g = h_ref[0, :, 2 * bw:3 * bw].astype(F32)
    mu = jnp.mean(vv, axis=-1, keepdims=True)
    d = vv - mu
    var = jnp.mean(d * d, axis=-1, keepdims=True)
    vn = (d * lax.rsqrt(var + 1e-5) * lg_ref[...] + lb_ref[...]).astype(BF16)
    gate = u * _silu(g)
    for ck in range(tt // SG_CHUNK):
        rs = slice(ck * SG_CHUNK, (ck + 1) * SG_CHUNK)
        for grp in range(SG_GROUPS):
            cs = slice(grp * LANES, (grp + 1) * LANES)
            sv = _dot(ws_ref[grp], vn[rs, cs]) + bs_ref[:, cs]
            o_ref[0, rs, cs] = (gate[rs, cs] * sv).astype(o_ref.dtype)


def _sgu(h_b, ln_g, ln_b, w_s, bs_cols, tt=512):
    bsz, t, w3 = h_b.shape
    bw = BRANCH_W
    tt = min(tt, t)
    vec = pl.BlockSpec((1, bw), lambda b, i: (0, 0))
    return pl.pallas_call(
        _sgu_kernel,
        grid=(bsz, t // tt),
        in_specs=[pl.BlockSpec((1, tt, w3), lambda b, i: (b, i, 0)), vec, vec,
                  pl.BlockSpec((SG_GROUPS, SG_CHUNK, SG_CHUNK), lambda b, i: (0, 0, 0)),
                  pl.BlockSpec((SG_CHUNK, bw), lambda b, i: (0, 0))],
        out_specs=pl.BlockSpec((1, tt, bw), lambda b, i: (b, i, 0)),
        out_shape=jax.ShapeDtypeStruct((bsz, t, bw), BF16),
        compiler_params=_cparams("parallel", "parallel"),
        name="spatial_gating",
    )(h_b, ln_g.reshape(1, bw), ln_b.reshape(1, bw), w_s, bs_cols)


def _natten_bias_table(rpb):
    p = np.arange(GRID_W)[:, None]
    m = np.arange(GRID_W)[None, :]
    sj = np.clip(p - NA_COLS // 2, 0, GRID_W - NA_COLS)
    valid = (m >= sj) & (m < sj + NA_COLS)
    dc = np.clip(m - p, -(NA_COLS - 1), NA_COLS - 1) + NA_COLS - 1
    by_rel = jnp.where(valid[None, None], rpb[:, :, dc], NEG_INF)
    tab = jnp.stack([by_rel[:, s:s + NA_ROWS] for s in range(NA_ROWS)], axis=1)
    tab = tab.transpose(0, 1, 3, 2, 4)
    return tab.reshape(rpb.shape[0], NA_ROWS, GRID_W, NA_ROWS * GRID_W).astype(F32)


NA_UNROLL = 16


def _natten_kernel(q_ref, k_ref, v_ref, g_ref, qn_ref, kn_ref, bias_ref, o_ref, qs_ref, ks_ref):
    t = q_ref.shape[1]
    n_rows = t // GRID_W
    win = NA_ROWS * GRID_W
    ones_bd = _head_block_ones(NA_HEAD)
    lane = lax.broadcasted_iota(jnp.int32, (1, LANES), 1)
    m0 = lane < NA_HEAD

    def norm(x_ref, gain_ref, scale):
        x = x_ref[0].astype(F32)
        ms = _head_sum(x * x, ones_bd) * (1.0 / NA_HEAD)
        return x * lax.rsqrt(ms + 1e-6) * (gain_ref[...] * scale)

    qs_ref[...] = norm(q_ref, qn_ref, NA_HEAD ** -0.5).astype(BF16)
    ks_ref[...] = norm(k_ref, kn_ref, 1.0).astype(BF16)

    def body(it, carry):
        rows = [it * NA_UNROLL + j for j in range(NA_UNROLL)]
        si = [jnp.clip(i - NA_ROWS // 2, 0, n_rows - NA_ROWS) for i in rows]
        start = [s - i + (NA_ROWS - 1) for s, i in zip(si, rows)]
        qo = [pl.multiple_of(i * GRID_W, GRID_W) for i in rows]
        ko = [pl.multiple_of(s * GRID_W, GRID_W) for s in si]
        q = [qs_ref[pl.ds(o, GRID_W), :] for o in qo]
        zero = jnp.zeros_like(q[0])
        q2 = [jnp.concatenate([jnp.where(m0, x, zero), jnp.where(m0, zero, x)], axis=0) for x in q]
        s = [_dot_nt(x, ks_ref[pl.ds(o, win), :]) for x, o in zip(q2, ko)]
        s = [x + jnp.concatenate([bias_ref[0, st], bias_ref[1, st]], axis=0) for x, st in zip(s, start)]
        e = [jnp.exp(x - jnp.max(x, axis=-1, keepdims=True)).astype(BF16) for x in s]
        o2 = [_dot(x, jnp.concatenate([v_ref[0, pl.ds(o, win), :], ones_blk], axis=1))
              for x, o in zip(e, ko)]
        for x, o in zip(o2, qo):
            x = x[:, 0:LANES] / x[:, LANES:2 * LANES]
            g = g_ref[0, pl.ds(o, GRID_W), :].astype(F32)
            val = jnp.where(m0, x[0:GRID_W], x[GRID_W:2 * GRID_W]) * _silu(g)
            o_ref[0, pl.ds(o, GRID_W), :] = val.astype(o_ref.dtype)
        return carry

    ones_blk = jnp.ones((win, LANES), BF16)

    lax.fori_loop(0, n_rows // NA_UNROLL, body, 0)


def _natten(h_c, q_norm, k_norm, bias_tab):
    bsz, t, w4 = h_c.shape
    bw = BRANCH_W
    nlb = bw // LANES
    sec = lambda s: pl.BlockSpec((1, t, LANES), functools.partial(lambda b, hp, s: (b, 0, s * nlb + hp), s=s))
    two = lambda a: jnp.concatenate([a, a]).reshape(1, LANES)
    return pl.pallas_call(
        _natten_kernel,
        grid=(bsz, nlb),
        in_specs=[sec(0), sec(1), sec(2), sec(3),
                  pl.BlockSpec((1, LANES), lambda b, hp: (0, 0)), pl.BlockSpec((1, LANES), lambda b, hp: (0, 0)),
                  pl.BlockSpec((2, NA_ROWS, GRID_W, NA_ROWS * GRID_W), lambda b, hp: (hp, 0, 0, 0))],
        out_specs=pl.BlockSpec((1, t, LANES), lambda b, hp: (b, 0, hp)),
        out_shape=jax.ShapeDtypeStruct((bsz, t, bw), BF16),
        scratch_shapes=[pltpu.VMEM((t, LANES), BF16), pltpu.VMEM((t, LANES), BF16)],
        compiler_params=_cparams("parallel", "parallel"),
        name="neighbourhood_attention",
    )(h_c, h_c, h_c, h_c, two(q_norm), two(k_norm), bias_tab)


def _memattn_kernel(qg_ref, kv_ref, qn_ref, kn_ref, o_ref):
    hd, bw = MEM_HEAD, BRANCH_W

    def norm(x, gain, scale):
        ms = jnp.mean(x * x, axis=-1, keepdims=True)
        return (x * lax.rsqrt(ms + 1e-6) * (gain * scale)).astype(BF16)

    heads = [slice(h * hd, (h + 1) * hd) for h in range(MEM_HEADS)]
    q = [norm(qg_ref[0, :, c].astype(F32), qn_ref[...], hd ** -0.5) for c in heads]
    k = [norm(kv_ref[0, :, c].astype(F32), kn_ref[...], 1.0) for c in heads]
    s = [_dot_nt(a, b) for a, b in zip(q, k)]
    e = [jnp.exp(x - jnp.max(x, axis=-1, keepdims=True)) for x in s]
    p = [(x / jnp.sum(x, axis=-1, keepdims=True)).astype(BF16) for x in e]
    o = [_dot(x, kv_ref[0, :, bw + h * hd:bw + (h + 1) * hd]) for h, x in enumerate(p)]
    for c, x in zip(heads, o):
        g = qg_ref[0, :, bw + c.start:bw + c.stop].astype(F32)
        o_ref[0, :, c] = (x * _silu(g)).astype(o_ref.dtype)


def _memattn(h_m, kv, q_norm, k_norm, tt=512):
    bsz, t, w2 = h_m.shape
    mlen = kv.shape[1]
    tt = min(tt, t)
    vec = pl.BlockSpec((1, MEM_HEAD), lambda b, i: (0, 0))
    return pl.pallas_call(
        _memattn_kernel,
        grid=(bsz, t // tt),
        in_specs=[pl.BlockSpec((1, tt, w2), lambda b, i: (b, i, 0)),
                  pl.BlockSpec((1, mlen, w2), lambda b, i: (b, 0, 0)),
                  vec, vec],
        out_specs=pl.BlockSpec((1, tt, BRANCH_W), lambda b, i: (b, i, 0)),
        out_shape=jax.ShapeDtypeStruct((bsz, t, BRANCH_W), BF16),
        compiler_params=_cparams("parallel", "parallel"),
        name="memory_attention",
    )(h_m, kv, q_norm.reshape(1, MEM_HEAD), k_norm.reshape(1, MEM_HEAD))


def _pad_lora(up):
    z = jnp.zeros_like(up[0])
    w = jnp.stack([jnp.concatenate([up[0], z], axis=0), jnp.concatenate([z, up[1]], axis=0)])
    hi = w.astype(BF16)
    return jnp.stack([hi, (w - hi.astype(F32)).astype(BF16)])


def _layer(x2d, mem2d, bsz, layer, p, big, w_in_bf16, w_in_f32):
    m, d = x2d.shape
    t = m // bsz
    bw = BRANCH_W
    a_w = A_SHIFT_W + bw
    o1, o2, o3, o4 = a_w, a_w + 3 * bw, a_w + 7 * bw, a_w + 9 * bw
    xn = _rmsnorm(x2d, p["norm_g"], BF16)
    proj = functools.partial(_matmul, xn, w_in_bf16, 0)
    proj_cast = functools.partial(_matmul_and_cast, xn, w_in_bf16, 0)
    n_layers = w_in_f32.shape[0]

    h_rkv = proj(0, 3 * bw, BF16, 1024, 1024, name="proj_a_rkv")
    h_lora = proj(3 * bw, 4 * LORA, BF16, 1024, 4 * LORA, name="proj_a_lora")
    h_ag, w_kv = proj_cast(A_SHIFT_W, bw, BF16, 1024, 1024, big["m_w_kv"], layer, name="proj_a_gate_cast_kv")
    h_b = proj(o1, 3 * bw, BF16, 1024, 1024, name="proj_b")
    h_c, w_br = proj_cast(o2, 4 * bw, BF16, 1024, 1024, big["w_branch"].reshape(n_layers, N_BRANCH * bw, d),
                          layer, name="proj_c_cast_branch")
    h_m, w_o = proj_cast(o3, 2 * bw, BF16, 1024, 1024, big["w_out"], layer, name="proj_m_cast_out")
    if layer + 1 < n_layers:
        gates, w_next = _matmul_and_cast(xn, w_in_bf16, 0, o4, 4 * d, BF16, 1024, 1024, w_in_f32, layer + 1,
                                         act="sigmoid", name="proj_gates_cast_next")
    else:
        gates, w_next = proj(o4, 4 * d, BF16, 1024, 1024, act="sigmoid", name="proj_gates"), None

    prep = _rwkv_prep(h_rkv.reshape(bsz, t, 3 * bw), h_lora.reshape(bsz, t, 4 * LORA), p["a_conv"],
                      _pad_lora(p["a_w_up"]), p["a_w0"], _pad_lora(p["a_a_up"]), p["a_a0"],
                      p["a_k_k"], p["a_k_a"], p["a_r_k"].reshape(-1))
    r, v, kk, lw0, lw1, kt0, kt1, b0, b1, bonus = prep
    yf, yb = _wkv(r, v, kk, lw0, lw1, kt0, kt1, b0, b1)
    y_a = _rwkv_post(yf, yb, bonus, h_ag.reshape(bsz, t, bw), p["a_lnx_w"], p["a_lnx_b"])

    bs_cols = jnp.repeat(p["b_b_s"].T, SG_CHUNK, axis=1)
    y_b = _sgu(h_b.reshape(bsz, t, 3 * bw), p["b_ln_g"], p["b_ln_b"], p["b_w_s"].astype(BF16), bs_cols)

    y_c = _natten(h_c.reshape(bsz, t, 4 * bw), p["c_q_norm"], p["c_k_norm"], _natten_bias_table(p["c_rpb"]))

    mem_n = _rmsnorm(mem2d, p["m_norm_g"], BF16)
    kv = _matmul(mem_n, w_kv, 0, 0, 2 * bw, BF16, 1024, 1024, name="proj_mem_kv")
    y_d = _memattn(h_m.reshape(bsz, t, 2 * bw), kv.reshape(bsz, -1, 2 * bw), p["m_q_norm"], p["m_k_norm"])

    ys = [y.reshape(m, bw) for y in (y_a, y_b, y_c, y_d)]
    merged = _merge(ys, w_br.reshape(1, N_BRANCH, bw, d), 0, gates, 512, 1024)
    return _matmul_residual(merged, w_o, 0, x2d, 512, 1024), w_next


def kernel(x, mem, norm_g, w_in, a_conv, a_w_up, a_w0, a_a_up, a_a0, a_k_k, a_k_a, a_r_k, a_lnx_w, a_lnx_b,
           b_ln_g, b_ln_b, b_w_s, b_b_s, c_q_norm, c_k_norm, c_rpb, m_norm_g, m_w_kv, m_q_norm, m_k_norm,
           w_branch, w_out):
    params = dict(norm_g=norm_g, a_conv=a_conv, a_w_up=a_w_up, a_w0=a_w0, a_a_up=a_a_up, a_a0=a_a0,
                  a_k_k=a_k_k, a_k_a=a_k_a, a_r_k=a_r_k, a_lnx_w=a_lnx_w, a_lnx_b=a_lnx_b, b_ln_g=b_ln_g,
                  b_ln_b=b_ln_b, b_w_s=b_w_s, b_b_s=b_b_s, c_q_norm=c_q_norm, c_k_norm=c_k_norm, c_rpb=c_rpb,
                  m_norm_g=m_norm_g, m_q_norm=m_q_norm, m_k_norm=m_k_norm)
    big = dict(m_w_kv=m_w_kv, w_branch=w_branch, w_out=w_out)
    bsz, t, d = x.shape
    x2d = x.reshape(bsz * t, d)
    mem2d = mem.reshape(-1, d)
    w_cur = w_in[0:1].astype(BF16)
    for l in range(norm_g.shape[0]):
        x2d, w_cur = _layer(x2d, mem2d, bsz, l, {k: v[l] for k, v in params.items()}, big, w_cur, w_in)
    return x2d.reshape(bsz, t, d)
```

```python
import functools

import numpy as np
import jax
import jax.numpy as jnp
from jax import lax
from jax.experimental import pallas as pl
from jax.experimental.pallas import tpu as pltpu

F32 = jnp.float32
BF16 = jnp.bfloat16

VMEM_LIMIT_BYTES = 56 * 1024 * 1024
LANES = 128
BF16_SUBLANES = 16
F32_SUBLANES = 8

BRANCH_W = 1024
N_BRANCH = 4
RWKV_HEAD = 64
LORA = 64
A_SHIFT_W = 3 * BRANCH_W + 4 * LORA
LNX_EPS = 64e-5
SG_CHUNK = 128
SG_GROUPS = 8
NA_HEAD = 64
NA_ROWS = 8
NA_COLS = 16
GRID_W = 64
MEM_HEADS = 4
MEM_HEAD = 256
WKV_CHUNK = 64
NEG_INF = -1e30


def _cparams(*sem):
    return pltpu.CompilerParams(dimension_semantics=sem, vmem_limit_bytes=VMEM_LIMIT_BYTES)


def _sigmoid(x):
    return 0.5 * jnp.tanh(0.5 * x) + 0.5


def _silu(x):
    return x * _sigmoid(x)


def _gelu_tanh(x):
    c = np.float32(np.sqrt(2.0 / np.pi))
    half = 0.5 * x
    return half + half * jnp.tanh(x * (c + np.float32(c * 0.044715) * (x * x)))


def _dot(a, b, precision=None):
    return jnp.dot(a, b, preferred_element_type=F32, precision=precision)


def _dot_nt(a, b, precision=None):
    return lax.dot_general(a, b, (((1,), (1,)), ((), ())), preferred_element_type=F32,
                           precision=precision)


def _dot_tn(a, b, precision=None):
    return lax.dot_general(a, b, (((0,), (0,)), ((), ())), preferred_element_type=F32,
                           precision=precision)


def _split_bf16(x, terms):
    parts = []
    for _ in range(terms - 1):
        hi = x.astype(BF16)
        parts.append(hi)
        x = x - hi.astype(F32)
    parts.append(x.astype(BF16))
    return parts


def _dot_exact_rhs(x, w_bf16, terms):
    acc = None
    for piece in _split_bf16(x, terms):
        d = _dot(piece, w_bf16)
        acc = d if acc is None else acc + d
    return acc


def _head_block_ones(head):
    shift = int(np.log2(head))
    r = lax.broadcasted_iota(jnp.int32, (LANES, LANES), 0) >> shift
    c = lax.broadcasted_iota(jnp.int32, (LANES, LANES), 1) >> shift
    return jnp.where(r == c, 1.0, 0.0).astype(BF16)


def _head_sum(x, ones_bd, terms=1):
    parts = [_dot_exact_rhs(x[:, j:j + LANES], ones_bd, terms) for j in range(0, x.shape[1], LANES)]
    return parts[0] if len(parts) == 1 else jnp.concatenate(parts, axis=1)


def _rmsnorm_kernel(x_ref, g_ref, o_ref, *, eps):
    d = x_ref.shape[1]
    chunks = [slice(c, c + NORM_CHUNK) for c in range(0, d, NORM_CHUNK)]
    ssq = None
    for cols in chunks:
        x = x_ref[:, cols].astype(F32)
        s = jnp.sum(x * x, axis=-1, keepdims=True)
        ssq = s if ssq is None else ssq + s
    scale = lax.rsqrt(ssq * (1.0 / d) + eps)
    for cols in chunks:
        o_ref[:, cols] = (x_ref[:, cols].astype(F32) * scale * g_ref[:, cols]).astype(o_ref.dtype)


NORM_CHUNK = 512


def _rmsnorm(x2d, g, out_dtype, tm=512, eps=1e-6):
    m, d = x2d.shape
    tm = min(tm, m)
    return pl.pallas_call(
        functools.partial(_rmsnorm_kernel, eps=eps),
        grid=(m // tm,),
        in_specs=[pl.BlockSpec((tm, d), lambda i: (i, 0)), pl.BlockSpec((1, d), lambda i: (0, 0))],
        out_specs=pl.BlockSpec((tm, d), lambda i: (i, 0)),
        out_shape=jax.ShapeDtypeStruct((m, d), out_dtype),
        compiler_params=_cparams("parallel"),
        name="rmsnorm",
    )(x2d, g.reshape(1, d))


def _mm_kernel(a_ref, b_ref, o_ref, *, act):
    acc = _dot(a_ref[...], b_ref[0])
    if act == "sigmoid":
        acc = _sigmoid(acc)
    o_ref[...] = acc.astype(o_ref.dtype)


def _matmul(a, w, layer, col0, n, out_dtype, tm, tn, act=None, name="matmul"):
    m, k = a.shape
    tm, tn = min(tm, m), min(tn, n)
    return pl.pallas_call(
        functools.partial(_mm_kernel, act=act),
        grid=(m // tm, n // tn),
        in_specs=[pl.BlockSpec((tm, k), lambda i, j: (i, 0)),
                  pl.BlockSpec((pl.Element(1), pl.Element(k), pl.Element(tn)),
                               lambda i, j: (layer, 0, pl.multiple_of(col0 + j * tn, LANES)))],
        out_specs=pl.BlockSpec((tm, tn), lambda i, j: (i, j)),
        out_shape=jax.ShapeDtypeStruct((m, n), out_dtype),
        compiler_params=_cparams("parallel", "parallel"),
        name=name,
    )(a, w)


def _mm_cast_kernel(a_ref, b_ref, src_ref, o_ref, dst_ref, *, act):
    _mm_kernel(a_ref, b_ref, o_ref, act=act)
    dst_ref[0] = src_ref[...].astype(dst_ref.dtype)


def _matmul_and_cast(a, w, layer, col0, n, out_dtype, tm, tn, src, src_layer, act=None, name="matmul_cast"):
    m, k = a.shape
    tm, tn = min(tm, m), min(tn, n)
    ni, nj = m // tm, n // tn
    rows, cols = src.shape[1] // (ni * nj), src.shape[2]
    assert rows * ni * nj == src.shape[1] and rows % BF16_SUBLANES == 0
    return pl.pallas_call(
        functools.partial(_mm_cast_kernel, act=act),
        grid=(ni, nj),
        in_specs=[pl.BlockSpec((tm, k), lambda i, j: (i, 0)),
                  pl.BlockSpec((pl.Element(1), pl.Element(k), pl.Element(tn)),
                               lambda i, j: (layer, 0, pl.multiple_of(col0 + j * tn, LANES))),
                  pl.BlockSpec((None, rows, cols), lambda i, j: (src_layer, i * nj + j, 0))],
        out_specs=[pl.BlockSpec((tm, tn), lambda i, j: (i, j)),
                   pl.BlockSpec((1, rows, cols), lambda i, j: (0, i * nj + j, 0))],
        out_shape=[jax.ShapeDtypeStruct((m, n), out_dtype),
                   jax.ShapeDtypeStruct((1, src.shape[1], cols), BF16)],
        compiler_params=_cparams("parallel", "parallel"),
        name=name,
    )(a, w, src)


def _mm_residual_kernel(a_ref, b_ref, x_ref, o_ref):
    o_ref[...] = x_ref[...] + _dot(a_ref[...], b_ref[...])


def _matmul_residual(a, w, layer, x, tm, tn):
    m, k = a.shape
    n = w.shape[2]
    tm, tn = min(tm, m), min(tn, n)
    return pl.pallas_call(
        _mm_residual_kernel,
        grid=(n // tn, m // tm),
        in_specs=[pl.BlockSpec((tm, k), lambda j, i: (i, 0)),
                  pl.BlockSpec((None, k, tn), lambda j, i: (layer, 0, j)),
                  pl.BlockSpec((tm, tn), lambda j, i: (i, j))],
        out_specs=pl.BlockSpec((tm, tn), lambda j, i: (i, j)),
        out_shape=jax.ShapeDtypeStruct((m, n), x.dtype),
        compiler_params=_cparams("parallel", "parallel"),
        name="out_proj_residual",
    )(a, w, x)


def _merge_kernel(y0, y1, y2, y3, wb_ref, g0, g1, g2, g3, o_ref):
    acc = None
    for n, (y, g) in enumerate(((y0, g0), (y1, g1), (y2, g2), (y3, g3))):
        term = g[...].astype(F32) * _dot(y[...], wb_ref[n])
        acc = term if acc is None else acc + term
    o_ref[...] = acc.astype(o_ref.dtype)


def _merge(ys, wb, layer, gates, tm, tn):
    m, bw = ys[0].shape
    d = wb.shape[3]
    tm, tn = min(tm, m), min(tn, d)
    nj = d // tn
    y_spec = pl.BlockSpec((tm, bw), lambda j, i: (i, 0))
    g_specs = [pl.BlockSpec((tm, tn), functools.partial(lambda j, i, n: (i, n * nj + j), n=n))
               for n in range(4)]
    return pl.pallas_call(
        _merge_kernel,
        grid=(nj, m // tm),
        in_specs=[y_spec] * 4 + [pl.BlockSpec((None, 4, bw, tn), lambda j, i: (layer, 0, 0, j))] + g_specs,
        out_specs=pl.BlockSpec((tm, tn), lambda j, i: (i, j)),
        out_shape=jax.ShapeDtypeStruct((m, d), BF16),
        compiler_params=_cparams("parallel", "parallel"),
        name="gated_merge",
    )(*ys, wb, gates, gates, gates, gates)


def _token_shift(h_ref, hp_ref, hn_ref, conv_ref, cols):
    i = pl.program_id(1)
    nt = pl.num_programs(1)
    h = h_ref[0, :, cols].astype(F32)
    tt = h.shape[0]
    prev_row = hp_ref[0, BF16_SUBLANES - 1:BF16_SUBLANES, cols].astype(F32) * (i > 0).astype(F32)
    next_row = hn_ref[0, 0:1, cols].astype(F32) * (i < nt - 1).astype(F32)
    rows = lax.broadcasted_iota(jnp.int32, (F32_SUBLANES, 1), 0)
    dn, up = pltpu.roll(h, 1, 0), pltpu.roll(h, tt - 1, 0)
    h_dn = jnp.concatenate([jnp.where(rows == 0, prev_row, dn[0:F32_SUBLANES]), dn[F32_SUBLANES:]], axis=0)
    h_up = jnp.concatenate([up[0:tt - F32_SUBLANES],
                            jnp.where(rows == F32_SUBLANES - 1, next_row, up[tt - F32_SUBLANES:])], axis=0)
    return h_dn * conv_ref[0:1, cols] + h * conv_ref[1:2, cols] + h_up * conv_ref[2:3, cols]


def _rwkv_prep_kernel(h_ref, hp_ref, hn_ref, l_ref, lp_ref, ln_ref, conv_ref, convl_ref, wup_ref, w0_ref,
                      aup_ref, a0_ref, kk_ref, ka_ref, rk_ref,
                      r_o, v_o, kk_o, lw0_o, lw1_o, kt0_o, kt1_o, b0_o, b1_o, bonus_o):
    bw = BRANCH_W
    ls = _token_shift(l_ref, lp_ref, ln_ref, convl_ref, slice(0, 4 * LORA))
    wd_pieces = _split_bf16(jnp.tanh(ls[:, 0:2 * LORA]), 2)
    ad_pieces = _split_bf16(ls[:, 2 * LORA:4 * LORA], 2)
    ones_bd = _head_block_ones(RWKV_HEAD)
    decay_scale = np.float32(np.exp(-0.5))

    def up_proj(x_pieces, w_ref, z, cols):
        (x_hi, x_lo), w_hi, w_lo = x_pieces, w_ref[0, z, :, cols], w_ref[1, z, :, cols]
        return _dot(x_hi, w_hi) + (_dot(x_lo, w_hi) + _dot(x_hi, w_lo))

    for j in range(bw // LANES):
        cols = slice(j * LANES, (j + 1) * LANES)
        r, k, v = (_token_shift(h_ref, hp_ref, hn_ref, conv_ref, slice(s * bw + j * LANES, s * bw + (j + 1) * LANES))
                   for s in range(3))
        kkr = k * kk_ref[:, cols]
        kk = kkr * lax.rsqrt(jnp.maximum(_head_sum(kkr * kkr, ones_bd), 1e-24))
        kts = []
        for z, (lw_o, kt_o, b_o) in enumerate(((lw0_o, kt0_o, b0_o), (lw1_o, kt1_o, b1_o))):
            w_raw = w0_ref[z:z + 1, cols] + up_proj(wd_pieces, wup_ref, z, cols)
            lw_o[0, :, cols] = -decay_scale * _sigmoid(w_raw)
            a = _sigmoid(a0_ref[z:z + 1, cols] + up_proj(ad_pieces, aup_ref, z, cols))
            kt = k * (1.0 + (a - 1.0) * ka_ref[:, cols])
            kt_o[0, :, cols] = kt
            b_o[0, :, cols] = kk * a
            kts.append(kt)
        r_o[0, :, cols] = r
        v_o[0, :, cols] = v
        kk_o[0, :, cols] = kk
        bonus_o[0, :, cols] = _head_sum(r * (kts[0] + kts[1]) * rk_ref[:, cols], ones_bd) * v


def _rwkv_prep(h_rkv, h_lora, conv, wup_pad, w0, aup_pad, a0, k_k, k_a, r_k, tt=256):
    bsz, t, w = h_rkv.shape
    wl = h_lora.shape[2]
    tt = min(tt, t)
    nt = t // tt
    hb = tt // BF16_SUBLANES
    n_halo = t // BF16_SUBLANES
    row = lambda a: a.reshape(1, -1)
    vec_spec = pl.BlockSpec((1, BRANCH_W), lambda b, i: (0, 0))
    out_spec = pl.BlockSpec((1, tt, BRANCH_W), lambda b, i: (b, i, 0))
    out_sds = jax.ShapeDtypeStruct((bsz, t, BRANCH_W), F32)
    tile = lambda width: [
        pl.BlockSpec((1, tt, width), lambda b, i: (b, i, 0)),
        pl.BlockSpec((1, BF16_SUBLANES, width), lambda b, i: (b, jnp.maximum(i * hb - 1, 0), 0)),
        pl.BlockSpec((1, BF16_SUBLANES, width), lambda b, i: (b, jnp.minimum((i + 1) * hb, n_halo - 1), 0))]
    return pl.pallas_call(
        _rwkv_prep_kernel,
        grid=(bsz, nt),
        in_specs=tile(w) + tile(wl) + [
            pl.BlockSpec((3, w), lambda b, i: (0, 0)),
            pl.BlockSpec((3, wl), lambda b, i: (0, 0)),
            pl.BlockSpec((2, 2, 2 * LORA, BRANCH_W), lambda b, i: (0, 0, 0, 0)),
            pl.BlockSpec((2, BRANCH_W), lambda b, i: (0, 0)),
            pl.BlockSpec((2, 2, 2 * LORA, BRANCH_W), lambda b, i: (0, 0, 0, 0)),
            pl.BlockSpec((2, BRANCH_W), lambda b, i: (0, 0)),
            vec_spec, vec_spec, vec_spec,
        ],
        out_specs=[out_spec] * 10,
        out_shape=[out_sds] * 10,
        compiler_params=_cparams("parallel", "parallel"),
        name="rwkv_prep",
    )(h_rkv, h_rkv, h_rkv, h_lora, h_lora, h_lora, conv[:, :w], conv[:, w:], wup_pad, w0, aup_pad, a0,
      row(k_k), row(k_a), row(r_k))


def _wkv_masks():
    c = WKV_CHUNK
    row = lax.broadcasted_iota(jnp.int32, (2 * c, 2 * c), 0)
    col = lax.broadcasted_iota(jnp.int32, (2 * c, 2 * c), 1)
    same = (row >> 6) == (col >> 6)
    rt, ct = row & (c - 1), col & (c - 1)
    f = lambda m: jnp.where(same & m, 1.0, 0.0).astype(F32)
    lane = lax.broadcasted_iota(jnp.int32, (1, LANES), 1)
    crow = lax.broadcasted_iota(jnp.int32, (c, c), 0)
    ccol = lax.broadcasted_iota(jnp.int32, (c, c), 1)
    return {
        "eye": f(rt == ct),
        "strict": (f(ct < rt), f(ct > rt)),
        "incl": (f(ct <= rt), f(ct >= rt)),
        "cum": (jnp.where(ccol <= crow, 1.0, 0.0).astype(BF16), jnp.where(ccol >= crow, 1.0, 0.0).astype(BF16)),
        "head0": jnp.where(lane < RWKV_HEAD, 1.0, 0.0).astype(F32),
        "head1": jnp.where(lane < RWKV_HEAD, 0.0, 1.0).astype(F32),
    }


def _wkv_local(problems, masks, fillers=()):
    c = WKV_CHUNK
    bf = lambda x: x.astype(BF16)
    m0, m1 = masks["head0"], masks["head1"]
    pair = lambda x: jnp.concatenate([x * m0, x * m1], axis=0)
    each = lambda fn, *lists: [fn(*xs) for xs in zip(*lists)]
    dirs = [p[6] for p in problems]
    fillers = list(fillers)
    n_points = 16
    stride = max(1, n_points // max(1, len(fillers)))
    seen = [0]

    def fill():
        seen[0] += 1
        if fillers and seen[0] % stride == 0:
            fillers.pop(0)()

    cl = [_cumsum_dot(masks["cum"][p[6]], p[0]) for p in problems]
    tot = [x[c - 1:c, :] if d == 0 else x[0:1, :] for x, d in zip(cl, dirs)]
    fill()
    zp = [pair(-p[4] * jnp.exp(x - p[0])) for p, x in zip(problems, cl)]
    rp = [pair(p[1] * jnp.exp(x)) for p, x in zip(problems, cl)]
    vpb = [bf(pair(p[3])) for p in problems]
    e_neg = [jnp.exp(-x) for x in cl]
    bk_start = [bf(jnp.concatenate([pair(p[5] * e), pair(p[2] * e)], axis=0)) for p, e in zip(problems, e_neg)]
    e_end = [jnp.exp(t - x) for t, x in zip(tot, cl)]
    bk_end = [bf(jnp.concatenate([pair(p[5] * e), pair(p[2] * e)], axis=0)) for p, e in zip(problems, e_end)]
    fill()

    scores = each(lambda z, r, bk: _dot_nt(bf(jnp.concatenate([z, r], axis=0)), bk), zp, rp, bk_start)
    fill()
    strict = [masks["strict"][d] for d in dirs]
    incl = [masks["incl"][d] for d in dirs]
    l_zb = each(lambda s, m: s[0:2 * c, 0:2 * c] * m, scores, strict)
    a_zk = each(lambda s, m: bf(s[0:2 * c, 2 * c:4 * c] * m), scores, strict)
    a_r = each(lambda s, m: bf(jnp.concatenate([s[2 * c:4 * c, 0:2 * c] * m, s[2 * c:4 * c, 2 * c:4 * c] * m],
                                               axis=1)), scores, incl)
    azk_v = each(_dot, a_zk, vpb)
    fill()

    inv = [masks["eye"] + l for l in l_zb]
    pb = [bf(l) for l in l_zb]
    for _ in range(5):
        pb = [bf(_dot(x, x)) for x in pb]
        fill()
        inv = each(lambda t, x: t + _dot(bf(t), x), inv, pb)
        fill()

    zu = each(lambda t, z, u: _dot(bf(t), bf(jnp.concatenate([z, u], axis=1))), inv, zp, azk_v)
    fill()
    stack = each(lambda x, v: jnp.concatenate([bf(x), jnp.concatenate([jnp.zeros_like(v), v], axis=1)], axis=0),
                 zu, vpb)
    ry = each(_dot, a_r, stack)
    fill()
    gh = each(_dot_tn, stack, bk_end)
    while fillers:
        fillers.pop(0)()
    out = []
    for r, y, g, t in zip(rp, ry, gh, tot):
        rb = r + y[:, 0:2 * c]
        out.append((rb[0:c] + rb[c:2 * c], y[0:c, 2 * c:4 * c] + y[c:2 * c, 2 * c:4 * c],
                    g[0:2 * c], g[2 * c:4 * c], jnp.exp(t)))
    return out


def _cumsum_dot(cum_bf16, lw):
    acc = None
    for piece in _split_bf16(lw, 2):
        d = _dot(cum_bf16, piece)
        acc = d if acc is None else acc + d
    return acc


def _wkv_kernel(r_f, v_f, kk_f, lw_f, kt_f, b_f, r_b, v_b, kk_b, lw_b, kt_b, b_b, yf_o, yb_o,
                s_ref, rb_ref, g_ref, h_ref, wc_ref):
    @pl.when(pl.program_id(2) == 0)
    def _():
        s_ref[...] = jnp.zeros_like(s_ref)

    c = WKV_CHUNK
    n_chunks = r_f.shape[1] // c
    half = n_chunks // 2
    masks = _wkv_masks()
    ins = ((lw_f, r_f, kt_f, v_f, kk_f, b_f), (lw_b, r_b, kt_b, v_b, kk_b, b_b))
    outs = (yf_o, yb_o)
    streams = [(d, p) for d in range(2) for p in range(WKV_PAIRS)]
    states = [s_ref[q] for q in range(len(streams))]
    chunk_of = lambda d, step: step if d == 0 else n_chunks - 1 - step
    lanes = lambda p: slice(p * LANES, (p + 1) * LANES)

    def local_factors(steps, fillers):
        where = [(q, chunk_of(streams[q][0], s)) for s in steps for q in range(len(streams))]
        problems = [[ref[0, ci * c:(ci + 1) * c, lanes(streams[q][1])] for ref in ins[streams[q][0]]]
                    + [streams[q][0]] for q, ci in where]
        for (q, ci), (rb, yloc, g, h, wc) in zip(where, _wkv_local(problems, masks, fillers)):
            d, p = streams[q]
            rb_ref[q, ci * c:(ci + 1) * c, :] = rb.astype(BF16)
            outs[d][0, ci * c:(ci + 1) * c, lanes(p)] = yloc
            g_ref[q, ci] = g.astype(BF16)
            h_ref[q, ci] = h
            wc_ref[q, ci] = jnp.broadcast_to(wc, (8, LANES))

    def state_step(step):
        for q, (d, p) in enumerate(streams):
            ci = chunk_of(d, step)
            rows = slice(ci * c, (ci + 1) * c)
            sb = states[q].astype(BF16)
            outs[d][0, rows, lanes(p)] = outs[d][0, rows, lanes(p)] + _dot_nt(rb_ref[q, rows, :], sb)
            states[q] = states[q] * wc_ref[q, ci, 0:1, :] + _dot(sb, g_ref[q, ci]) + h_ref[q, ci]

    local_factors(range(0, half), ())
    local_factors(range(half, n_chunks), [functools.partial(state_step, s) for s in range(half)])
    for s in range(half, n_chunks):
        state_step(s)
    for q in range(len(streams)):
        s_ref[q] = states[q]


WKV_PAIRS = 4


def _wkv(r, v, kk, lw0, lw1, kt0, kt1, b0, b1, tb=256):
    bsz, t, w = r.shape
    tb = min(tb, t)
    nb = t // tb
    nc = tb // WKV_CHUNK
    wl = WKV_PAIRS * LANES
    ns = 2 * WKV_PAIRS
    fwd = pl.BlockSpec((1, tb, wl), lambda b, h, g: (b, g, h))
    bwd = pl.BlockSpec((1, tb, wl), lambda b, h, g: (b, nb - 1 - g, h))
    sds = jax.ShapeDtypeStruct((bsz, t, w), F32)
    return pl.pallas_call(
        _wkv_kernel,
        grid=(bsz, w // wl, nb),
        in_specs=[fwd] * 6 + [bwd] * 6,
        out_specs=[fwd, bwd],
        out_shape=[sds, sds],
        scratch_shapes=[pltpu.VMEM((ns, LANES, LANES), F32), pltpu.VMEM((ns, tb, LANES), BF16),
                        pltpu.VMEM((ns, nc, LANES, LANES), BF16), pltpu.VMEM((ns, nc, LANES, LANES), F32),
                        pltpu.VMEM((ns, nc, 8, LANES), F32)],
        compiler_params=_cparams("parallel", "parallel", "arbitrary"),
        name="wkv7_chunked",
    )(r, v, kk, lw0, kt0, b0, r, v, kk, lw1, kt1, b1)


def _rwkv_post_kernel(yf_ref, yb_ref, bonus_ref, g_ref, lw_ref, lb_ref, o_ref):
    ones_bd = _head_block_ones(RWKV_HEAD)
    inv_n = 1.0 / RWKV_HEAD
    for j in range(o_ref.shape[2] // LANES):
        cols = slice(j * LANES, (j + 1) * LANES)
        wkv = yf_ref[0, :, cols] + yb_ref[0, :, cols]
        mu = _head_sum(wkv, ones_bd) * inv_n
        d = wkv - mu
        var = _head_sum(d * d, ones_bd) * inv_n
        gn = d * lax.rsqrt(var + LNX_EPS) * lw_ref[:, cols] + lb_ref[:, cols]
        gate = _silu(g_ref[0, :, cols].astype(F32))
        o_ref[0, :, cols] = ((gn + bonus_ref[0, :, cols]) * gate).astype(o_ref.dtype)


def _rwkv_post(yf, yb, bonus, g, lnx_w, lnx_b, tt=512):
    bsz, t, w = yf.shape
    tt = min(tt, t)
    spec = pl.BlockSpec((1, tt, w), lambda b, i: (b, i, 0))
    vec = pl.BlockSpec((1, w), lambda b, i: (0, 0))
    return pl.pallas_call(
        _rwkv_post_kernel,
        grid=(bsz, t // tt),
        in_specs=[spec, spec, spec, spec, vec, vec],
        out_specs=spec,
        out_shape=jax.ShapeDtypeStruct((bsz, t, w), BF16),
        compiler_params=_cparams("parallel", "parallel"),
        name="rwkv_post",
    )(yf, yb, bonus, g, lnx_w.reshape(1, w), lnx_b.reshape(1, w))


def _sgu_kernel(h_ref, lg_ref, lb_ref, ws_ref, bs_ref, o_ref):
    bw = BRANCH_W
    tt = h_ref.shape[1]
    u = _gelu_tanh(h_ref[0, :, 0:bw].astype(F32))
    vv = _gelu_tanh(h_ref[0, :, bw:2 * bw].astype(F32))
    g = h_ref[0, :, 2 * bw:3 * bw].astype(F32)
    mu = jnp.mean(vv, axis=-1, keepdims=True)
    d = vv - mu
    var = jnp.mean(d * d, axis=-1, keepdims=True)
    vn = (d * lax.rsqrt(var + 1e-5) * lg_ref[...] + lb_ref[...]).astype(BF16)
    gate = u * _silu(g)
    for ck in range(tt // SG_CHUNK):
        rs = slice(ck * SG_CHUNK, (ck + 1) * SG_CHUNK)
        for grp in range(SG_GROUPS):
            cs = slice(grp * LANES, (grp + 1) * LANES)
            sv = _dot(ws_ref[grp], vn[rs, cs]) + bs_ref[:, cs]
            o_ref[0, rs, cs] = (gate[rs, cs] * sv).astype(o_ref.dtype)


def _sgu(h_b, ln_g, ln_b, w_s, bs_cols, tt=512):
    bsz, t, w3 = h_b.shape
    bw = BRANCH_W
    tt = min(tt, t)
    vec = pl.BlockSpec((1, bw), lambda b, i: (0, 0))
    return pl.pallas_call(
        _sgu_kernel,
        grid=(bsz, t // tt),
        in_specs=[pl.BlockSpec((1, tt, w3), lambda b, i: (b, i, 0)), vec, vec,
                  pl.BlockSpec((SG_GROUPS, SG_CHUNK, SG_CHUNK), lambda b, i: (0, 0, 0)),
                  pl.BlockSpec((SG_CHUNK, bw), lambda b, i: (0, 0))],
        out_specs=pl.BlockSpec((1, tt, bw), lambda b, i: (b, i, 0)),
        out_shape=jax.ShapeDtypeStruct((bsz, t, bw), BF16),
        compiler_params=_cparams("parallel", "parallel"),
        name="spatial_gating",
    )(h_b, ln_g.reshape(1, bw), ln_b.reshape(1, bw), w_s, bs_cols)


def _natten_bias_table(rpb):
    p = np.arange(GRID_W)[:, None]
    m = np.arange(GRID_W)[None, :]
    sj = np.clip(p - NA_COLS // 2, 0, GRID_W - NA_COLS)
    valid = (m >= sj) & (m < sj + NA_COLS)
    dc = np.clip(m - p, -(NA_COLS - 1), NA_COLS - 1) + NA_COLS - 1
    by_rel = jnp.where(valid[None, None], rpb[:, :, dc], NEG_INF)
    tab = jnp.stack([by_rel[:, s:s + NA_ROWS] for s in range(NA_ROWS)], axis=1)
    tab = tab.transpose(0, 1, 3, 2, 4)
    return tab.reshape(rpb.shape[0], NA_ROWS, GRID_W, NA_ROWS * GRID_W).astype(BF16)


NA_UNROLL = 16


def _natten_kernel(q_ref, k_ref, v_ref, g_ref, qn_ref, kn_ref, bias_ref, o_ref, qs_ref, ks_ref):
    t = q_ref.shape[1]
    n_rows = t // GRID_W
    win = NA_ROWS * GRID_W
    ones_bd = _head_block_ones(NA_HEAD)
    lane = lax.broadcasted_iota(jnp.int32, (1, LANES), 1)
    m0 = lane < NA_HEAD

    def norm(x_ref, gain_ref, scale):
        x = x_ref[0].astype(F32)
        ms = _head_sum(x * x, ones_bd) * (1.0 / NA_HEAD)
        return x * lax.rsqrt(ms + 1e-6) * (gain_ref[...] * scale)

    qs_ref[...] = norm(q_ref, qn_ref, NA_HEAD ** -0.5).astype(BF16)
    ks_ref[...] = norm(k_ref, kn_ref, 1.0).astype(BF16)

    def body(it, carry):
        rows = [it * NA_UNROLL + j for j in range(NA_UNROLL)]
        si = [jnp.clip(i - NA_ROWS // 2, 0, n_rows - NA_ROWS) for i in rows]
        start = [s - i + (NA_ROWS - 1) for s, i in zip(si, rows)]
        qo = [pl.multiple_of(i * GRID_W, GRID_W) for i in rows]
        ko = [pl.multiple_of(s * GRID_W, GRID_W) for s in si]
        q = [qs_ref[pl.ds(o, GRID_W), :] for o in qo]
        zero = jnp.zeros_like(q[0])
        q2 = [jnp.concatenate([jnp.where(m0, x, zero), jnp.where(m0, zero, x)], axis=0) for x in q]
        s = [_dot_nt(x, ks_ref[pl.ds(o, win), :]) for x, o in zip(q2, ko)]
        s = [x + jnp.concatenate([bias_ref[0, st], bias_ref[1, st]], axis=0).astype(F32) for x, st in zip(s, start)]
        e = [jnp.exp(x - jnp.max(x, axis=-1, keepdims=True)).astype(BF16) for x in s]
        o2 = [_dot(x, jnp.concatenate([v_ref[0, pl.ds(o, win), :], ones_blk], axis=1))
              for x, o in zip(e, ko)]
        for x, o in zip(o2, qo):
            x = x[:, 0:LANES] / x[:, LANES:2 * LANES]
            g = g_ref[0, pl.ds(o, GRID_W), :].astype(F32)
            val = jnp.where(m0, x[0:GRID_W], x[GRID_W:2 * GRID_W]) * _silu(g)
            o_ref[0, pl.ds(o, GRID_W), :] = val.astype(o_ref.dtype)
        return carry

    ones_blk = jnp.ones((win, LANES), BF16)

    lax.fori_loop(0, n_rows // NA_UNROLL, body, 0)


def _natten(h_c, q_norm, k_norm, bias_tab):
    bsz, t, w4 = h_c.shape
    bw = BRANCH_W
    nlb = bw // LANES
    sec = lambda s: pl.BlockSpec((1, t, LANES), functools.partial(lambda b, hp, s: (b, 0, s * nlb + hp), s=s))
    two = lambda a: jnp.concatenate([a, a]).reshape(1, LANES)
    return pl.pallas_call(
        _natten_kernel,
        grid=(bsz, nlb),
        in_specs=[sec(0), sec(1), sec(2), sec(3),
                  pl.BlockSpec((1, LANES), lambda b, hp: (0, 0)), pl.BlockSpec((1, LANES), lambda b, hp: (0, 0)),
                  pl.BlockSpec((2, NA_ROWS, GRID_W, NA_ROWS * GRID_W), lambda b, hp: (hp, 0, 0, 0))],
        out_specs=pl.BlockSpec((1, t, LANES), lambda b, hp: (b, 0, hp)),
        out_shape=jax.ShapeDtypeStruct((bsz, t, bw), BF16),
        scratch_shapes=[pltpu.VMEM((t, LANES), BF16), pltpu.VMEM((t, LANES), BF16)],
        compiler_params=_cparams("parallel", "parallel"),
        name="neighbourhood_attention",
    )(h_c, h_c, h_c, h_c, two(q_norm), two(k_norm), bias_tab)


def _memattn_kernel(qg_ref, kv_ref, qn_ref, kn_ref, o_ref):
    hd, bw = MEM_HEAD, BRANCH_W

    def norm(x, gain, scale):
        ms = jnp.mean(x * x, axis=-1, keepdims=True)
        return (x * lax.rsqrt(ms + 1e-6) * (gain * scale)).astype(BF16)

    heads = [slice(h * hd, (h + 1) * hd) for h in range(MEM_HEADS)]
    q = [norm(qg_ref[0, :, c].astype(F32), qn_ref[...], hd ** -0.5) for c in heads]
    k = [norm(kv_ref[0, :, c].astype(F32), kn_ref[...], 1.0) for c in heads]
    s = [_dot_nt(a, b) for a, b in zip(q, k)]
    e = [jnp.exp(x - jnp.max(x, axis=-1, keepdims=True)) for x in s]
    p = [(x / jnp.sum(x, axis=-1, keepdims=True)).astype(BF16) for x in e]
    o = [_dot(x, kv_ref[0, :, bw + h * hd:bw + (h + 1) * hd]) for h, x in enumerate(p)]
    for c, x in zip(heads, o):
        g = qg_ref[0, :, bw + c.start:bw + c.stop].astype(F32)
        o_ref[0, :, c] = (x * _silu(g)).astype(o_ref.dtype)


def _memattn(h_m, kv, q_norm, k_norm, tt=512):
    bsz, t, w2 = h_m.shape
    mlen = kv.shape[1]
    tt = min(tt, t)
    vec = pl.BlockSpec((1, MEM_HEAD), lambda b, i: (0, 0))
    return pl.pallas_call(
        _memattn_kernel,
        grid=(bsz, t // tt),
        in_specs=[pl.BlockSpec((1, tt, w2), lambda b, i: (b, i, 0)),
                  pl.BlockSpec((1, mlen, w2), lambda b, i: (b, 0, 0)),
                  vec, vec],
        out_specs=pl.BlockSpec((1, tt, BRANCH_W), lambda b, i: (b, i, 0)),
        out_shape=jax.ShapeDtypeStruct((bsz, t, BRANCH_W), BF16),
        compiler_params=_cparams("parallel", "parallel"),
        name="memory_attention",
    )(h_m, kv, q_norm.reshape(1, MEM_HEAD), k_norm.reshape(1, MEM_HEAD))


def _pad_lora(up):
    z = jnp.zeros_like(up[0])
    w = jnp.stack([jnp.concatenate([up[0], z], axis=0), jnp.concatenate([z, up[1]], axis=0)])
    hi = w.astype(BF16)
    return jnp.stack([hi, (w - hi.astype(F32)).astype(BF16)])


def _layer(x2d, mem2d, bsz, layer, p, big, w_in_bf16, w_in_f32):
    m, d = x2d.shape
    t = m // bsz
    bw = BRANCH_W
    a_w = A_SHIFT_W + bw
    o1, o2, o3, o4 = a_w, a_w + 3 * bw, a_w + 7 * bw, a_w + 9 * bw
    xn = _rmsnorm(x2d, p["norm_g"], BF16)
    proj = functools.partial(_matmul, xn, w_in_bf16, 0)
    proj_cast = functools.partial(_matmul_and_cast, xn, w_in_bf16, 0)
    n_layers = w_in_f32.shape[0]

    h_rkv = proj(0, 3 * bw, BF16, 1024, 1024, name="proj_a_rkv")
    h_lora = proj(3 * bw, 4 * LORA, BF16, 1024, 4 * LORA, name="proj_a_lora")
    h_ag, w_kv = proj_cast(A_SHIFT_W, bw, BF16, 1024, 1024, big["m_w_kv"], layer, name="proj_a_gate_cast_kv")
    h_b = proj(o1, 3 * bw, BF16, 1024, 1024, name="proj_b")
    h_c, w_br = proj_cast(o2, 4 * bw, BF16, 1024, 1024, big["w_branch"].reshape(n_layers, N_BRANCH * bw, d),
                          layer, name="proj_c_cast_branch")
    h_m, w_o = proj_cast(o3, 2 * bw, BF16, 1024, 1024, big["w_out"], layer, name="proj_m_cast_out")
    if layer + 1 < n_layers:
        gates, w_next = _matmul_and_cast(xn, w_in_bf16, 0, o4, 4 * d, BF16, 1024, 1024, w_in_f32, layer + 1,
                                         act="sigmoid", name="proj_gates_cast_next")
    else:
        gates, w_next = proj(o4, 4 * d, BF16, 1024, 1024, act="sigmoid", name="proj_gates"), None

    prep = _rwkv_prep(h_rkv.reshape(bsz, t, 3 * bw), h_lora.reshape(bsz, t, 4 * LORA), p["a_conv"],
                      _pad_lora(p["a_w_up"]), p["a_w0"], _pad_lora(p["a_a_up"]), p["a_a0"],
                      p["a_k_k"], p["a_k_a"], p["a_r_k"].reshape(-1))
    r, v, kk, lw0, lw1, kt0, kt1, b0, b1, bonus = prep
    yf, yb = _wkv(r, v, kk, lw0, lw1, kt0, kt1, b0, b1)
    y_a = _rwkv_post(yf, yb, bonus, h_ag.reshape(bsz, t, bw), p["a_lnx_w"], p["a_lnx_b"])

    bs_cols = jnp.repeat(p["b_b_s"].T, SG_CHUNK, axis=1)
    y_b = _sgu(h_b.reshape(bsz, t, 3 * bw), p["b_ln_g"], p["b_ln_b"], p["b_w_s"].astype(BF16), bs_cols)

    y_c = _natten(h_c.reshape(bsz, t, 4 * bw), p["c_q_norm"], p["c_k_norm"], _natten_bias_table(p["c_rpb"]))

    mem_n = _rmsnorm(mem2d, p["m_norm_g"], BF16)
    kv = _matmul(mem_n, w_kv, 0, 0, 2 * bw, BF16, 1024, 1024, name="proj_mem_kv")
    y_d = _memattn(h_m.reshape(bsz, t, 2 * bw), kv.reshape(bsz, -1, 2 * bw), p["m_q_norm"], p["m_k_norm"])

    ys = [y.reshape(m, bw) for y in (y_a, y_b, y_c, y_d)]
    merged = _merge(ys, w_br.reshape(1, N_BRANCH, bw, d), 0, gates, 512, 1024)
    return _matmul_residual(merged, w_o, 0, x2d, 512, 1024), w_next


def kernel(x, mem, norm_g, w_in, a_conv, a_w_up, a_w0, a_a_up, a_a0, a_k_k, a_k_a, a_r_k, a_lnx_w, a_lnx_b,
           b_ln_g, b_ln_b, b_w_s, b_b_s, c_q_norm, c_k_norm, c_rpb, m_norm_g, m_w_kv, m_q_norm, m_k_norm,
           w_branch, w_out):
    params = dict(norm_g=norm_g, a_conv=a_conv, a_w_up=a_w_up, a_w0=a_w0, a_a_up=a_a_up, a_a0=a_a0,
                  a_k_k=a_k_k, a_k_a=a_k_a, a_r_k=a_r_k, a_lnx_w=a_lnx_w, a_lnx_b=a_lnx_b, b_ln_g=b_ln_g,
                  b_ln_b=b_ln_b, b_w_s=b_w_s, b_b_s=b_b_s, c_q_norm=c_q_norm, c_k_norm=c_k_norm, c_rpb=c_rpb,
                  m_norm_g=m_norm_g, m_q_norm=m_q_norm, m_k_norm=m_k_norm)
    big = dict(m_w_kv=m_w_kv, w_branch=w_branch, w_out=w_out)
    bsz, t, d = x.shape
    x2d = x.reshape(bsz * t, d)
    mem2d = mem.reshape(-1, d)
    w_cur = w_in[0:1].astype(BF16)
    for l in range(norm_g.shape[0]):
        x2d, w_cur = _layer(x2d, mem2d, bsz, l, {k: v[l] for k, v in params.items()}, big, w_cur, w_in)
    return x2d.reshape(bsz, t, d)
```

```python
import functools

import numpy as np
import jax
import jax.numpy as jnp
from jax import lax
from jax.experimental import pallas as pl
from jax.experimental.pallas import tpu as pltpu

F32 = jnp.float32
BF16 = jnp.bfloat16

VMEM_LIMIT_BYTES = 56 * 1024 * 1024
LANES = 128
BF16_SUBLANES = 16
F32_SUBLANES = 8

BRANCH_W = 1024
N_BRANCH = 4
RWKV_HEAD = 64
LORA = 64
A_SHIFT_W = 3 * BRANCH_W + 4 * LORA
LNX_EPS = 64e-5
SG_CHUNK = 128
SG_GROUPS = 8
NA_HEAD = 64
NA_ROWS = 8
NA_COLS = 16
GRID_W = 64
MEM_HEADS = 4
MEM_HEAD = 256
WKV_CHUNK = 64
NEG_INF = -1e30


def _cparams(*sem):
    return pltpu.CompilerParams(dimension_semantics=sem, vmem_limit_bytes=VMEM_LIMIT_BYTES)


def _sigmoid(x):
    return 0.5 * jnp.tanh(0.5 * x) + 0.5


def _silu(x):
    return x * _sigmoid(x)


def _gelu_tanh(x):
    c = np.float32(np.sqrt(2.0 / np.pi))
    half = 0.5 * x
    return half + half * jnp.tanh(x * (c + np.float32(c * 0.044715) * (x * x)))


def _dot(a, b, precision=None):
    return jnp.dot(a, b, preferred_element_type=F32, precision=precision)


def _dot_nt(a, b, precision=None):
    return lax.dot_general(a, b, (((1,), (1,)), ((), ())), preferred_element_type=F32,
                           precision=precision)


def _dot_tn(a, b, precision=None):
    return lax.dot_general(a, b, (((0,), (0,)), ((), ())), preferred_element_type=F32,
                           precision=precision)


def _split_bf16(x, terms):
    parts = []
    for _ in range(terms - 1):
        hi = x.astype(BF16)
        parts.append(hi)
        x = x - hi.astype(F32)
    parts.append(x.astype(BF16))
    return parts


def _dot_exact_rhs(x, w_bf16, terms):
    acc = None
    for piece in _split_bf16(x, terms):
        d = _dot(piece, w_bf16)
        acc = d if acc is None else acc + d
    return acc


def _head_block_ones(head):
    shift = int(np.log2(head))
    r = lax.broadcasted_iota(jnp.int32, (LANES, LANES), 0) >> shift
    c = lax.broadcasted_iota(jnp.int32, (LANES, LANES), 1) >> shift
    return jnp.where(r == c, 1.0, 0.0).astype(BF16)


def _head_sum(x, ones_bd, terms=1):
    parts = [_dot_exact_rhs(x[:, j:j + LANES], ones_bd, terms) for j in range(0, x.shape[1], LANES)]
    return parts[0] if len(parts) == 1 else jnp.concatenate(parts, axis=1)


def _rmsnorm_kernel(x_ref, g_ref, o_ref, *, eps):
    d = x_ref.shape[1]
    chunks = [slice(c, c + NORM_CHUNK) for c in range(0, d, NORM_CHUNK)]
    ssq = None
    for cols in chunks:
        x = x_ref[:, cols].astype(F32)
        s = jnp.sum(x * x, axis=-1, keepdims=True)
        ssq = s if ssq is None else ssq + s
    scale = lax.rsqrt(ssq * (1.0 / d) + eps)
    for cols in chunks:
        o_ref[:, cols] = (x_ref[:, cols].astype(F32) * scale * g_ref[:, cols]).astype(o_ref.dtype)


NORM_CHUNK = 512


def _rmsnorm(x2d, g, out_dtype, tm=512, eps=1e-6):
    m, d = x2d.shape
    tm = min(tm, m)
    return pl.pallas_call(
        functools.partial(_rmsnorm_kernel, eps=eps),
        grid=(m // tm,),
        in_specs=[pl.BlockSpec((tm, d), lambda i: (i, 0)), pl.BlockSpec((1, d), lambda i: (0, 0))],
        out_specs=pl.BlockSpec((tm, d), lambda i: (i, 0)),
        out_shape=jax.ShapeDtypeStruct((m, d), out_dtype),
        compiler_params=_cparams("parallel"),
        name="rmsnorm",
    )(x2d, g.reshape(1, d))


def _norm_mm_kernel(x_ref, g_ref, b_ref, o_ref, xn_ref, *, eps):
    @pl.when(pl.program_id(1) == 0)
    def _():
        _rmsnorm_kernel(x_ref, g_ref, xn_ref, eps=eps)

    o_ref[...] = _dot(xn_ref[...], b_ref[0]).astype(o_ref.dtype)


def _rmsnorm_matmul(x2d, g, w, layer, col0, n, out_dtype, tm, tn, eps=1e-6, name="rmsnorm_matmul"):
    m, d = x2d.shape
    tm, tn = min(tm, m), min(tn, n)
    return pl.pallas_call(
        functools.partial(_norm_mm_kernel, eps=eps),
        grid=(m // tm, n // tn),
        in_specs=[pl.BlockSpec((tm, d), lambda i, j: (i, 0)),
                  pl.BlockSpec((1, d), lambda i, j: (0, 0)),
                  pl.BlockSpec((pl.Element(1), pl.Element(d), pl.Element(tn)),
                               lambda i, j: (layer, 0, pl.multiple_of(col0 + j * tn, LANES)))],
        out_specs=[pl.BlockSpec((tm, tn), lambda i, j: (i, j)),
                   pl.BlockSpec((tm, d), lambda i, j: (i, 0))],
        out_shape=[jax.ShapeDtypeStruct((m, n), out_dtype), jax.ShapeDtypeStruct((m, d), BF16)],
        compiler_params=_cparams("parallel", "arbitrary"),
        name=name,
    )(x2d, g.reshape(1, d), w)


def _mm_kernel(a_ref, b_ref, o_ref, *, act):
    acc = _dot(a_ref[...], b_ref[0])
    if act == "sigmoid":
        acc = _sigmoid(acc)
    o_ref[...] = acc.astype(o_ref.dtype)


def _matmul(a, w, layer, col0, n, out_dtype, tm, tn, act=None, name="matmul"):
    m, k = a.shape
    tm, tn = min(tm, m), min(tn, n)
    return pl.pallas_call(
        functools.partial(_mm_kernel, act=act),
        grid=(m // tm, n // tn),
        in_specs=[pl.BlockSpec((tm, k), lambda i, j: (i, 0)),
                  pl.BlockSpec((pl.Element(1), pl.Element(k), pl.Element(tn)),
                               lambda i, j: (layer, 0, pl.multiple_of(col0 + j * tn, LANES)))],
        out_specs=pl.BlockSpec((tm, tn), lambda i, j: (i, j)),
        out_shape=jax.ShapeDtypeStruct((m, n), out_dtype),
        compiler_params=_cparams("parallel", "parallel"),
        name=name,
    )(a, w)


def _mm_cast_kernel(a_ref, b_ref, src_ref, o_ref, dst_ref, *, act):
    _mm_kernel(a_ref, b_ref, o_ref, act=act)
    dst_ref[0] = src_ref[...].astype(dst_ref.dtype)


def _matmul_and_cast(a, w, layer, col0, n, out_dtype, tm, tn, src, src_layer, act=None, name="matmul_cast"):
    m, k = a.shape
    tm, tn = min(tm, m), min(tn, n)
    ni, nj = m // tm, n // tn
    rows, cols = src.shape[1] // (ni * nj), src.shape[2]
    assert rows * ni * nj == src.shape[1] and rows % BF16_SUBLANES == 0
    return pl.pallas_call(
        functools.partial(_mm_cast_kernel, act=act),
        grid=(ni, nj),
        in_specs=[pl.BlockSpec((tm, k), lambda i, j: (i, 0)),
                  pl.BlockSpec((pl.Element(1), pl.Element(k), pl.Element(tn)),
                               lambda i, j: (layer, 0, pl.multiple_of(col0 + j * tn, LANES))),
                  pl.BlockSpec((None, rows, cols), lambda i, j: (src_layer, i * nj + j, 0))],
        out_specs=[pl.BlockSpec((tm, tn), lambda i, j: (i, j)),
                   pl.BlockSpec((1, rows, cols), lambda i, j: (0, i * nj + j, 0))],
        out_shape=[jax.ShapeDtypeStruct((m, n), out_dtype),
                   jax.ShapeDtypeStruct((1, src.shape[1], cols), BF16)],
        compiler_params=_cparams("parallel", "parallel"),
        name=name,
    )(a, w, src)


def _mm_residual_kernel(a_ref, b_ref, x_ref, o_ref):
    o_ref[...] = x_ref[...] + _dot(a_ref[...], b_ref[...])


def _matmul_residual(a, w, layer, x, tm, tn):
    m, k = a.shape
    n = w.shape[2]
    tm, tn = min(tm, m), min(tn, n)
    return pl.pallas_call(
        _mm_residual_kernel,
        grid=(n // tn, m // tm),
        in_specs=[pl.BlockSpec((tm, k), lambda j, i: (i, 0)),
                  pl.BlockSpec((None, k, tn), lambda j, i: (layer, 0, j)),
                  pl.BlockSpec((tm, tn), lambda j, i: (i, j))],
        out_specs=pl.BlockSpec((tm, tn), lambda j, i: (i, j)),
        out_shape=jax.ShapeDtypeStruct((m, n), x.dtype),
        compiler_params=_cparams("parallel", "parallel"),
        name="out_proj_residual",
    )(a, w, x)


def _merge_kernel(y0, y1, y2, y3, wb_ref, g0, g1, g2, g3, o_ref):
    acc = None
    for n, (y, g) in enumerate(((y0, g0), (y1, g1), (y2, g2), (y3, g3))):
        term = g[...].astype(F32) * _dot(y[...], wb_ref[n])
        acc = term if acc is None else acc + term
    o_ref[...] = acc.astype(o_ref.dtype)


def _merge(ys, wb, layer, gates, tm, tn):
    m, bw = ys[0].shape
    d = wb.shape[3]
    tm, tn = min(tm, m), min(tn, d)
    nj = d // tn
    y_spec = pl.BlockSpec((tm, bw), lambda j, i: (i, 0))
    g_specs = [pl.BlockSpec((tm, tn), functools.partial(lambda j, i, n: (i, n * nj + j), n=n))
               for n in range(4)]
    return pl.pallas_call(
        _merge_kernel,
        grid=(nj, m // tm),
        in_specs=[y_spec] * 4 + [pl.BlockSpec((None, 4, bw, tn), lambda j, i: (layer, 0, 0, j))] + g_specs,
        out_specs=pl.BlockSpec((tm, tn), lambda j, i: (i, j)),
        out_shape=jax.ShapeDtypeStruct((m, d), BF16),
        compiler_params=_cparams("parallel", "parallel"),
        name="gated_merge",
    )(*ys, wb, gates, gates, gates, gates)


def _token_shift(h_ref, hp_ref, hn_ref, conv_ref, cols):
    i = pl.program_id(1)
    nt = pl.num_programs(1)
    h = h_ref[0, :, cols].astype(F32)
    tt = h.shape[0]
    prev_row = hp_ref[0, BF16_SUBLANES - 1:BF16_SUBLANES, cols].astype(F32) * (i > 0).astype(F32)
    next_row = hn_ref[0, 0:1, cols].astype(F32) * (i < nt - 1).astype(F32)
    rows = lax.broadcasted_iota(jnp.int32, (F32_SUBLANES, 1), 0)
    dn, up = pltpu.roll(h, 1, 0), pltpu.roll(h, tt - 1, 0)
    h_dn = jnp.concatenate([jnp.where(rows == 0, prev_row, dn[0:F32_SUBLANES]), dn[F32_SUBLANES:]], axis=0)
    h_up = jnp.concatenate([up[0:tt - F32_SUBLANES],
                            jnp.where(rows == F32_SUBLANES - 1, next_row, up[tt - F32_SUBLANES:])], axis=0)
    return h_dn * conv_ref[0:1, cols] + h * conv_ref[1:2, cols] + h_up * conv_ref[2:3, cols]


def _rwkv_prep_kernel(h_ref, hp_ref, hn_ref, l_ref, lp_ref, ln_ref, conv_ref, convl_ref, wup_ref, w0_ref,
                      aup_ref, a0_ref, kk_ref, ka_ref, rk_ref,
                      r_o, v_o, kk_o, lw0_o, lw1_o, kt0_o, kt1_o, b0_o, b1_o, bonus_o):
    bw = BRANCH_W
    ls = _token_shift(l_ref, lp_ref, ln_ref, convl_ref, slice(0, 4 * LORA))
    wd_pieces = _split_bf16(jnp.tanh(ls[:, 0:2 * LORA]), 2)
    ad_pieces = _split_bf16(ls[:, 2 * LORA:4 * LORA], 2)
    ones_bd = _head_block_ones(RWKV_HEAD)
    decay_scale = np.float32(np.exp(-0.5))

    def up_proj(x_pieces, w_ref, z, cols):
        (x_hi, x_lo), w_hi, w_lo = x_pieces, w_ref[0, z, :, cols], w_ref[1, z, :, cols]
        return _dot(x_hi, w_hi) + (_dot(x_lo, w_hi) + _dot(x_hi, w_lo))

    for j in range(bw // LANES):
        cols = slice(j * LANES, (j + 1) * LANES)
        r, k, v = (_token_shift(h_ref, hp_ref, hn_ref, conv_ref, slice(s * bw + j * LANES, s * bw + (j + 1) * LANES))
                   for s in range(3))
        kkr = k * kk_ref[:, cols]
        kk = kkr * lax.rsqrt(jnp.maximum(_head_sum(kkr * kkr, ones_bd), 1e-24))
        kts = []
        for z, (lw_o, kt_o, b_o) in enumerate(((lw0_o, kt0_o, b0_o), (lw1_o, kt1_o, b1_o))):
            w_raw = w0_ref[z:z + 1, cols] + up_proj(wd_pieces, wup_ref, z, cols)
            lw_o[0, :, cols] = -decay_scale * _sigmoid(w_raw)
            a = _sigmoid(a0_ref[z:z + 1, cols] + up_proj(ad_pieces, aup_ref, z, cols))
            kt = k * (1.0 + (a - 1.0) * ka_ref[:, cols])
            kt_o[0, :, cols] = kt
            b_o[0, :, cols] = kk * a
            kts.append(kt)
        r_o[0, :, cols] = r
        v_o[0, :, cols] = v
        kk_o[0, :, cols] = kk
        bonus_o[0, :, cols] = _head_sum(r * (kts[0] + kts[1]) * rk_ref[:, cols], ones_bd) * v


def _rwkv_prep(h_rkv, h_lora, conv, wup_pad, w0, aup_pad, a0, k_k, k_a, r_k, tt=256):
    bsz, t, w = h_rkv.shape
    wl = h_lora.shape[2]
    tt = min(tt, t)
    nt = t // tt
    hb = tt // BF16_SUBLANES
    n_halo = t // BF16_SUBLANES
    row = lambda a: a.reshape(1, -1)
    vec_spec = pl.BlockSpec((1, BRANCH_W), lambda b, i: (0, 0))
    out_spec = pl.BlockSpec((1, tt, BRANCH_W), lambda b, i: (b, i, 0))
    out_sds = jax.ShapeDtypeStruct((bsz, t, BRANCH_W), F32)
    tile = lambda width: [
        pl.BlockSpec((1, tt, width), lambda b, i: (b, i, 0)),
        pl.BlockSpec((1, BF16_SUBLANES, width), lambda b, i: (b, jnp.maximum(i * hb - 1, 0), 0)),
        pl.BlockSpec((1, BF16_SUBLANES, width), lambda b, i: (b, jnp.minimum((i + 1) * hb, n_halo - 1), 0))]
    return pl.pallas_call(
        _rwkv_prep_kernel,
        grid=(bsz, nt),
        in_specs=tile(w) + tile(wl) + [
            pl.BlockSpec((3, w), lambda b, i: (0, 0)),
            pl.BlockSpec((3, wl), lambda b, i: (0, 0)),
            pl.BlockSpec((2, 2, 2 * LORA, BRANCH_W), lambda b, i: (0, 0, 0, 0)),
            pl.BlockSpec((2, BRANCH_W), lambda b, i: (0, 0)),
            pl.BlockSpec((2, 2, 2 * LORA, BRANCH_W), lambda b, i: (0, 0, 0, 0)),
            pl.BlockSpec((2, BRANCH_W), lambda b, i: (0, 0)),
            vec_spec, vec_spec, vec_spec,
        ],
        out_specs=[out_spec] * 10,
        out_shape=[out_sds] * 10,
        compiler_params=_cparams("parallel", "parallel"),
        name="rwkv_prep",
    )(h_rkv, h_rkv, h_rkv, h_lora, h_lora, h_lora, conv[:, :w], conv[:, w:], wup_pad, w0, aup_pad, a0,
      row(k_k), row(k_a), row(r_k))


def _wkv_masks():
    c = WKV_CHUNK
    row = lax.broadcasted_iota(jnp.int32, (2 * c, 2 * c), 0)
    col = lax.broadcasted_iota(jnp.int32, (2 * c, 2 * c), 1)
    same = (row >> 6) == (col >> 6)
    rt, ct = row & (c - 1), col & (c - 1)
    f = lambda m: jnp.where(same & m, 1.0, 0.0).astype(F32)
    lane = lax.broadcasted_iota(jnp.int32, (1, LANES), 1)
    crow = lax.broadcasted_iota(jnp.int32, (c, c), 0)
    ccol = lax.broadcasted_iota(jnp.int32, (c, c), 1)
    return {
        "eye": f(rt == ct),
        "strict": (f(ct < rt), f(ct > rt)),
        "incl": (f(ct <= rt), f(ct >= rt)),
        "cum": (jnp.where(ccol <= crow, 1.0, 0.0).astype(BF16), jnp.where(ccol >= crow, 1.0, 0.0).astype(BF16)),
        "head0": jnp.where(lane < RWKV_HEAD, 1.0, 0.0).astype(F32),
        "head1": jnp.where(lane < RWKV_HEAD, 0.0, 1.0).astype(F32),
    }


def _wkv_local(problems, masks, fillers=()):
    c = WKV_CHUNK
    bf = lambda x: x.astype(BF16)
    m0, m1 = masks["head0"], masks["head1"]
    pair = lambda x: jnp.concatenate([x * m0, x * m1], axis=0)
    each = lambda fn, *lists: [fn(*xs) for xs in zip(*lists)]
    dirs = [p[6] for p in problems]
    fillers = list(fillers)
    n_points = 16
    stride = max(1, n_points // max(1, len(fillers)))
    seen = [0]

    def fill():
        seen[0] += 1
        if fillers and seen[0] % stride == 0:
            fillers.pop(0)()

    cl = [_cumsum_dot(masks["cum"][p[6]], p[0]) for p in problems]
    tot = [x[c - 1:c, :] if d == 0 else x[0:1, :] for x, d in zip(cl, dirs)]
    fill()
    zp = [pair(-p[4] * jnp.exp(x - p[0])) for p, x in zip(problems, cl)]
    rp = [pair(p[1] * jnp.exp(x)) for p, x in zip(problems, cl)]
    vpb = [bf(pair(p[3])) for p in problems]
    e_neg = [jnp.exp(-x) for x in cl]
    bk_start = [bf(jnp.concatenate([pair(p[5] * e), pair(p[2] * e)], axis=0)) for p, e in zip(problems, e_neg)]
    e_end = [jnp.exp(t - x) for t, x in zip(tot, cl)]
    bk_end = [bf(jnp.concatenate([pair(p[5] * e), pair(p[2] * e)], axis=0)) for p, e in zip(problems, e_end)]
    fill()

    scores = each(lambda z, r, bk: _dot_nt(bf(jnp.concatenate([z, r], axis=0)), bk), zp, rp, bk_start)
    fill()
    strict = [masks["strict"][d] for d in dirs]
    incl = [masks["incl"][d] for d in dirs]
    l_zb = each(lambda s, m: s[0:2 * c, 0:2 * c] * m, scores, strict)
    a_zk = each(lambda s, m: bf(s[0:2 * c, 2 * c:4 * c] * m), scores, strict)
    a_r = each(lambda s, m: bf(jnp.concatenate([s[2 * c:4 * c, 0:2 * c] * m, s[2 * c:4 * c, 2 * c:4 * c] * m],
                                               axis=1)), scores, incl)
    azk_v = each(_dot, a_zk, vpb)
    fill()

    inv = [masks["eye"] + l for l in l_zb]
    pb = [bf(l) for l in l_zb]
    for _ in range(5):
        pb = [bf(_dot(x, x)) for x in pb]
        fill()
        inv = each(lambda t, x: t + _dot(bf(t), x), inv, pb)
        fill()

    zu = each(lambda t, z, u: _dot(bf(t), bf(jnp.concatenate([z, u], axis=1))), inv, zp, azk_v)
    fill()
    stack = each(lambda x, v: jnp.concatenate([bf(x), jnp.concatenate([jnp.zeros_like(v), v], axis=1)], axis=0),
                 zu, vpb)
    ry = each(_dot, a_r, stack)
    fill()
    gh = each(_dot_tn, stack, bk_end)
    while fillers:
        fillers.pop(0)()
    out = []
    for r, y, g, t in zip(rp, ry, gh, tot):
        rb = r + y[:, 0:2 * c]
        out.append((rb[0:c] + rb[c:2 * c], y[0:c, 2 * c:4 * c] + y[c:2 * c, 2 * c:4 * c],
                    g[0:2 * c], g[2 * c:4 * c], jnp.exp(t)))
    return out


def _cumsum_dot(cum_bf16, lw):
    acc = None
    for piece in _split_bf16(lw, 2):
        d = _dot(cum_bf16, piece)
        acc = d if acc is None else acc + d
    return acc


def _wkv_kernel(r_f, v_f, kk_f, lw_f, kt_f, b_f, r_b, v_b, kk_b, lw_b, kt_b, b_b, yf_o, yb_o,
                s_ref, rb_ref, g_ref, h_ref, wc_ref):
    @pl.when(pl.program_id(2) == 0)
    def _():
        s_ref[...] = jnp.zeros_like(s_ref)

    c = WKV_CHUNK
    n_chunks = r_f.shape[1] // c
    half = n_chunks // 2
    masks = _wkv_masks()
    ins = ((lw_f, r_f, kt_f, v_f, kk_f, b_f), (lw_b, r_b, kt_b, v_b, kk_b, b_b))
    outs = (yf_o, yb_o)
    streams = [(d, p) for d in range(2) for p in range(WKV_PAIRS)]
    states = [s_ref[q] for q in range(len(streams))]
    chunk_of = lambda d, step: step if d == 0 else n_chunks - 1 - step
    lanes = lambda p: slice(p * LANES, (p + 1) * LANES)

    def local_factors(steps, fillers):
        where = [(q, chunk_of(streams[q][0], s)) for s in steps for q in range(len(streams))]
        problems = [[ref[0, ci * c:(ci + 1) * c, lanes(streams[q][1])] for ref in ins[streams[q][0]]]
                    + [streams[q][0]] for q, ci in where]
        for (q, ci), (rb, yloc, g, h, wc) in zip(where, _wkv_local(problems, masks, fillers)):
            d, p = streams[q]
            rb_ref[q, ci * c:(ci + 1) * c, :] = rb.astype(BF16)
            outs[d][0, ci * c:(ci + 1) * c, lanes(p)] = yloc
            g_ref[q, ci] = g.astype(BF16)
            h_ref[q, ci] = h
            wc_ref[q, ci] = jnp.broadcast_to(wc, (8, LANES))

    def state_step(step):
        for q, (d, p) in enumerate(streams):
            ci = chunk_of(d, step)
            rows = slice(ci * c, (ci + 1) * c)
            sb = states[q].astype(BF16)
            outs[d][0, rows, lanes(p)] = outs[d][0, rows, lanes(p)] + _dot_nt(rb_ref[q, rows, :], sb)
            states[q] = states[q] * wc_ref[q, ci, 0:1, :] + _dot(sb, g_ref[q, ci]) + h_ref[q, ci]

    local_factors(range(0, half), ())
    local_factors(range(half, n_chunks), [functools.partial(state_step, s) for s in range(half)])
    for s in range(half, n_chunks):
        state_step(s)
    for q in range(len(streams)):
        s_ref[q] = states[q]


WKV_PAIRS = 4


def _wkv(r, v, kk, lw0, lw1, kt0, kt1, b0, b1, tb=256):
    bsz, t, w = r.shape
    tb = min(tb, t)
    nb = t // tb
    nc = tb // WKV_CHUNK
    wl = WKV_PAIRS * LANES
    ns = 2 * WKV_PAIRS
    fwd = pl.BlockSpec((1, tb, wl), lambda b, h, g: (b, g, h))
    bwd = pl.BlockSpec((1, tb, wl), lambda b, h, g: (b, nb - 1 - g, h))
    sds = jax.ShapeDtypeStruct((bsz, t, w), F32)
    return pl.pallas_call(
        _wkv_kernel,
        grid=(bsz, w // wl, nb),
        in_specs=[fwd] * 6 + [bwd] * 6,
        out_specs=[fwd, bwd],
        out_shape=[sds, sds],
        scratch_shapes=[pltpu.VMEM((ns, LANES, LANES), F32), pltpu.VMEM((ns, tb, LANES), BF16),
                        pltpu.VMEM((ns, nc, LANES, LANES), BF16), pltpu.VMEM((ns, nc, LANES, LANES), F32),
                        pltpu.VMEM((ns, nc, 8, LANES), F32)],
        compiler_params=_cparams("parallel", "parallel", "arbitrary"),
        name="wkv7_chunked",
    )(r, v, kk, lw0, kt0, b0, r, v, kk, lw1, kt1, b1)


def _rwkv_post_kernel(yf_ref, yb_ref, bonus_ref, g_ref, lw_ref, lb_ref, o_ref):
    ones_bd = _head_block_ones(RWKV_HEAD)
    inv_n = 1.0 / RWKV_HEAD
    for j in range(o_ref.shape[2] // LANES):
        cols = slice(j * LANES, (j + 1) * LANES)
        wkv = yf_ref[0, :, cols] + yb_ref[0, :, cols]
        mu = _head_sum(wkv, ones_bd) * inv_n
        d = wkv - mu
        var = _head_sum(d * d, ones_bd) * inv_n
        gn = d * lax.rsqrt(var + LNX_EPS) * lw_ref[:, cols] + lb_ref[:, cols]
        gate = _silu(g_ref[0, :, cols].astype(F32))
        o_ref[0, :, cols] = ((gn + bonus_ref[0, :, cols]) * gate).astype(o_ref.dtype)


def _rwkv_post(yf, yb, bonus, g, lnx_w, lnx_b, tt=512):
    bsz, t, w = yf.shape
    tt = min(tt, t)
    spec = pl.BlockSpec((1, tt, w), lambda b, i: (b, i, 0))
    vec = pl.BlockSpec((1, w), lambda b, i: (0, 0))
    return pl.pallas_call(
        _rwkv_post_kernel,
        grid=(bsz, t // tt),
        in_specs=[spec, spec, spec, spec, vec, vec],
        out_specs=spec,
        out_shape=jax.ShapeDtypeStruct((bsz, t, w), BF16),
        compiler_params=_cparams("parallel", "parallel"),
        name="rwkv_post",
    )(yf, yb, bonus, g, lnx_w.reshape(1, w), lnx_b.reshape(1, w))


def _sgu_kernel(h_ref, lg_ref, lb_ref, ws_ref, bs_ref, o_ref):
    bw = BRANCH_W
    tt = h_ref.shape[1]
    u = _gelu_tanh(h_ref[0, :, 0:bw].astype(F32))
    vv = _gelu_tanh(h_ref[0, :, bw:2 * bw].astype(F32))
    g = h_ref[0, :, 2 * bw:3 * bw].astype(F32)
    mu = jnp.mean(vv, axis=-1, keepdims=True)
    d = vv - mu
    var = jnp.mean(d * d, axis=-1, keepdims=True)
    vn = (d * lax.rsqrt(var + 1e-5) * lg_ref[...] + lb_ref[...]).astype(BF16)
    gate = u * _silu(g)
    for ck in range(tt // SG_CHUNK):
        rs = slice(ck * SG_CHUNK, (ck + 1) * SG_CHUNK)
        for grp in range(SG_GROUPS):
            cs = slice(grp * LANES, (grp + 1) * LANES)
            sv = _dot(ws_ref[grp], vn[rs, cs]) + bs_ref[:, cs]
            o_ref[0, rs, cs] = (gate[rs, cs] * sv).astype(o_ref.dtype)


def _sgu(h_b, ln_g, ln_b, w_s, bs_cols, tt=512):
    bsz, t, w3 = h_b.shape
    bw = BRANCH_W
    tt = min(tt, t)
    vec = pl.BlockSpec((1, bw), lambda b, i: (0, 0))
    return pl.pallas_call(
        _sgu_kernel,
        grid=(bsz, t // tt),
        in_specs=[pl.BlockSpec((1, tt, w3), lambda b, i: (b, i, 0)), vec, vec,
                  pl.BlockSpec((SG_GROUPS, SG_CHUNK, SG_CHUNK), lambda b, i: (0, 0, 0)),
                  pl.BlockSpec((SG_CHUNK, bw), lambda b, i: (0, 0))],
        out_specs=pl.BlockSpec((1, tt, bw), lambda b, i: (b, i, 0)),
        out_shape=jax.ShapeDtypeStruct((bsz, t, bw), BF16),
        compiler_params=_cparams("parallel", "parallel"),
        name="spatial_gating",
    )(h_b, ln_g.reshape(1, bw), ln_b.reshape(1, bw), w_s, bs_cols)


def _natten_bias_table(rpb):
    p = np.arange(GRID_W)[:, None]
    m = np.arange(GRID_W)[None, :]
    sj = np.clip(p - NA_COLS // 2, 0, GRID_W - NA_COLS)
    valid = (m >= sj) & (m < sj + NA_COLS)
    dc = np.clip(m - p, -(NA_COLS - 1), NA_COLS - 1) + NA_COLS - 1
    by_rel = jnp.where(valid[None, None], rpb[:, :, dc], NEG_INF)
    tab = jnp.stack([by_rel[:, s:s + NA_ROWS] for s in range(NA_ROWS)], axis=1)
    tab = tab.transpose(0, 1, 3, 2, 4)
    return tab.reshape(rpb.shape[0], NA_ROWS, GRID_W, NA_ROWS * GRID_W).astype(BF16)


NA_UNROLL = 16


def _natten_kernel(q_ref, k_ref, v_ref, g_ref, qn_ref, kn_ref, bias_ref, o_ref, qs_ref, ks_ref):
    t = q_ref.shape[1]
    n_rows = t // GRID_W
    win = NA_ROWS * GRID_W
    ones_bd = _head_block_ones(NA_HEAD)
    lane = lax.broadcasted_iota(jnp.int32, (1, LANES), 1)
    m0 = lane < NA_HEAD

    def norm(x_ref, gain_ref, scale):
        x = x_ref[0].astype(F32)
        ms = _head_sum(x * x, ones_bd) * (1.0 / NA_HEAD)
        return x * lax.rsqrt(ms + 1e-6) * (gain_ref[...] * scale)

    qs_ref[...] = norm(q_ref, qn_ref, NA_HEAD ** -0.5).astype(BF16)
    ks_ref[...] = norm(k_ref, kn_ref, 1.0).astype(BF16)

    def body(it, carry):
        rows = [it * NA_UNROLL + j for j in range(NA_UNROLL)]
        si = [jnp.clip(i - NA_ROWS // 2, 0, n_rows - NA_ROWS) for i in rows]
        start = [s - i + (NA_ROWS - 1) for s, i in zip(si, rows)]
        qo = [pl.multiple_of(i * GRID_W, GRID_W) for i in rows]
        ko = [pl.multiple_of(s * GRID_W, GRID_W) for s in si]
        q = [qs_ref[pl.ds(o, GRID_W), :] for o in qo]
        zero = jnp.zeros_like(q[0])
        q2 = [jnp.concatenate([jnp.where(m0, x, zero), jnp.where(m0, zero, x)], axis=0) for x in q]
        s = [_dot_nt(x, ks_ref[pl.ds(o, win), :]) for x, o in zip(q2, ko)]
        s = [x + jnp.concatenate([bias_ref[0, st], bias_ref[1, st]], axis=0).astype(F32) for x, st in zip(s, start)]
        e = [jnp.exp(x - jnp.max(x, axis=-1, keepdims=True)).astype(BF16) for x in s]
        o2 = [_dot(x, jnp.concatenate([v_ref[0, pl.ds(o, win), :], ones_blk], axis=1))
              for x, o in zip(e, ko)]
        for x, o in zip(o2, qo):
            x = x[:, 0:LANES] / x[:, LANES:2 * LANES]
            g = g_ref[0, pl.ds(o, GRID_W), :].astype(F32)
            val = jnp.where(m0, x[0:GRID_W], x[GRID_W:2 * GRID_W]) * _silu(g)
            o_ref[0, pl.ds(o, GRID_W), :] = val.astype(o_ref.dtype)
        return carry

    ones_blk = jnp.ones((win, LANES), BF16)

    lax.fori_loop(0, n_rows // NA_UNROLL, body, 0)


def _natten(h_c, q_norm, k_norm, bias_tab):
    bsz, t, w4 = h_c.shape
    bw = BRANCH_W
    nlb = bw // LANES
    sec = lambda s: pl.BlockSpec((1, t, LANES), functools.partial(lambda b, hp, s: (b, 0, s * nlb + hp), s=s))
    two = lambda a: jnp.concatenate([a, a]).reshape(1, LANES)
    return pl.pallas_call(
        _natten_kernel,
        grid=(bsz, nlb),
        in_specs=[sec(0), sec(1), sec(2), sec(3),
                  pl.BlockSpec((1, LANES), lambda b, hp: (0, 0)), pl.BlockSpec((1, LANES), lambda b, hp: (0, 0)),
                  pl.BlockSpec((2, NA_ROWS, GRID_W, NA_ROWS * GRID_W), lambda b, hp: (hp, 0, 0, 0))],
        out_specs=pl.BlockSpec((1, t, LANES), lambda b, hp: (b, 0, hp)),
        out_shape=jax.ShapeDtypeStruct((bsz, t, bw), BF16),
        scratch_shapes=[pltpu.VMEM((t, LANES), BF16), pltpu.VMEM((t, LANES), BF16)],
        compiler_params=_cparams("parallel", "parallel"),
        name="neighbourhood_attention",
    )(h_c, h_c, h_c, h_c, two(q_norm), two(k_norm), bias_tab)


def _memattn_kernel(qg_ref, kv_ref, qn_ref, kn_ref, o_ref):
    hd, bw = MEM_HEAD, BRANCH_W

    def norm(x, gain, scale):
        ms = jnp.mean(x * x, axis=-1, keepdims=True)
        return (x * lax.rsqrt(ms + 1e-6) * (gain * scale)).astype(BF16)

    heads = [slice(h * hd, (h + 1) * hd) for h in range(MEM_HEADS)]
    q = [norm(qg_ref[0, :, c].astype(F32), qn_ref[...], hd ** -0.5) for c in heads]
    k = [norm(kv_ref[0, :, c].astype(F32), kn_ref[...], 1.0) for c in heads]
    s = [_dot_nt(a, b) for a, b in zip(q, k)]
    e = [jnp.exp(x - jnp.max(x, axis=-1, keepdims=True)) for x in s]
    p = [(x / jnp.sum(x, axis=-1, keepdims=True)).astype(BF16) for x in e]
    o = [_dot(x, kv_ref[0, :, bw + h * hd:bw + (h + 1) * hd]) for h, x in enumerate(p)]
    for c, x in zip(heads, o):
        g = qg_ref[0, :, bw + c.start:bw + c.stop].astype(F32)
        o_ref[0, :, c] = (x * _silu(g)).astype(o_ref.dtype)


def _memattn(h_m, kv, q_norm, k_norm, tt=512):
    bsz, t, w2 = h_m.shape
    mlen = kv.shape[1]
    tt = min(tt, t)
    vec = pl.BlockSpec((1, MEM_HEAD), lambda b, i: (0, 0))
    return pl.pallas_call(
        _memattn_kernel,
        grid=(bsz, t // tt),
        in_specs=[pl.BlockSpec((1, tt, w2), lambda b, i: (b, i, 0)),
                  pl.BlockSpec((1, mlen, w2), lambda b, i: (b, 0, 0)),
                  vec, vec],
        out_specs=pl.BlockSpec((1, tt, BRANCH_W), lambda b, i: (b, i, 0)),
        out_shape=jax.ShapeDtypeStruct((bsz, t, BRANCH_W), BF16),
        compiler_params=_cparams("parallel", "parallel"),
        name="memory_attention",
    )(h_m, kv, q_norm.reshape(1, MEM_HEAD), k_norm.reshape(1, MEM_HEAD))


def _pad_lora(up):
    z = jnp.zeros_like(up[0])
    w = jnp.stack([jnp.concatenate([up[0], z], axis=0), jnp.concatenate([z, up[1]], axis=0)])
    hi = w.astype(BF16)
    return jnp.stack([hi, (w - hi.astype(F32)).astype(BF16)])


def _layer(x2d, mem2d, bsz, layer, p, big, w_in_bf16, w_in_f32):
    m, d = x2d.shape
    t = m // bsz
    bw = BRANCH_W
    a_w = A_SHIFT_W + bw
    o1, o2, o3, o4 = a_w, a_w + 3 * bw, a_w + 7 * bw, a_w + 9 * bw
    h_rkv, xn = _rmsnorm_matmul(x2d, p["norm_g"], w_in_bf16, 0, 0, 3 * bw, BF16, 512, 1024,
                                name="rmsnorm_proj_a_rkv")
    proj = functools.partial(_matmul, xn, w_in_bf16, 0)
    proj_cast = functools.partial(_matmul_and_cast, xn, w_in_bf16, 0)
    n_layers = w_in_f32.shape[0]

    h_lora = proj(3 * bw, 4 * LORA, BF16, 1024, 4 * LORA, name="proj_a_lora")
    h_ag, w_kv = proj_cast(A_SHIFT_W, bw, BF16, 1024, 1024, big["m_w_kv"], layer, name="proj_a_gate_cast_kv")
    h_b = proj(o1, 3 * bw, BF16, 1024, 1024, name="proj_b")
    h_c, w_br = proj_cast(o2, 4 * bw, BF16, 1024, 1024, big["w_branch"].reshape(n_layers, N_BRANCH * bw, d),
                          layer, name="proj_c_cast_branch")
    h_m, w_o = proj_cast(o3, 2 * bw, BF16, 1024, 1024, big["w_out"], layer, name="proj_m_cast_out")
    if layer + 1 < n_layers:
        gates, w_next = _matmul_and_cast(xn, w_in_bf16, 0, o4, 4 * d, BF16, 1024, 1024, w_in_f32, layer + 1,
                                         act="sigmoid", name="proj_gates_cast_next")
    else:
        gates, w_next = proj(o4, 4 * d, BF16, 1024, 1024, act="sigmoid", name="proj_gates"), None

    prep = _rwkv_prep(h_rkv.reshape(bsz, t, 3 * bw), h_lora.reshape(bsz, t, 4 * LORA), p["a_conv"],
                      _pad_lora(p["a_w_up"]), p["a_w0"], _pad_lora(p["a_a_up"]), p["a_a0"],
                      p["a_k_k"], p["a_k_a"], p["a_r_k"].reshape(-1))
    r, v, kk, lw0, lw1, kt0, kt1, b0, b1, bonus = prep
    yf, yb = _wkv(r, v, kk, lw0, lw1, kt0, kt1, b0, b1)
    y_a = _rwkv_post(yf, yb, bonus, h_ag.reshape(bsz, t, bw), p["a_lnx_w"], p["a_lnx_b"])

    bs_cols = jnp.repeat(p["b_b_s"].T, SG_CHUNK, axis=1)
    y_b = _sgu(h_b.reshape(bsz, t, 3 * bw), p["b_ln_g"], p["b_ln_b"], p["b_w_s"].astype(BF16), bs_cols)

    y_c = _natten(h_c.reshape(bsz, t, 4 * bw), p["c_q_norm"], p["c_k_norm"], _natten_bias_table(p["c_rpb"]))

    mem_n = _rmsnorm(mem2d, p["m_norm_g"], BF16)
    kv = _matmul(mem_n, w_kv, 0, 0, 2 * bw, BF16, 1024, 1024, name="proj_mem_kv")
    y_d = _memattn(h_m.reshape(bsz, t, 2 * bw), kv.reshape(bsz, -1, 2 * bw), p["m_q_norm"], p["m_k_norm"])

    ys = [y.reshape(m, bw) for y in (y_a, y_b, y_c, y_d)]
    merged = _merge(ys, w_br.reshape(1, N_BRANCH, bw, d), 0, gates, 512, 1024)
    return _matmul_residual(merged, w_o, 0, x2d, 512, 1024), w_next


def kernel(x, mem, norm_g, w_in, a_conv, a_w_up, a_w0, a_a_up, a_a0, a_k_k, a_k_a, a_r_k, a_lnx_w, a_lnx_b,
           b_ln_g, b_ln_b, b_w_s, b_b_s, c_q_norm, c_k_norm, c_rpb, m_norm_g, m_w_kv, m_q_norm, m_k_norm,
           w_branch, w_out):
    params = dict(norm_g=norm_g, a_conv=a_conv, a_w_up=a_w_up, a_w0=a_w0, a_a_up=a_a_up, a_a0=a_a0,
                  a_k_k=a_k_k, a_k_a=a_k_a, a_r_k=a_r_k, a_lnx_w=a_lnx_w, a_lnx_b=a_lnx_b, b_ln_g=b_ln_g,
                  b_ln_b=b_ln_b, b_w_s=b_w_s, b_b_s=b_b_s, c_q_norm=c_q_norm, c_k_norm=c_k_norm, c_rpb=c_rpb,
                  m_norm_g=m_norm_g, m_q_norm=m_q_norm, m_k_norm=m_k_norm)
    big = dict(m_w_kv=m_w_kv, w_branch=w_branch, w_out=w_out)
    bsz, t, d = x.shape
    x2d = x.reshape(bsz * t, d)
    mem2d = mem.reshape(-1, d)
    w_cur = w_in[0:1].astype(BF16)
    for l in range(norm_g.shape[0]):
        x2d, w_cur = _layer(x2d, mem2d, bsz, l, {k: v[l] for k, v in params.items()}, big, w_cur, w_in)
    return x2d.reshape(bsz, t, d)
```

```python
import functools

import numpy as np
import jax
import jax.numpy as jnp
from jax import lax
from jax.experimental import pallas as pl
from jax.experimental.pallas import tpu as pltpu

F32 = jnp.float32
BF16 = jnp.bfloat16

VMEM_LIMIT_BYTES = 56 * 1024 * 1024
LANES = 128
BF16_SUBLANES = 16
F32_SUBLANES = 8

BRANCH_W = 1024
N_BRANCH = 4
RWKV_HEAD = 64
LORA = 64
A_SHIFT_W = 3 * BRANCH_W + 4 * LORA
LNX_EPS = 64e-5
SG_CHUNK = 128
SG_GROUPS = 8
NA_HEAD = 64
NA_ROWS = 8
NA_COLS = 16
GRID_W = 64
MEM_HEADS = 4
MEM_HEAD = 256
WKV_CHUNK = 64
NEG_INF = -1e30

NORM_CHUNK = 512
WKV_PAIRS = 4
NA_UNROLL = 16


def _cparams(*sem):
    return pltpu.CompilerParams(dimension_semantics=sem, vmem_limit_bytes=VMEM_LIMIT_BYTES)


def _sigmoid(x):
    return 0.5 * jnp.tanh(0.5 * x) + 0.5


def _silu(x):
    return x * _sigmoid(x)


def _gelu_tanh(x):
    c = np.float32(np.sqrt(2.0 / np.pi))
    half = 0.5 * x
    return half + half * jnp.tanh(x * (c + np.float32(c * 0.044715) * (x * x)))


def _dot(a, b):
    return jnp.dot(a, b, preferred_element_type=F32)


def _dot_nt(a, b):
    return lax.dot_general(a, b, (((1,), (1,)), ((), ())), preferred_element_type=F32)


def _dot_tn(a, b):
    return lax.dot_general(a, b, (((0,), (0,)), ((), ())), preferred_element_type=F32)


def _split_bf16(x, terms):
    parts = []
    for _ in range(terms - 1):
        hi = x.astype(BF16)
        parts.append(hi)
        x = x - hi.astype(F32)
    parts.append(x.astype(BF16))
    return parts


def _head_block_ones(head):
    shift = int(np.log2(head))
    r = lax.broadcasted_iota(jnp.int32, (LANES, LANES), 0) >> shift
    c = lax.broadcasted_iota(jnp.int32, (LANES, LANES), 1) >> shift
    return jnp.where(r == c, 1.0, 0.0).astype(BF16)


def _head_sum(x, ones_bd):
    parts = [_dot(x[:, j:j + LANES].astype(BF16), ones_bd) for j in range(0, x.shape[1], LANES)]
    return parts[0] if len(parts) == 1 else jnp.concatenate(parts, axis=1)


def _rmsnorm_kernel(x_ref, g_ref, o_ref, *, eps):
    d = x_ref.shape[1]
    chunks = [slice(c, c + NORM_CHUNK) for c in range(0, d, NORM_CHUNK)]
    ssq = None
    for cols in chunks:
        x = x_ref[:, cols].astype(F32)
        s = jnp.sum(x * x, axis=-1, keepdims=True)
        ssq = s if ssq is None else ssq + s
    scale = lax.rsqrt(ssq * (1.0 / d) + eps)
    for cols in chunks:
        o_ref[:, cols] = (x_ref[:, cols].astype(F32) * scale * g_ref[:, cols]).astype(o_ref.dtype)


def _rmsnorm(x2d, g, out_dtype, tm=512, eps=1e-6):
    m, d = x2d.shape
    tm = min(tm, m)
    return pl.pallas_call(
        functools.partial(_rmsnorm_kernel, eps=eps),
        grid=(m // tm,),
        in_specs=[pl.BlockSpec((tm, d), lambda i: (i, 0)), pl.BlockSpec((1, d), lambda i: (0, 0))],
        out_specs=pl.BlockSpec((tm, d), lambda i: (i, 0)),
        out_shape=jax.ShapeDtypeStruct((m, d), out_dtype),
        compiler_params=_cparams("parallel"),
        name="rmsnorm",
    )(x2d, g.reshape(1, d))


def _mm_kernel(a_ref, b_ref, o_ref, *, act):
    acc = _dot(a_ref[...], b_ref[0])
    if act == "sigmoid":
        acc = _sigmoid(acc)
    o_ref[...] = acc.astype(o_ref.dtype)


def _matmul(a, w, layer, col0, n, out_dtype, tm, tn, act=None, name="matmul"):
    m, k = a.shape
    tm, tn = min(tm, m), min(tn, n)
    return pl.pallas_call(
        functools.partial(_mm_kernel, act=act),
        grid=(m // tm, n // tn),
        in_specs=[pl.BlockSpec((tm, k), lambda i, j: (i, 0)),
                  pl.BlockSpec((pl.Element(1), pl.Element(k), pl.Element(tn)),
                               lambda i, j: (layer, 0, pl.multiple_of(col0 + j * tn, LANES)))],
        out_specs=pl.BlockSpec((tm, tn), lambda i, j: (i, j)),
        out_shape=jax.ShapeDtypeStruct((m, n), out_dtype),
        compiler_params=_cparams("parallel", "parallel"),
        name=name,
    )(a, w)


def _mm_cast_kernel(a_ref, b_ref, src_ref, o_ref, dst_ref, *, act):
    _mm_kernel(a_ref, b_ref, o_ref, act=act)
    dst_ref[0] = src_ref[...].astype(dst_ref.dtype)


def _matmul_and_cast(a, w, layer, col0, n, out_dtype, tm, tn, src, src_layer, act=None, name="matmul_cast"):
    m, k = a.shape
    tm, tn = min(tm, m), min(tn, n)
    ni, nj = m // tm, n // tn
    rows, cols = src.shape[1] // (ni * nj), src.shape[2]
    assert rows * ni * nj == src.shape[1] and rows % BF16_SUBLANES == 0
    return pl.pallas_call(
        functools.partial(_mm_cast_kernel, act=act),
        grid=(ni, nj),
        in_specs=[pl.BlockSpec((tm, k), lambda i, j: (i, 0)),
                  pl.BlockSpec((pl.Element(1), pl.Element(k), pl.Element(tn)),
                               lambda i, j: (layer, 0, pl.multiple_of(col0 + j * tn, LANES))),
                  pl.BlockSpec((None, rows, cols), lambda i, j: (src_layer, i * nj + j, 0))],
        out_specs=[pl.BlockSpec((tm, tn), lambda i, j: (i, j)),
                   pl.BlockSpec((1, rows, cols), lambda i, j: (0, i * nj + j, 0))],
        out_shape=[jax.ShapeDtypeStruct((m, n), out_dtype),
                   jax.ShapeDtypeStruct((1, src.shape[1], cols), BF16)],
        compiler_params=_cparams("parallel", "parallel"),
        name=name,
    )(a, w, src)


def _mm_residual_kernel(a_ref, b_ref, x_ref, o_ref):
    o_ref[...] = x_ref[...] + _dot(a_ref[...], b_ref[...])


def _matmul_residual(a, w, layer, x, tm, tn):
    m, k = a.shape
    n = w.shape[2]
    tm, tn = min(tm, m), min(tn, n)
    return pl.pallas_call(
        _mm_residual_kernel,
        grid=(n // tn, m // tm),
        in_specs=[pl.BlockSpec((tm, k), lambda j, i: (i, 0)),
                  pl.BlockSpec((None, k, tn), lambda j, i: (layer, 0, j)),
                  pl.BlockSpec((tm, tn), lambda j, i: (i, j))],
        out_specs=pl.BlockSpec((tm, tn), lambda j, i: (i, j)),
        out_shape=jax.ShapeDtypeStruct((m, n), x.dtype),
        compiler_params=_cparams("parallel", "parallel"),
        name="out_proj_residual",
    )(a, w, x)


def _merge_kernel(y0, y1, y2, y3, wb_ref, g0, g1, g2, g3, o_ref):
    acc = None
    for n, (y, g) in enumerate(((y0, g0), (y1, g1), (y2, g2), (y3, g3))):
        term = g[...].astype(F32) * _dot(y[...], wb_ref[n])
        acc = term if acc is None else acc + term
    o_ref[...] = acc.astype(o_ref.dtype)


def _merge(ys, wb, layer, gates, tm, tn):
    m, bw = ys[0].shape
    d = wb.shape[3]
    tm, tn = min(tm, m), min(tn, d)
    nj = d // tn
    y_spec = pl.BlockSpec((tm, bw), lambda j, i: (i, 0))
    g_specs = [pl.BlockSpec((tm, tn), functools.partial(lambda j, i, n: (i, n * nj + j), n=n))
               for n in range(4)]
    return pl.pallas_call(
        _merge_kernel,
        grid=(nj, m // tm),
        in_specs=[y_spec] * 4 + [pl.BlockSpec((None, 4, bw, tn), lambda j, i: (layer, 0, 0, j))] + g_specs,
        out_specs=pl.BlockSpec((tm, tn), lambda j, i: (i, j)),
        out_shape=jax.ShapeDtypeStruct((m, d), BF16),
        compiler_params=_cparams("parallel", "parallel"),
        name="gated_merge",
    )(*ys, wb, gates, gates, gates, gates)


def _token_shift(h_ref, hp_ref, hn_ref, conv_ref, cols):
    i = pl.program_id(1)
    nt = pl.num_programs(1)
    h = h_ref[0, :, cols].astype(F32)
    tt = h.shape[0]
    prev_row = hp_ref[0, BF16_SUBLANES - 1:BF16_SUBLANES, cols].astype(F32) * (i > 0).astype(F32)
    next_row = hn_ref[0, 0:1, cols].astype(F32) * (i < nt - 1).astype(F32)
    rows = lax.broadcasted_iota(jnp.int32, (F32_SUBLANES, 1), 0)
    dn, up = pltpu.roll(h, 1, 0), pltpu.roll(h, tt - 1, 0)
    h_dn = jnp.concatenate([jnp.where(rows == 0, prev_row, dn[0:F32_SUBLANES]), dn[F32_SUBLANES:]], axis=0)
    h_up = jnp.concatenate([up[0:tt - F32_SUBLANES],
                            jnp.where(rows == F32_SUBLANES - 1, next_row, up[tt - F32_SUBLANES:])], axis=0)
    return h_dn * conv_ref[0:1, cols] + h * conv_ref[1:2, cols] + h_up * conv_ref[2:3, cols]


def _rwkv_prep_kernel(h_ref, hp_ref, hn_ref, l_ref, lp_ref, ln_ref, conv_ref, convl_ref, wup_ref, w0_ref,
                      aup_ref, a0_ref, kk_ref, ka_ref, rk_ref,
                      r_o, v_o, kk_o, lw0_o, lw1_o, kt0_o, kt1_o, b0_o, b1_o, bonus_o):
    bw = BRANCH_W
    ls = _token_shift(l_ref, lp_ref, ln_ref, convl_ref, slice(0, 4 * LORA))
    wd_pieces = _split_bf16(jnp.tanh(ls[:, 0:2 * LORA]), 2)
    ad_pieces = _split_bf16(ls[:, 2 * LORA:4 * LORA], 2)
    ones_bd = _head_block_ones(RWKV_HEAD)
    decay_scale = np.float32(np.exp(-0.5))

    def up_proj(x_pieces, w_ref, z, cols):
        (x_hi, x_lo), w_hi, w_lo = x_pieces, w_ref[0, z, :, cols], w_ref[1, z, :, cols]
        return _dot(x_hi, w_hi) + (_dot(x_lo, w_hi) + _dot(x_hi, w_lo))

    for j in range(bw // LANES):
        cols = slice(j * LANES, (j + 1) * LANES)
        r, k, v = (_token_shift(h_ref, hp_ref, hn_ref, conv_ref, slice(s * bw + j * LANES, s * bw + (j + 1) * LANES))
                   for s in range(3))
        kkr = k * kk_ref[:, cols]
        kk = kkr * lax.rsqrt(jnp.maximum(_head_sum(kkr * kkr, ones_bd), 1e-24))
        kts = []
        for z, (lw_o, kt_o, b_o) in enumerate(((lw0_o, kt0_o, b0_o), (lw1_o, kt1_o, b1_o))):
            w_raw = w0_ref[z:z + 1, cols] + up_proj(wd_pieces, wup_ref, z, cols)
            lw_o[0, :, cols] = -decay_scale * _sigmoid(w_raw)
            a = _sigmoid(a0_ref[z:z + 1, cols] + up_proj(ad_pieces, aup_ref, z, cols))
            kt = k * (1.0 + (a - 1.0) * ka_ref[:, cols])
            kt_o[0, :, cols] = kt
            b_o[0, :, cols] = kk * a
            kts.append(kt)
        r_o[0, :, cols] = r
        v_o[0, :, cols] = v
        kk_o[0, :, cols] = kk
        bonus_o[0, :, cols] = _head_sum(r * (kts[0] + kts[1]) * rk_ref[:, cols], ones_bd) * v


def _rwkv_prep(h_rkv, h_lora, conv, wup_pad, w0, aup_pad, a0, k_k, k_a, r_k, tt=256):
    bsz, t, w = h_rkv.shape
    wl = h_lora.shape[2]
    tt = min(tt, t)
    nt = t // tt
    hb = tt // BF16_SUBLANES
    n_halo = t // BF16_SUBLANES
    row = lambda a: a.reshape(1, -1)
    vec_spec = pl.BlockSpec((1, BRANCH_W), lambda b, i: (0, 0))
    out_spec = pl.BlockSpec((1, tt, BRANCH_W), lambda b, i: (b, i, 0))
    out_sds = jax.ShapeDtypeStruct((bsz, t, BRANCH_W), F32)
    tile = lambda width: [
        pl.BlockSpec((1, tt, width), lambda b, i: (b, i, 0)),
        pl.BlockSpec((1, BF16_SUBLANES, width), lambda b, i: (b, jnp.maximum(i * hb - 1, 0), 0)),
        pl.BlockSpec((1, BF16_SUBLANES, width), lambda b, i: (b, jnp.minimum((i + 1) * hb, n_halo - 1), 0))]
    return pl.pallas_call(
        _rwkv_prep_kernel,
        grid=(bsz, nt),
        in_specs=tile(w) + tile(wl) + [
            pl.BlockSpec((3, w), lambda b, i: (0, 0)),
            pl.BlockSpec((3, wl), lambda b, i: (0, 0)),
            pl.BlockSpec((2, 2, 2 * LORA, BRANCH_W), lambda b, i: (0, 0, 0, 0)),
            pl.BlockSpec((2, BRANCH_W), lambda b, i: (0, 0)),
            pl.BlockSpec((2, 2, 2 * LORA, BRANCH_W), lambda b, i: (0, 0, 0, 0)),
            pl.BlockSpec((2, BRANCH_W), lambda b, i: (0, 0)),
            vec_spec, vec_spec, vec_spec,
        ],
        out_specs=[out_spec] * 10,
        out_shape=[out_sds] * 10,
        compiler_params=_cparams("parallel", "parallel"),
        name="rwkv_prep",
    )(h_rkv, h_rkv, h_rkv, h_lora, h_lora, h_lora, conv[:, :w], conv[:, w:], wup_pad, w0, aup_pad, a0,
      row(k_k), row(k_a), row(r_k))


def _wkv_masks():
    c = WKV_CHUNK
    row = lax.broadcasted_iota(jnp.int32, (2 * c, 2 * c), 0)
    col = lax.broadcasted_iota(jnp.int32, (2 * c, 2 * c), 1)
    same = (row >> 6) == (col >> 6)
    rt, ct = row & (c - 1), col & (c - 1)
    f = lambda m: jnp.where(same & m, 1.0, 0.0).astype(F32)
    lane = lax.broadcasted_iota(jnp.int32, (1, LANES), 1)
    crow = lax.broadcasted_iota(jnp.int32, (c, c), 0)
    ccol = lax.broadcasted_iota(jnp.int32, (c, c), 1)
    return {
        "eye": f(rt == ct),
        "strict": (f(ct < rt), f(ct > rt)),
        "incl": (f(ct <= rt), f(ct >= rt)),
        "cum": (jnp.where(ccol <= crow, 1.0, 0.0).astype(BF16), jnp.where(ccol >= crow, 1.0, 0.0).astype(BF16)),
        "head0": jnp.where(lane < RWKV_HEAD, 1.0, 0.0).astype(F32),
        "head1": jnp.where(lane < RWKV_HEAD, 0.0, 1.0).astype(F32),
    }


def _wkv_local(problems, masks, fillers=()):
    c = WKV_CHUNK
    bf = lambda x: x.astype(BF16)
    m0, m1 = masks["head0"], masks["head1"]
    pair = lambda x: jnp.concatenate([x * m0, x * m1], axis=0)
    each = lambda fn, *lists: [fn(*xs) for xs in zip(*lists)]
    dirs = [p[6] for p in problems]
    fillers = list(fillers)
    n_points = 16
    stride = max(1, n_points // max(1, len(fillers)))
    seen = [0]

    def fill():
        seen[0] += 1
        if fillers and seen[0] % stride == 0:
            fillers.pop(0)()

    cl = [_cumsum_dot(masks["cum"][p[6]], p[0]) for p in problems]
    tot = [x[c - 1:c, :] if d == 0 else x[0:1, :] for x, d in zip(cl, dirs)]
    fill()
    zp = [pair(-p[4] * jnp.exp(x - p[0])) for p, x in zip(problems, cl)]
    rp = [pair(p[1] * jnp.exp(x)) for p, x in zip(problems, cl)]
    vpb = [bf(pair(p[3])) for p in problems]
    e_neg = [jnp.exp(-x) for x in cl]
    bk_start = [bf(jnp.concatenate([pair(p[5] * e), pair(p[2] * e)], axis=0)) for p, e in zip(problems, e_neg)]
    e_end = [jnp.exp(t - x) for t, x in zip(tot, cl)]
    bk_end = [bf(jnp.concatenate([pair(p[5] * e), pair(p[2] * e)], axis=0)) for p, e in zip(problems, e_end)]
    fill()

    scores = each(lambda z, r, bk: _dot_nt(bf(jnp.concatenate([z, r], axis=0)), bk), zp, rp, bk_start)
    fill()
    strict = [masks["strict"][d] for d in dirs]
    incl = [masks["incl"][d] for d in dirs]
    l_zb = each(lambda s, m: s[0:2 * c, 0:2 * c] * m, scores, strict)
    a_zk = each(lambda s, m: bf(s[0:2 * c, 2 * c:4 * c] * m), scores, strict)
    a_r = each(lambda s, m: bf(jnp.concatenate([s[2 * c:4 * c, 0:2 * c] * m, s[2 * c:4 * c, 2 * c:4 * c] * m],
                                               axis=1)), scores, incl)
    azk_v = each(_dot, a_zk, vpb)
    fill()

    inv = [masks["eye"] + l for l in l_zb]
    pb = [bf(l) for l in l_zb]
    for _ in range(5):
        pb = [bf(_dot(x, x)) for x in pb]
        fill()
        inv = each(lambda t, x: t + _dot(bf(t), x), inv, pb)
        fill()

    zu = each(lambda t, z, u: _dot(bf(t), bf(jnp.concatenate([z, u], axis=1))), inv, zp, azk_v)
    fill()
    stack = each(lambda x, v: jnp.concatenate([bf(x), jnp.concatenate([jnp.zeros_like(v), v], axis=1)], axis=0),
                 zu, vpb)
    ry = each(_dot, a_r, stack)
    fill()
    gh = each(_dot_tn, stack, bk_end)
    while fillers:
        fillers.pop(0)()
    out = []
    for r, y, g, t in zip(rp, ry, gh, tot):
        rb = r + y[:, 0:2 * c]
        out.append((rb[0:c] + rb[c:2 * c], y[0:c, 2 * c:4 * c] + y[c:2 * c, 2 * c:4 * c],
                    g[0:2 * c], g[2 * c:4 * c], jnp.exp(t)))
    return out


def _cumsum_dot(cum_bf16, lw):
    acc = None
    for piece in _split_bf16(lw, 2):
        d = _dot(cum_bf16, piece)
        acc = d if acc is None else acc + d
    return acc


def _wkv_kernel(r_f, v_f, kk_f, lw_f, kt_f, b_f, r_b, v_b, kk_b, lw_b, kt_b, b_b, yf_o, yb_o,
                s_ref, rb_ref, g_ref, h_ref, wc_ref):
    @pl.when(pl.program_id(2) == 0)
    def _():
        s_ref[...] = jnp.zeros_like(s_ref)

    c = WKV_CHUNK
    n_chunks = r_f.shape[1] // c
    half = n_chunks // 2
    masks = _wkv_masks()
    ins = ((lw_f, r_f, kt_f, v_f, kk_f, b_f), (lw_b, r_b, kt_b, v_b, kk_b, b_b))
    outs = (yf_o, yb_o)
    streams = [(d, p) for d in range(2) for p in range(WKV_PAIRS)]
    states = [s_ref[q] for q in range(len(streams))]
    chunk_of = lambda d, step: step if d == 0 else n_chunks - 1 - step
    lanes = lambda p: slice(p * LANES, (p + 1) * LANES)

    def local_factors(steps, fillers):
        where = [(q, chunk_of(streams[q][0], s)) for s in steps for q in range(len(streams))]
        problems = [[ref[0, ci * c:(ci + 1) * c, lanes(streams[q][1])] for ref in ins[streams[q][0]]]
                    + [streams[q][0]] for q, ci in where]
        for (q, ci), (rb, yloc, g, h, wc) in zip(where, _wkv_local(problems, masks, fillers)):
            d, p = streams[q]
            rb_ref[q, ci * c:(ci + 1) * c, :] = rb.astype(BF16)
            outs[d][0, ci * c:(ci + 1) * c, lanes(p)] = yloc
            g_ref[q, ci] = g.astype(BF16)
            h_ref[q, ci] = h
            wc_ref[q, ci] = jnp.broadcast_to(wc, (8, LANES))

    def state_step(step):
        for q, (d, p) in enumerate(streams):
            ci = chunk_of(d, step)
            rows = slice(ci * c, (ci + 1) * c)
            sb = states[q].astype(BF16)
            outs[d][0, rows, lanes(p)] = outs[d][0, rows, lanes(p)] + _dot_nt(rb_ref[q, rows, :], sb)
            states[q] = states[q] * wc_ref[q, ci, 0:1, :] + _dot(sb, g_ref[q, ci]) + h_ref[q, ci]

    local_factors(range(0, half), ())
    local_factors(range(half, n_chunks), [functools.partial(state_step, s) for s in range(half)])
    for s in range(half, n_chunks):
        state_step(s)
    for q in range(len(streams)):
        s_ref[q] = states[q]


def _wkv(r, v, kk, lw0, lw1, kt0, kt1, b0, b1, tb=256):
    bsz, t, w = r.shape
    tb = min(tb, t)
    nb = t // tb
    nc = tb // WKV_CHUNK
    wl = WKV_PAIRS * LANES
    ns = 2 * WKV_PAIRS
    fwd = pl.BlockSpec((1, tb, wl), lambda b, h, g: (b, g, h))
    bwd = pl.BlockSpec((1, tb, wl), lambda b, h, g: (b, nb - 1 - g, h))
    sds = jax.ShapeDtypeStruct((bsz, t, w), F32)
    return pl.pallas_call(
        _wkv_kernel,
        grid=(bsz, w // wl, nb),
        in_specs=[fwd] * 6 + [bwd] * 6,
        out_specs=[fwd, bwd],
        out_shape=[sds, sds],
        scratch_shapes=[pltpu.VMEM((ns, LANES, LANES), F32), pltpu.VMEM((ns, tb, LANES), BF16),
                        pltpu.VMEM((ns, nc, LANES, LANES), BF16), pltpu.VMEM((ns, nc, LANES, LANES), F32),
                        pltpu.VMEM((ns, nc, 8, LANES), F32)],
        compiler_params=_cparams("parallel", "parallel", "arbitrary"),
        name="wkv7_chunked",
    )(r, v, kk, lw0, kt0, b0, r, v, kk, lw1, kt1, b1)


def _rwkv_post_kernel(yf_ref, yb_ref, bonus_ref, g_ref, lw_ref, lb_ref, o_ref):
    ones_bd = _head_block_ones(RWKV_HEAD)
    inv_n = 1.0 / RWKV_HEAD
    for j in range(o_ref.shape[2] // LANES):
        cols = slice(j * LANES, (j + 1) * LANES)
        wkv = yf_ref[0, :, cols] + yb_ref[0, :, cols]
        mu = _head_sum(wkv, ones_bd) * inv_n
        d = wkv - mu
        var = _head_sum(d * d, ones_bd) * inv_n
        gn = d * lax.rsqrt(var + LNX_EPS) * lw_ref[:, cols] + lb_ref[:, cols]
        gate = _silu(g_ref[0, :, cols].astype(F32))
        o_ref[0, :, cols] = ((gn + bonus_ref[0, :, cols]) * gate).astype(o_ref.dtype)


def _rwkv_post(yf, yb, bonus, g, lnx_w, lnx_b, tt=512):
    bsz, t, w = yf.shape
    tt = min(tt, t)
    spec = pl.BlockSpec((1, tt, w), lambda b, i: (b, i, 0))
    vec = pl.BlockSpec((1, w), lambda b, i: (0, 0))
    return pl.pallas_call(
        _rwkv_post_kernel,
        grid=(bsz, t // tt),
        in_specs=[spec, spec, spec, spec, vec, vec],
        out_specs=spec,
        out_shape=jax.ShapeDtypeStruct((bsz, t, w), BF16),
        compiler_params=_cparams("parallel", "parallel"),
        name="rwkv_post",
    )(yf, yb, bonus, g, lnx_w.reshape(1, w), lnx_b.reshape(1, w))


def _sgu_kernel(h_ref, lg_ref, lb_ref, ws_ref, bs_ref, o_ref):
    bw = BRANCH_W
    tt = h_ref.shape[1]
    u = _gelu_tanh(h_ref[0, :, 0:bw].astype(F32))
    vv = _gelu_tanh(h_ref[0, :, bw:2 * bw].astype(F32))
    g = h_ref[0, :, 2 * bw:3 * bw].astype(F32)
    mu = jnp.mean(vv, axis=-1, keepdims=True)
    d = vv - mu
    var = jnp.mean(d * d, axis=-1, keepdims=True)
    vn = (d * lax.rsqrt(var + 1e-5) * lg_ref[...] + lb_ref[...]).astype(BF16)
    gate = u * _silu(g)
    for ck in range(tt // SG_CHUNK):
        rs = slice(ck * SG_CHUNK, (ck + 1) * SG_CHUNK)
        for grp in range(SG_GROUPS):
            cs = slice(grp * LANES, (grp + 1) * LANES)
            sv = _dot(ws_ref[grp], vn[rs, cs]) + bs_ref[:, cs]
            o_ref[0, rs, cs] = (gate[rs, cs] * sv).astype(o_ref.dtype)


def _sgu(h_b, ln_g, ln_b, w_s, bs_cols, tt=512):
    bsz, t, w3 = h_b.shape
    bw = BRANCH_W
    tt = min(tt, t)
    vec = pl.BlockSpec((1, bw), lambda b, i: (0, 0))
    return pl.pallas_call(
        _sgu_kernel,
        grid=(bsz, t // tt),
        in_specs=[pl.BlockSpec((1, tt, w3), lambda b, i: (b, i, 0)), vec, vec,
                  pl.BlockSpec((SG_GROUPS, SG_CHUNK, SG_CHUNK), lambda b, i: (0, 0, 0)),
                  pl.BlockSpec((SG_CHUNK, bw), lambda b, i: (0, 0))],
        out_specs=pl.BlockSpec((1, tt, bw), lambda b, i: (b, i, 0)),
        out_shape=jax.ShapeDtypeStruct((bsz, t, bw), BF16),
        compiler_params=_cparams("parallel", "parallel"),
        name="spatial_gating",
    )(h_b, ln_g.reshape(1, bw), ln_b.reshape(1, bw), w_s, bs_cols)


def _natten_bias_table(rpb):
    p = np.arange(GRID_W)[:, None]
    m = np.arange(GRID_W)[None, :]
    sj = np.clip(p - NA_COLS // 2, 0, GRID_W - NA_COLS)
    valid = (m >= sj) & (m < sj + NA_COLS)
    dc = np.clip(m - p, -(NA_COLS - 1), NA_COLS - 1) + NA_COLS - 1
    by_rel = jnp.where(valid[None, None], rpb[:, :, dc], NEG_INF)
    tab = jnp.stack([by_rel[:, s:s + NA_ROWS] for s in range(NA_ROWS)], axis=1)
    tab = tab.transpose(0, 1, 3, 2, 4)
    return tab.reshape(rpb.shape[0], NA_ROWS, GRID_W, NA_ROWS * GRID_W).astype(BF16)


def _natten_kernel(q_ref, k_ref, v_ref, g_ref, qn_ref, kn_ref, bias_ref, o_ref, qs_ref, ks_ref):
    t = q_ref.shape[1]
    n_rows = t // GRID_W
    win = NA_ROWS * GRID_W
    ones_bd = _head_block_ones(NA_HEAD)
    lane = lax.broadcasted_iota(jnp.int32, (1, LANES), 1)
    m0 = lane < NA_HEAD

    def norm(x_ref, gain_ref, scale):
        x = x_ref[0].astype(F32)
        ms = _head_sum(x * x, ones_bd) * (1.0 / NA_HEAD)
        return x * lax.rsqrt(ms + 1e-6) * (gain_ref[...] * scale)

    qs_ref[...] = norm(q_ref, qn_ref, NA_HEAD ** -0.5).astype(BF16)
    ks_ref[...] = norm(k_ref, kn_ref, 1.0).astype(BF16)

    def body(it, carry):
        rows = [it * NA_UNROLL + j for j in range(NA_UNROLL)]
        si = [jnp.clip(i - NA_ROWS // 2, 0, n_rows - NA_ROWS) for i in rows]
        start = [s - i + (NA_ROWS - 1) for s, i in zip(si, rows)]
        qo = [pl.multiple_of(i * GRID_W, GRID_W) for i in rows]
        ko = [pl.multiple_of(s * GRID_W, GRID_W) for s in si]
        q = [qs_ref[pl.ds(o, GRID_W), :] for o in qo]
        zero = jnp.zeros_like(q[0])
        q2 = [jnp.concatenate([jnp.where(m0, x, zero), jnp.where(m0, zero, x)], axis=0) for x in q]
        s = [_dot_nt(x, ks_ref[pl.ds(o, win), :]) for x, o in zip(q2, ko)]
        s = [x + jnp.concatenate([bias_ref[0, st], bias_ref[1, st]], axis=0).astype(F32) for x, st in zip(s, start)]
        e = [jnp.exp(x - jnp.max(x, axis=-1, keepdims=True)).astype(BF16) for x in s]
        o2 = [_dot(x, jnp.concatenate([v_ref[0, pl.ds(o, win), :], ones_blk], axis=1))
              for x, o in zip(e, ko)]
        for x, o in zip(o2, qo):
            x = x[:, 0:LANES] / x[:, LANES:2 * LANES]
            g = g_ref[0, pl.ds(o, GRID_W), :].astype(F32)
            val = jnp.where(m0, x[0:GRID_W], x[GRID_W:2 * GRID_W]) * _silu(g)
            o_ref[0, pl.ds(o, GRID_W), :] = val.astype(o_ref.dtype)
        return carry

    ones_blk = jnp.ones((win, LANES), BF16)

    lax.fori_loop(0, n_rows // NA_UNROLL, body, 0)


def _natten(h_c, q_norm, k_norm, bias_tab):
    bsz, t, w4 = h_c.shape
    bw = BRANCH_W
    nlb = bw // LANES
    sec = lambda s: pl.BlockSpec((1, t, LANES), functools.partial(lambda b, hp, s: (b, 0, s * nlb + hp), s=s))
    two = lambda a: jnp.concatenate([a, a]).reshape(1, LANES)
    return pl.pallas_call(
        _natten_kernel,
        grid=(bsz, nlb),
        in_specs=[sec(0), sec(1), sec(2), sec(3),
                  pl.BlockSpec((1, LANES), lambda b, hp: (0, 0)), pl.BlockSpec((1, LANES), lambda b, hp: (0, 0)),
                  pl.BlockSpec((2, NA_ROWS, GRID_W, NA_ROWS * GRID_W), lambda b, hp: (hp, 0, 0, 0))],
        out_specs=pl.BlockSpec((1, t, LANES), lambda b, hp: (b, 0, hp)),
        out_shape=jax.ShapeDtypeStruct((bsz, t, bw), BF16),
        scratch_shapes=[pltpu.VMEM((t, LANES), BF16), pltpu.VMEM((t, LANES), BF16)],
        compiler_params=_cparams("parallel", "parallel"),
        name="neighbourhood_attention",
    )(h_c, h_c, h_c, h_c, two(q_norm), two(k_norm), bias_tab)


def _memattn_kernel(qg_ref, kv_ref, qn_ref, kn_ref, o_ref):
    hd, bw = MEM_HEAD, BRANCH_W

    def norm(x, gain, scale):
        ms = jnp.mean(x * x, axis=-1, keepdims=True)
        return (x * lax.rsqrt(ms + 1e-6) * (gain * scale)).astype(BF16)

    heads = [slice(h * hd, (h + 1) * hd) for h in range(MEM_HEADS)]
    q = [norm(qg_ref[0, :, c].astype(F32), qn_ref[...], hd ** -0.5) for c in heads]
    k = [norm(kv_ref[0, :, c].astype(F32), kn_ref[...], 1.0) for c in heads]
    s = [_dot_nt(a, b) for a, b in zip(q, k)]
    e = [jnp.exp(x - jnp.max(x, axis=-1, keepdims=True)) for x in s]
    p = [(x / jnp.sum(x, axis=-1, keepdims=True)).astype(BF16) for x in e]
    o = [_dot(x, kv_ref[0, :, bw + h * hd:bw + (h + 1) * hd]) for h, x in enumerate(p)]
    for c, x in zip(heads, o):
        g = qg_ref[0, :, bw + c.start:bw + c.stop].astype(F32)
        o_ref[0, :, c] = (x * _silu(g)).astype(o_ref.dtype)


def _memattn(h_m, kv, q_norm, k_norm, tt=512):
    bsz, t, w2 = h_m.shape
    mlen = kv.shape[1]
    tt = min(tt, t)
    vec = pl.BlockSpec((1, MEM_HEAD), lambda b, i: (0, 0))
    return pl.pallas_call(
        _memattn_kernel,
        grid=(bsz, t // tt),
        in_specs=[pl.BlockSpec((1, tt, w2), lambda b, i: (b, i, 0)),
                  pl.BlockSpec((1, mlen, w2), lambda b, i: (b, 0, 0)),
                  vec, vec],
        out_specs=pl.BlockSpec((1, tt, BRANCH_W), lambda b, i: (b, i, 0)),
        out_shape=jax.ShapeDtypeStruct((bsz, t, BRANCH_W), BF16),
        compiler_params=_cparams("parallel", "parallel"),
        name="memory_attention",
    )(h_m, kv, q_norm.reshape(1, MEM_HEAD), k_norm.reshape(1, MEM_HEAD))


def _pad_lora(up):
    z = jnp.zeros_like(up[0])
    w = jnp.stack([jnp.concatenate([up[0], z], axis=0), jnp.concatenate([z, up[1]], axis=0)])
    hi = w.astype(BF16)
    return jnp.stack([hi, (w - hi.astype(F32)).astype(BF16)])


def _layer(x2d, mem2d, bsz, layer, p, big, w_in_bf16, w_in_f32):
    m, d = x2d.shape
    t = m // bsz
    bw = BRANCH_W
    a_w = A_SHIFT_W + bw
    o1, o2, o3, o4 = a_w, a_w + 3 * bw, a_w + 7 * bw, a_w + 9 * bw
    xn = _rmsnorm(x2d, p["norm_g"], BF16)
    proj = functools.partial(_matmul, xn, w_in_bf16, 0)
    proj_cast = functools.partial(_matmul_and_cast, xn, w_in_bf16, 0)
    n_layers = w_in_f32.shape[0]

    h_rkv = proj(0, 3 * bw, BF16, 1024, 1024, name="proj_a_rkv")
    h_lora = proj(3 * bw, 4 * LORA, BF16, 1024, 4 * LORA, name="proj_a_lora")
    h_ag, w_kv = proj_cast(A_SHIFT_W, bw, BF16, 1024, 1024, big["m_w_kv"], layer, name="proj_a_gate_cast_kv")
    h_b = proj(o1, 3 * bw, BF16, 1024, 1024, name="proj_b")
    h_c, w_br = proj_cast(o2, 4 * bw, BF16, 1024, 1024, big["w_branch"].reshape(n_layers, N_BRANCH * bw, d),
                          layer, name="proj_c_cast_branch")
    h_m, w_o = proj_cast(o3, 2 * bw, BF16, 1024, 1024, big["w_out"], layer, name="proj_m_cast_out")
    if layer + 1 < n_layers:
        gates, w_next = _matmul_and_cast(xn, w_in_bf16, 0, o4, 4 * d, BF16, 1024, 1024, w_in_f32, layer + 1,
                                         act="sigmoid", name="proj_gates_cast_next")
    else:
        gates, w_next = proj(o4, 4 * d, BF16, 1024, 1024, act="sigmoid", name="proj_gates"), None

    prep = _rwkv_prep(h_rkv.reshape(bsz, t, 3 * bw), h_lora.reshape(bsz, t, 4 * LORA), p["a_conv"],
                      _pad_lora(p["a_w_up"]), p["a_w0"], _pad_lora(p["a_a_up"]), p["a_a0"],
                      p["a_k_k"], p["a_k_a"], p["a_r_k"].reshape(-1))
    r, v, kk, lw0, lw1, kt0, kt1, b0, b1, bonus = prep
    yf, yb = _wkv(r, v, kk, lw0, lw1, kt0, kt1, b0, b1)
    y_a = _rwkv_post(yf, yb, bonus, h_ag.reshape(bsz, t, bw), p["a_lnx_w"], p["a_lnx_b"])

    bs_cols = jnp.repeat(p["b_b_s"].T, SG_CHUNK, axis=1)
    y_b = _sgu(h_b.reshape(bsz, t, 3 * bw), p["b_ln_g"], p["b_ln_b"], p["b_w_s"].astype(BF16), bs_cols)

    y_c = _natten(h_c.reshape(bsz, t, 4 * bw), p["c_q_norm"], p["c_k_norm"], _natten_bias_table(p["c_rpb"]))

    mem_n = _rmsnorm(mem2d, p["m_norm_g"], BF16)
    kv = _matmul(mem_n, w_kv, 0, 0, 2 * bw, BF16, 1024, 1024, name="proj_mem_kv")
    y_d = _memattn(h_m.reshape(bsz, t, 2 * bw), kv.reshape(bsz, -1, 2 * bw), p["m_q_norm"], p["m_k_norm"])

    ys = [y.reshape(m, bw) for y in (y_a, y_b, y_c, y_d)]
    merged = _merge(ys, w_br.reshape(1, N_BRANCH, bw, d), 0, gates, 512, 1024)
    return _matmul_residual(merged, w_o, 0, x2d, 1024, 1024), w_next


def kernel(x, mem, norm_g, w_in, a_conv, a_w_up, a_w0, a_a_up, a_a0, a_k_k, a_k_a, a_r_k, a_lnx_w, a_lnx_b,
           b_ln_g, b_ln_b, b_w_s, b_b_s, c_q_norm, c_k_norm, c_rpb, m_norm_g, m_w_kv, m_q_norm, m_k_norm,
           w_branch, w_out):
    params = dict(norm_g=norm_g, a_conv=a_conv, a_w_up=a_w_up, a_w0=a_w0, a_a_up=a_a_up, a_a0=a_a0,
                  a_k_k=a_k_k, a_k_a=a_k_a, a_r_k=a_r_k, a_lnx_w=a_lnx_w, a_lnx_b=a_lnx_b, b_ln_g=b_ln_g,
                  b_ln_b=b_ln_b, b_w_s=b_w_s, b_b_s=b_b_s, c_q_norm=c_q_norm, c_k_norm=c_k_norm, c_rpb=c_rpb,
                  m_norm_g=m_norm_g, m_q_norm=m_q_norm, m_k_norm=m_k_norm)
    big = dict(m_w_kv=m_w_kv, w_branch=w_branch, w_out=w_out)
    bsz, t, d = x.shape
    x2d = x.reshape(bsz * t, d)
    mem2d = mem.reshape(-1, d)
    w_cur = w_in[0:1].astype(BF16)
    for l in range(norm_g.shape[0]):
        x2d, w_cur = _layer(x2d, mem2d, bsz, l, {k: v[l] for k, v in params.items()}, big, w_cur, w_in)
    return x2d.reshape(bsz, t, d)
```
